```python
import jax, jax.numpy as jnp
from jax import lax
import numpy as np

D_MODEL = 1024
BATCH = 2
SEQ = 8192
DEPTH = 2
DEC_BATCH = 32
DEC_SEQ = 32
PAST_LEN = 1024

CHUNK = 64
A_HEADS = 8
A_HEAD_DIM = 64
DA = A_HEADS * A_HEAD_DIM
W_LORA = 64
A_LORA = 64
V_LORA = 32
G_LORA = 160
RWKV_COLS = 3 * DA + W_LORA + A_LORA + G_LORA
B_HEADS = 4
B_KEY_DIM = 128
B_VAL_DIM = 128
BK = B_HEADS * B_KEY_DIM
DB = B_HEADS * B_VAL_DIM
HGRN_COLS = 2 * BK + 2 * DB
GATE_COLS = 2 * D_MODEL
P_TOTAL = RWKV_COLS + HGRN_COLS + GATE_COLS
D_FF = 4 * D_MODEL
HGRN_BLOCK = 16
RMS_EPS = 1e-6
GN_EPS = 64e-5

kernel_name = 'rwkv7_hgrn2_gated_hybrid_step'


def rmsnorm(x, w):
    xf = x.astype(jnp.float32)
    y = xf * lax.rsqrt(jnp.mean(xf * xf, axis=-1, keepdims=True) + RMS_EPS)
    return (y * w.astype(jnp.float32)).astype(x.dtype)


def rwkv7_recurrence(r, w, k, v, a, b, s0):
    f32 = jnp.float32
    xs = tuple(jnp.moveaxis(t.astype(f32), 1, 0) for t in (r, w, k, v, a, b))

    def step(s, inp):
        r_t, w_t, k_t, v_t, a_t, b_t = inp
        sa = jnp.einsum('bhvk,bhk->bhv', s, a_t)
        s = s * w_t[:, :, None, :] + sa[..., None] * b_t[:, :, None, :] + v_t[..., None] * k_t[:, :, None, :]
        return s, jnp.einsum('bhvk,bhk->bhv', s, r_t)

    s, ys = lax.scan(step, s0.astype(f32), xs)
    return jnp.moveaxis(ys, 0, 1), s


def hgrn2_chunked(q, k, v, logf, s0):
    f32 = jnp.float32
    bsz, t_len, n_h, _ = q.shape
    v_dim = v.shape[-1]
    pad = (-t_len) % HGRN_BLOCK

    def prep(t):
        t = jnp.pad(t.astype(f32), ((0, 0), (0, pad), (0, 0), (0, 0)))
        t = t.reshape(bsz, -1, HGRN_BLOCK, n_h, t.shape[-1])
        return jnp.transpose(t, (1, 0, 3, 2, 4))

    mask = jnp.tril(jnp.ones((HGRN_BLOCK, HGRN_BLOCK), dtype=bool))[:, :, None]

    def step(s, inp):
        qc, kc, vc, gc = inp
        g_cum = jnp.cumsum(gc, axis=2)
        diff = g_cum[:, :, :, None, :] - g_cum[:, :, None, :, :]
        decay = jnp.exp(jnp.where(mask, diff, -jnp.inf))
        att = jnp.einsum('bhtk,bhtsk,bhsk->bhts', qc, decay, kc)
        o = jnp.einsum('bhtk,bhkv->bhtv', qc * jnp.exp(g_cum), s) + jnp.einsum('bhts,bhsv->bhtv', att, vc)
        g_last = g_cum[:, :, -1:, :]
        s = jnp.exp(g_last[:, :, 0, :])[..., None] * s + jnp.einsum('bhsk,bhsv->bhkv', kc * jnp.exp(g_last - g_cum), vc)
        return s, o

    s, o = lax.scan(step, s0.astype(f32), (prep(q), prep(k), prep(v), prep(logf)))
    o = jnp.transpose(o, (1, 0, 3, 2, 4)).reshape(bsz, -1, n_h, v_dim)[:, :t_len]
    return o, s


def token_mixer(h, l, lb, shift_prev, s_rwkv, s_hgrn, v_first, p):
    f32 = jnp.float32
    bsz, t_len, _ = h.shape
    proj = h @ p['w_in'][l]
    rw = proj[..., :RWKV_COLS]
    hg = proj[..., RWKV_COLS:RWKV_COLS + HGRN_COLS]
    gates = jax.nn.sigmoid(proj[..., RWKV_COLS + HGRN_COLS:])

    prev = jnp.concatenate([shift_prev[:, None, :].astype(rw.dtype), rw[:, :-1]], axis=1)
    rw_mix = rw + (prev - rw) * p['rwkv_mu'][l]
    r, k, v, wl, al, gl = jnp.split(rw_mix, [DA, 2 * DA, 3 * DA, 3 * DA + W_LORA, 3 * DA + W_LORA + A_LORA], axis=-1)
    w_raw = (p['rwkv_w0'][l] + jnp.tanh(wl) @ p['rwkv_w2'][l]).astype(f32)
    decay = jnp.exp(-jnp.exp(-jax.nn.softplus(-w_raw) - 0.5))
    a = jax.nn.sigmoid((p['rwkv_a0'][l] + al @ p['rwkv_a2'][l]).astype(f32))
    g = jax.nn.sigmoid(gl) @ p['rwkv_g2'][l]
    if l == 0:
        v_first = v
    else:
        vg = jax.nn.sigmoid(p['rwkv_v0'][l - 1] + (v @ p['rwkv_vres_w1'][l - 1]) @ p['rwkv_vres_w2'][l - 1])
        v = v + (v_first - v) * vg

    def heads(t):
        return t.reshape(bsz, t_len, A_HEADS, A_HEAD_DIM).astype(f32)

    r_h, v_h, a_h, w_h = heads(r), heads(v), heads(a), heads(decay)
    kk = heads(k * p['rwkv_k_k'][l])
    kk = kk * lax.rsqrt(jnp.maximum(jnp.sum(kk * kk, axis=-1, keepdims=True), 1e-24))
    k_a = p['rwkv_k_a'][l].astype(f32).reshape(A_HEADS, A_HEAD_DIM)
    k_h = heads(k) * (1.0 + (a_h - 1.0) * k_a)
    y, s_rwkv_new = rwkv7_recurrence(r_h, w_h, k_h, v_h, -kk, kk * a_h, s_rwkv)
    g_mean = jnp.mean(y, axis=-1, keepdims=True)
    g_var = jnp.mean(jnp.square(y - g_mean), axis=-1, keepdims=True)
    y = (y - g_mean) * lax.rsqrt(g_var + GN_EPS) * p['rwkv_ln_w'][l].astype(f32).reshape(A_HEADS, A_HEAD_DIM) \
        + p['rwkv_ln_b'][l].astype(f32).reshape(A_HEADS, A_HEAD_DIM)
    y = y + jnp.sum(r_h * k_h * p['rwkv_r_k'][l].astype(f32), axis=-1, keepdims=True) * v_h
    y_a = (y.reshape(bsz, t_len, DA).astype(h.dtype) * g) @ p['w_out_a'][l]

    q, fz, i_in, og = jnp.split(hg, [BK, 2 * BK, 2 * BK + DB], axis=-1)
    f = lb + (1.0 - lb) * jax.nn.sigmoid(fz.astype(f32))

    def bheads(t, d):
        return t.reshape(bsz, t_len, B_HEADS, d)

    o, s_hgrn_new = hgrn2_chunked(bheads(jax.nn.silu(q), B_KEY_DIM), bheads(1.0 - f, B_KEY_DIM),
                                  bheads(i_in, B_VAL_DIM), bheads(jnp.log(f), B_KEY_DIM), s_hgrn)
    o = o * lax.rsqrt(jnp.mean(o * o, axis=-1, keepdims=True) + RMS_EPS)
    o = o.reshape(bsz, t_len, DB).astype(h.dtype) * p['hgrn_norm_w'][l] * jax.nn.silu(og)
    y_b = o @ p['w_out_b'][l]

    mix = (gates[..., :D_MODEL] * y_a + gates[..., D_MODEL:] * y_b) @ p['w_out'][l]
    return mix, v_first, rw[:, -1], s_rwkv_new.astype(h.dtype), s_hgrn_new.astype(h.dtype)


def trunk(x, state_shift, state_rwkv, state_hgrn, p):
    lb_soft = jax.nn.softmax(p['hgrn_lb_logits'].astype(jnp.float32), axis=0)
    lb_all = jnp.cumsum(lb_soft, axis=0) - lb_soft[0]
    v_first = None
    shifts, rwkv_states, hgrn_states = [], [], []
    for l in range(DEPTH):
        h = rmsnorm(x, p['norm_mix'][l])
        mix, v_first, sh, sa, sb = token_mixer(h, l, lb_all[l], state_shift[l], state_rwkv[l], state_hgrn[l], v_first, p)
        x = x + mix
        h = rmsnorm(x, p['norm_ffn'][l])
        x = x + jnp.square(jax.nn.relu(h @ p['w_ffn_up'][l])) @ p['w_ffn_down'][l]
        shifts.append(sh)
        rwkv_states.append(sa)
        hgrn_states.append(sb)
    y = rmsnorm(x, p['norm_final'])
    return y, jnp.stack(shifts), jnp.stack(rwkv_states), jnp.stack(hgrn_states)


def setup_inputs(seed: int = 0) -> dict:
    key = jax.random.key(seed)
    ks = iter(jax.random.split(key, 40))
    f32 = jnp.float32

    def nrm(shape, scale):
        return jax.random.normal(next(ks), shape, f32) * scale

    def gain(shape):
        return 1.0 + nrm(shape, 0.02)

    return {
        'x_prompt': nrm((BATCH, SEQ, D_MODEL), 1.0),
        'x_sample': nrm((DEC_BATCH, DEC_SEQ, D_MODEL), 1.0),
        'state_shift': nrm((DEPTH, DEC_BATCH, RWKV_COLS), 1.0),
        'state_rwkv': nrm((DEPTH, DEC_BATCH, A_HEADS, A_HEAD_DIM, A_HEAD_DIM), 0.2),
        'state_hgrn': nrm((DEPTH, DEC_BATCH, B_HEADS, B_KEY_DIM, B_VAL_DIM), 0.5),
        'norm_mix': gain((DEPTH, D_MODEL)),
        'w_in': nrm((DEPTH, D_MODEL, P_TOTAL), D_MODEL ** -0.5),
        'rwkv_mu': jax.random.uniform(next(ks), (DEPTH, RWKV_COLS), f32, 0.2, 0.8),
        'rwkv_w0': jax.random.uniform(next(ks), (DEPTH, DA), f32, -6.0, 1.0),
        'rwkv_w2': nrm((DEPTH, W_LORA, DA), 0.1 * W_LORA ** -0.5),
        'rwkv_a0': nrm((DEPTH, DA), 0.1),
        'rwkv_a2': nrm((DEPTH, A_LORA, DA), 0.1 * A_LORA ** -0.5),
        'rwkv_g2': nrm((DEPTH, G_LORA, DA), G_LORA ** -0.5),
        'rwkv_v0': 1.0 + nrm((DEPTH - 1, DA), 0.1),
        'rwkv_vres_w1': nrm((DEPTH - 1, DA, V_LORA), DA ** -0.5),
        'rwkv_vres_w2': nrm((DEPTH - 1, V_LORA, DA), 0.1 * V_LORA ** -0.5),
        'rwkv_k_k': 0.85 + nrm((DEPTH, DA), 0.02),
        'rwkv_k_a': gain((DEPTH, DA)),
        'rwkv_r_k': nrm((DEPTH, A_HEADS, A_HEAD_DIM), 0.1),
        'rwkv_ln_w': gain((DEPTH, DA)),
        'rwkv_ln_b': nrm((DEPTH, DA), 0.02),
        'hgrn_lb_logits': nrm((DEPTH, BK), 0.1),
        'hgrn_norm_w': gain((DEPTH, DB)),
        'w_out_a': nrm((DEPTH, DA, D_MODEL), DA ** -0.5),
        'w_out_b': nrm((DEPTH, DB, D_MODEL), DB ** -0.5),
        'w_out': nrm((DEPTH, D_MODEL, D_MODEL), D_MODEL ** -0.5),
        'norm_ffn': gain((DEPTH, D_MODEL)),
        'w_ffn_up': nrm((DEPTH, D_MODEL, D_FF), D_MODEL ** -0.5),
        'w_ffn_down': nrm((DEPTH, D_FF, D_MODEL), D_FF ** -0.5),
        'norm_final': gain((D_MODEL,)),
    }


def reference(x_prompt, x_sample, state_shift, state_rwkv, state_hgrn, norm_mix, w_in, rwkv_mu, rwkv_w0, rwkv_w2,
              rwkv_a0, rwkv_a2, rwkv_g2, rwkv_v0, rwkv_vres_w1, rwkv_vres_w2, rwkv_k_k, rwkv_k_a, rwkv_r_k,
              rwkv_ln_w, rwkv_ln_b, hgrn_lb_logits, hgrn_norm_w, w_out_a, w_out_b, w_out, norm_ffn, w_ffn_up,
              w_ffn_down, norm_final):
    p = dict(norm_mix=norm_mix, w_in=w_in, rwkv_mu=rwkv_mu, rwkv_w0=rwkv_w0, rwkv_w2=rwkv_w2, rwkv_a0=rwkv_a0,
             rwkv_a2=rwkv_a2, rwkv_g2=rwkv_g2, rwkv_v0=rwkv_v0, rwkv_vres_w1=rwkv_vres_w1,
             rwkv_vres_w2=rwkv_vres_w2, rwkv_k_k=rwkv_k_k, rwkv_k_a=rwkv_k_a, rwkv_r_k=rwkv_r_k,
             rwkv_ln_w=rwkv_ln_w, rwkv_ln_b=rwkv_ln_b, hgrn_lb_logits=hgrn_lb_logits, hgrn_norm_w=hgrn_norm_w,
             w_out_a=w_out_a, w_out_b=w_out_b, w_out=w_out, norm_ffn=norm_ffn, w_ffn_up=w_ffn_up,
             w_ffn_down=w_ffn_down, norm_final=norm_final)
    dt = x_prompt.dtype
    bp = x_prompt.shape[0]
    zero_shift = jnp.zeros((DEPTH, bp, RWKV_COLS), dt)
    zero_rwkv = jnp.zeros((DEPTH, bp, A_HEADS, A_HEAD_DIM, A_HEAD_DIM), dt)
    zero_hgrn = jnp.zeros((DEPTH, bp, B_HEADS, B_KEY_DIM, B_VAL_DIM), dt)
    y_prompt, shift_p, rwkv_p, hgrn_p = trunk(x_prompt, zero_shift, zero_rwkv, zero_hgrn, p)
    y_sample, shift_s, rwkv_s, hgrn_s = trunk(x_sample, state_shift, state_rwkv, state_hgrn, p)
    return (y_prompt, y_sample, shift_p, rwkv_p, hgrn_p, shift_s, rwkv_s, hgrn_s)
```

```python
import functools
import math

import jax
import jax.numpy as jnp
from jax import lax
from jax.experimental import pallas as pl
from jax.experimental.pallas import tpu as pltpu

F32 = jnp.float32
BF16 = jnp.bfloat16

D_MODEL = 1024
DEPTH = 2
A_HEADS = 8
A_HEAD_DIM = 64
DA = A_HEADS * A_HEAD_DIM
W_LORA = 64
A_LORA = 64
V_LORA = 32
G_LORA = 160
RWKV_COLS = 3 * DA + W_LORA + A_LORA + G_LORA
B_HEADS = 4
B_KEY_DIM = 128
B_VAL_DIM = 128
BK = B_HEADS * B_KEY_DIM
DB = B_HEADS * B_VAL_DIM
HGRN_COLS = 2 * BK + 2 * DB
GATE_COLS = 2 * D_MODEL
D_FF = 4 * D_MODEL
HGRN_BLOCK = 16
RMS_EPS = 1e-6
GN_EPS = 64e-5

LANES = 128
GROUP_COLS = 2048
PROJ_COLS = 3 * GROUP_COLS
LORA_WA = 3 * DA
LORA_G = LORA_WA + W_LORA + A_LORA
LORA_G_PAD = 256
PAIR = 2 * A_HEAD_DIM
N_PAIRS = A_HEADS // 2
VMEM_LIMIT = 56 * 1024 * 1024


def _dot(a, b):
    return jnp.dot(a.astype(BF16), b.astype(BF16), preferred_element_type=F32)


def _dot_nt(a, b):
    return lax.dot_general(a.astype(BF16), b.astype(BF16), (((1,), (1,)), ((), ())), preferred_element_type=F32)


def _dot_tn(a, b):
    return lax.dot_general(a.astype(BF16), b.astype(BF16), (((0,), (0,)), ((), ())), preferred_element_type=F32)


def _dot_01(m01, x):
    hi = x.astype(BF16)
    r1 = x - hi.astype(F32)
    mid = r1.astype(BF16)
    lo = (r1 - mid.astype(F32)).astype(BF16)
    return (jnp.dot(m01, hi, preferred_element_type=F32) + jnp.dot(m01, mid, preferred_element_type=F32)
            + jnp.dot(m01, lo, preferred_element_type=F32))


def _rms(x, w):
    return x * lax.rsqrt(jnp.mean(x * x, axis=-1, keepdims=True) + RMS_EPS) * w


def _sigmoid(x):
    return 1.0 / (1.0 + jnp.exp(-x))


def _in_proj_kernel(x_ref, nw_ref, w_ref, o_ref, h_ref):
    j = pl.program_id(1)

    @pl.when(j == 0)
    def _():
        h_ref[...] = _rms(x_ref[...], nw_ref[...]).astype(BF16)

    acc = jnp.dot(h_ref[...], w_ref[...], preferred_element_type=F32)

    @pl.when(j < 2)
    def _():
        o_ref[...] = acc

    @pl.when(j == 2)
    def _():
        o_ref[...] = _sigmoid(acc)


def _in_proj(x, norm_w, w_p):
    n = x.shape[0]
    tm = min(n, 1024)
    return pl.pallas_call(
        _in_proj_kernel,
        grid=(n // tm, PROJ_COLS // GROUP_COLS),
        in_specs=[
            pl.BlockSpec((tm, D_MODEL), lambda i, j: (i, 0)),
            pl.BlockSpec((1, D_MODEL), lambda i, j: (0, 0)),
            pl.BlockSpec((D_MODEL, GROUP_COLS), lambda i, j: (0, j)),
        ],
        out_specs=pl.BlockSpec((tm, GROUP_COLS), lambda i, j: (i, j)),
        out_shape=jax.ShapeDtypeStruct((n, PROJ_COLS), F32),
        scratch_shapes=[pltpu.VMEM((tm, D_MODEL), BF16)],
        compiler_params=pltpu.CompilerParams(
            dimension_semantics=("arbitrary", "arbitrary"), vmem_limit_bytes=VMEM_LIMIT),
        name="in_proj",
    )(x, norm_w, w_p)


def _tri_inverse(low, eye, levels):
    t = eye + low
    p = low
    for _ in range(levels - 1):
        p = _dot(p, p)
        t = t + _dot(t, p)
    return t


def _rwkv_kernel(layer, c_len, *refs):
    if layer == 0:
        (rw_ref, shift_ref, s0_ref, mu_ref, pv_ref, w2_ref, a2_ref, g2_ref,
         yg_ref, shift_out_ref, s_ref, vfirst_out_ref, xs_ref) = refs
    else:
        (rw_ref, shift_ref, s0_ref, vfirst_ref, mu_ref, pv_ref, w2_ref, a2_ref, g2_ref, vw1_ref, vw2_ref,
         yg_ref, shift_out_ref, s_ref, xs_ref) = refs
    c = pl.program_id(1)

    @pl.when(c == 0)
    def _():
        xs_ref[7:8, :] = shift_ref[0]
        s_ref[0] = s0_ref[0]

    rw = rw_ref[...]
    xs_ref[8:8 + c_len, :] = rw
    prev = xs_ref[7:7 + c_len, :]
    last = rw[c_len - 1:c_len, :]
    xs_ref[7:8, :] = last
    shift_out_ref[0] = last
    mix = rw + (prev - rw) * mu_ref[...]

    r = mix[:, 0:DA]
    k = mix[:, DA:2 * DA]
    v = mix[:, 2 * DA:3 * DA]
    wa = mix[:, LORA_WA:LORA_G]
    gl = mix[:, LORA_G:LORA_G + LORA_G_PAD]

    w0 = pv_ref[0:1, :]
    a0 = pv_ref[1:2, :]
    k_k = pv_ref[2:3, :]
    k_a = pv_ref[3:4, :]
    r_k = pv_ref[4:5, :]
    ln_w = pv_ref[5:6, :]
    ln_b = pv_ref[6:7, :]

    w_raw = w0 + _dot(jnp.tanh(wa), w2_ref[...])
    lw = (-math.exp(-0.5)) * _sigmoid(w_raw)
    a_gate = _sigmoid(a0 + _dot(wa, a2_ref[...]))
    g = _dot(_sigmoid(gl), g2_ref[...])
    if layer == 0:
        vfirst_out_ref[...] = v
    else:
        v0 = pv_ref[7:8, :]
        vg = _sigmoid(v0 + _dot(_dot(v, vw1_ref[...]), vw2_ref[...]))
        v = v + (vfirst_ref[...] - v) * vg

    lane = lax.broadcasted_iota(jnp.int32, (PAIR, PAIR), 1)
    sub = lax.broadcasted_iota(jnp.int32, (PAIR, PAIR), 0)
    same_head = (lane < A_HEAD_DIM) == (sub < A_HEAD_DIM)
    head_ones = jnp.where(same_head, 1.0, 0.0).astype(BF16)

    def seg_sum(x):
        return jnp.concatenate(
            [jnp.dot(x[:, p * PAIR:(p + 1) * PAIR].astype(BF16), head_ones, preferred_element_type=F32)
             for p in range(N_PAIRS)], axis=1)

    kk = k * k_k
    kk = kk * lax.rsqrt(jnp.maximum(seg_sum(kk * kk), 1e-24))
    k_h = k * (1.0 + (a_gate - 1.0) * k_a)
    a_vec = -kk
    b_vec = kk * a_gate

    row = lax.broadcasted_iota(jnp.int32, (c_len, c_len), 0)
    col = lax.broadcasted_iota(jnp.int32, (c_len, c_len), 1)
    lower = row >= col
    strict = row > col
    eye = jnp.where(row == col, 1.0, 0.0).astype(F32)
    g_cum = _dot_01(jnp.where(lower, 1.0, 0.0).astype(BF16), lw)
    g_end = g_cum[c_len - 1:c_len, :]
    e_pos = jnp.exp(g_cum)
    e_neg = jnp.exp(-g_cum)
    rq = r * e_pos
    aq = a_vec * jnp.exp(g_cum - lw)
    kn = k_h * e_neg
    bn = b_vec * e_neg
    e_end = jnp.exp(g_end - g_cum)
    k_end = k_h * e_end
    b_end = b_vec * e_end
    decay_end = jnp.exp(g_end)

    lane_c = lax.broadcasted_iota(jnp.int32, (c_len, PAIR), 1)
    first_head = lane_c < A_HEAD_DIM
    levels = int(math.log2(c_len))

    y_pairs = []
    for p in range(N_PAIRS):
        sl = slice(p * PAIR, (p + 1) * PAIR)
        s_pair = s_ref[0, p]
        aq_p, rq_p, v_p = aq[:, sl], rq[:, sl], v[:, sl]
        kn_p, bn_p = kn[:, sl].astype(BF16), bn[:, sl].astype(BF16)
        a_ak, a_rk, a_rb, t_inv = [], [], [], []
        for hh in range(2):
            mine = first_head if hh == 0 else jnp.logical_not(first_head)
            lhs = jnp.concatenate([jnp.where(mine, aq_p, 0.0), jnp.where(mine, rq_p, 0.0)], axis=0).astype(BF16)
            sb = _dot_nt(lhs, bn_p)
            sk = _dot_nt(lhs, kn_p)
            t_inv.append(_tri_inverse(jnp.where(strict, sb[:c_len], 0.0), eye, levels))
            a_rb.append(jnp.where(lower, sb[c_len:], 0.0))
            a_ak.append(jnp.where(strict, sk[:c_len], 0.0))
            a_rk.append(jnp.where(lower, sk[c_len:], 0.0))

        def merge(x0, x1):
            return jnp.where(first_head, x0, x1)

        x_in = _dot_nt(aq_p, s_pair) + merge(_dot(a_ak[0], v_p), _dot(a_ak[1], v_p))
        u = merge(_dot(t_inv[0], x_in), _dot(t_inv[1], x_in))
        y_pairs.append(_dot_nt(rq_p, s_pair)
                       + merge(_dot(a_rk[0], v_p) + _dot(a_rb[0], u), _dot(a_rk[1], v_p) + _dot(a_rb[1], u)))
        upd = _dot_tn(v_p, k_end[:, sl]) + _dot_tn(u, b_end[:, sl])
        s_ref[0, p] = s_pair * decay_end[:, sl] + jnp.where(same_head, upd, 0.0)

    y = jnp.concatenate(y_pairs, axis=1)
    inv_n = 1.0 / A_HEAD_DIM
    mean = seg_sum(y) * inv_n
    dev = y - mean
    var = seg_sum(dev * dev) * inv_n
    y = dev * lax.rsqrt(var + GN_EPS) * ln_w + ln_b
    y = y + seg_sum(r * k_h * r_k) * v
    yg_ref[...] = (y * g).astype(BF16)


def _rwkv(layer, proj, shift0, s0, vfirst, mu_p, pv, w2_p, a2_p, g2_p, vw1_p, vw2_p, bsz, t_len, c_len):
    n = bsz * t_len
    n_chunks = t_len // c_len
    const2 = lambda b, c: (0, 0)
    in_specs = [
        pl.BlockSpec((c_len, GROUP_COLS), lambda b, c: (b * n_chunks + c, 0)),
        pl.BlockSpec((1, 1, GROUP_COLS), lambda b, c: (b, 0, 0)),
        pl.BlockSpec((1, N_PAIRS, PAIR, PAIR), lambda b, c: (b, 0, 0, 0)),
    ]
    args = [proj, shift0, s0]
    if layer > 0:
        in_specs.append(pl.BlockSpec((c_len, DA), lambda b, c: (b * n_chunks + c, 0)))
        args.append(vfirst)
    in_specs += [
        pl.BlockSpec((1, GROUP_COLS), const2),
        pl.BlockSpec((8, DA), const2),
        pl.BlockSpec((LANES, DA), const2),
        pl.BlockSpec((LANES, DA), const2),
        pl.BlockSpec((LORA_G_PAD, DA), const2),
    ]
    args += [mu_p, pv, w2_p, a2_p, g2_p]
    if layer > 0:
        in_specs += [pl.BlockSpec((DA, LANES), const2), pl.BlockSpec((LANES, DA), const2)]
        args += [vw1_p, vw2_p]
    out_specs = [
        pl.BlockSpec((c_len, DA), lambda b, c: (b * n_chunks + c, 0)),
        pl.BlockSpec((1, 1, GROUP_COLS), lambda b, c: (b, 0, 0)),
        pl.BlockSpec((1, N_PAIRS, PAIR, PAIR), lambda b, c: (b, 0, 0, 0)),
    ]
    out_shape = [
        jax.ShapeDtypeStruct((n, DA), BF16),
        jax.ShapeDtypeStruct((bsz, 1, GROUP_COLS), F32),
        jax.ShapeDtypeStruct((bsz, N_PAIRS, PAIR, PAIR), F32),
    ]
    if layer == 0:
        out_specs.append(pl.BlockSpec((c_len, DA), lambda b, c: (b * n_chunks + c, 0)))
        out_shape.append(jax.ShapeDtypeStruct((n, DA), F32))
    return pl.pallas_call(
        functools.partial(_rwkv_kernel, layer, c_len),
        grid=(bsz, n_chunks),
        in_specs=in_specs,
        out_specs=out_specs,
        out_shape=out_shape,
        scratch_shapes=[pltpu.VMEM((c_len + 8, GROUP_COLS), F32)],
        compiler_params=pltpu.CompilerParams(
            dimension_semantics=("arbitrary", "arbitrary"), vmem_limit_bytes=VMEM_LIMIT),
        name=f"rwkv{layer}",
    )(*args)


def _hgrn_kernel(layer, tb, hg_ref, lg_ref, nw_ref, s0_ref, on_ref, s_out_ref, st_ref):
    t = pl.program_id(1)

    @pl.when(t == 0)
    def _():
        for h in range(B_HEADS):
            st_ref[h] = s0_ref[0, h].T

    hg = hg_ref[...]
    q = hg[:, 0:BK]
    fz = hg[:, BK:2 * BK]
    iv = hg[:, 2 * BK:2 * BK + DB]
    og = hg[:, 2 * BK + DB:]

    lg = lg_ref[...]
    rows = [lg[i:i + 1, :] for i in range(DEPTH)]
    mx = functools.reduce(jnp.maximum, rows)
    ex = [jnp.exp(x - mx) for x in rows]
    den = functools.reduce(lambda a, b: a + b, ex)
    sm = [e / den for e in ex]
    lb = functools.reduce(lambda a, b: a + b, sm[:layer + 1]) - sm[0]

    f = lb + (1.0 - lb) * _sigmoid(fz)
    g = jnp.log(f)
    kx = 1.0 - f
    qs = q * _sigmoid(q)

    row = lax.broadcasted_iota(jnp.int32, (tb, tb), 0)
    col = lax.broadcasted_iota(jnp.int32, (tb, tb), 1)
    blk_shift = HGRN_BLOCK.bit_length() - 1
    same_block = lax.shift_right_logical(row, blk_shift) == lax.shift_right_logical(col, blk_shift)
    causal = jnp.logical_and(same_block, row >= col)
    g_cum = _dot_01(jnp.where(causal, 1.0, 0.0).astype(BF16), g)
    g_end = _dot_01(jnp.where(same_block, 1.0, 0.0).astype(BF16), g)
    qg = qs * jnp.exp(g_cum)
    kg = kx * jnp.exp(-g_cum)
    k_end = kx * jnp.exp(g_end - g_cum)
    decay = jnp.exp(g_end)

    outs = []
    for h in range(B_HEADS):
        sl = slice(h * B_KEY_DIM, (h + 1) * B_KEY_DIM)
        qg_h, v_h, k_end_h = qg[:, sl], iv[:, sl], k_end[:, sl]
        att = jnp.where(causal, _dot_nt(qg_h, kg[:, sl]), 0.0)
        o_h = _dot(att, v_h)
        s = st_ref[h]
        inter = []
        for j in range(tb // HGRN_BLOCK):
            rs = slice(j * HGRN_BLOCK, (j + 1) * HGRN_BLOCK)
            inter.append(_dot_nt(qg_h[rs], s))
            s = s * decay[j * HGRN_BLOCK:j * HGRN_BLOCK + 1, sl] + _dot_tn(v_h[rs], k_end_h[rs])
        st_ref[h] = s
        o_h = o_h + jnp.concatenate(inter, axis=0)
        outs.append(o_h * lax.rsqrt(jnp.mean(o_h * o_h, axis=-1, keepdims=True) + RMS_EPS))
    o = jnp.concatenate(outs, axis=1)
    on_ref[...] = (o * nw_ref[...] * (og * _sigmoid(og))).astype(BF16)

    @pl.when(t == pl.num_programs(1) - 1)
    def _():
        for h in range(B_HEADS):
            s_out_ref[0, h] = st_ref[h].T


def _hgrn(layer, proj, lb_logits, norm_w, s0, bsz, t_len, tb):
    n = bsz * t_len
    n_blocks = t_len // tb
    return pl.pallas_call(
        functools.partial(_hgrn_kernel, layer, tb),
        grid=(bsz, n_blocks),
        in_specs=[
            pl.BlockSpec((tb, GROUP_COLS), lambda b, t: (b * n_blocks + t, 1)),
            pl.BlockSpec((DEPTH, BK), lambda b, t: (0, 0)),
            pl.BlockSpec((1, DB), lambda b, t: (0, 0)),
            pl.BlockSpec((1, B_HEADS, B_KEY_DIM, B_VAL_DIM), lambda b, t: (b, 0, 0, 0)),
        ],
        out_specs=[
            pl.BlockSpec((tb, DB), lambda b, t: (b * n_blocks + t, 0)),
            pl.BlockSpec((1, B_HEADS, B_KEY_DIM, B_VAL_DIM), lambda b, t: (b, 0, 0, 0)),
        ],
        out_shape=[
            jax.ShapeDtypeStruct((n, DB), BF16),
            jax.ShapeDtypeStruct((bsz, B_HEADS, B_KEY_DIM, B_VAL_DIM), F32),
        ],
        scratch_shapes=[pltpu.VMEM((B_HEADS, B_VAL_DIM, B_KEY_DIM), F32)],
        compiler_params=pltpu.CompilerParams(
            dimension_semantics=("arbitrary", "arbitrary"), vmem_limit_bytes=VMEM_LIMIT),
        name=f"hgrn{layer}",
    )(proj, lb_logits, norm_w, s0)


FF_CHUNK = 1024


def _mix_ffn_kernel(final, *refs):
    if final:
        (x_ref, yg_ref, on_ref, gt_ref, wa_ref, wb_ref, wo_ref, nf_ref, up_ref, dn_ref, nfin_ref, o_ref) = refs
    else:
        (x_ref, yg_ref, on_ref, gt_ref, wa_ref, wb_ref, wo_ref, nf_ref, up_ref, dn_ref, o_ref) = refs
    ya = jnp.dot(yg_ref[...], wa_ref[...], preferred_element_type=F32)
    yb = jnp.dot(on_ref[...], wb_ref[...], preferred_element_type=F32)
    gt = gt_ref[...]
    merged = gt[:, 0:D_MODEL] * ya + gt[:, D_MODEL:] * yb
    x = x_ref[...] + jnp.dot(merged.astype(BF16), wo_ref[...], preferred_element_type=F32)
    h = _rms(x, nf_ref[...]).astype(BF16)
    acc = x
    for cf in range(D_FF // FF_CHUNK):
        u = jnp.dot(h, up_ref[:, cf * FF_CHUNK:(cf + 1) * FF_CHUNK], preferred_element_type=F32)
        u = jnp.maximum(u, 0.0)
        acc = acc + jnp.dot((u * u).astype(BF16), dn_ref[cf * FF_CHUNK:(cf + 1) * FF_CHUNK, :],
                            preferred_element_type=F32)
    if final:
        acc = _rms(acc, nfin_ref[...])
    o_ref[...] = acc


def _mix_ffn(final, x, yg, on, proj, wa, wb, wo, nf, up, dn, nfin):
    n = x.shape[0]
    tm = min(n, 512)
    const = lambda i: (0, 0)
    resident = functools.partial(pl.BlockSpec, index_map=const, pipeline_mode=pl.Buffered(1))
    in_specs = [
        pl.BlockSpec((tm, D_MODEL), lambda i: (i, 0)),
        pl.BlockSpec((tm, DA), lambda i: (i, 0)),
        pl.BlockSpec((tm, DB), lambda i: (i, 0)),
        pl.BlockSpec((tm, GROUP_COLS), lambda i: (i, 2)),
        resident((DA, D_MODEL)),
        resident((DB, D_MODEL)),
        resident((D_MODEL, D_MODEL)),
        resident((1, D_MODEL)),
        resident((D_MODEL, D_FF)),
        resident((D_FF, D_MODEL)),
    ]
    args = [x, yg, on, proj, wa, wb, wo, nf, up, dn]
    if final:
        in_specs.append(resident((1, D_MODEL)))
        args.append(nfin)
    return pl.pallas_call(
        functools.partial(_mix_ffn_kernel, final),
        grid=(n // tm,),
        in_specs=in_specs,
        out_specs=pl.BlockSpec((tm, D_MODEL), lambda i: (i, 0)),
        out_shape=jax.ShapeDtypeStruct((n, D_MODEL), F32),
        compiler_params=pltpu.CompilerParams(dimension_semantics=("arbitrary",), vmem_limit_bytes=VMEM_LIMIT),
        name="mix_ffn_final" if final else "mix_ffn",
    )(*args)


def _pad_rows(w, rows, at=0):
    out = jnp.zeros((rows, w.shape[1]), w.dtype)
    return out.at[at:at + w.shape[0]].set(w)


def _prep_layer(l, p):
    w_in = p["w_in"][l]
    pad = jnp.zeros((D_MODEL, GROUP_COLS - RWKV_COLS), w_in.dtype)
    w_p = jnp.concatenate([w_in[:, :RWKV_COLS], pad, w_in[:, RWKV_COLS:]], axis=1).astype(BF16)
    mu_p = jnp.zeros((1, GROUP_COLS), F32).at[0, :RWKV_COLS].set(p["rwkv_mu"][l])
    v0 = p["rwkv_v0"][l - 1] if l > 0 else jnp.zeros((DA,), F32)
    pv = jnp.stack([p["rwkv_w0"][l], p["rwkv_a0"][l], p["rwkv_k_k"][l], p["rwkv_k_a"][l],
                    p["rwkv_r_k"][l].reshape(DA), p["rwkv_ln_w"][l], p["rwkv_ln_b"][l], v0])
    out = dict(
        w_p=w_p, mu_p=mu_p, pv=pv,
        w2_p=_pad_rows(p["rwkv_w2"][l], LANES, 0).astype(BF16),
        a2_p=_pad_rows(p["rwkv_a2"][l], LANES, W_LORA).astype(BF16),
        g2_p=_pad_rows(p["rwkv_g2"][l], LORA_G_PAD, 0).astype(BF16),
        vw1_p=None, vw2_p=None,
        norm_mix=p["norm_mix"][l].reshape(1, D_MODEL),
        hgrn_norm_w=p["hgrn_norm_w"][l].reshape(1, DB),
        wa=p["w_out_a"][l].astype(BF16), wb=p["w_out_b"][l].astype(BF16), wo=p["w_out"][l].astype(BF16),
        nf=p["norm_ffn"][l].reshape(1, D_MODEL),
        up=p["w_ffn_up"][l].astype(BF16), dn=p["w_ffn_down"][l].astype(BF16),
    )
    if l > 0:
        w1 = p["rwkv_vres_w1"][l - 1]
        out["vw1_p"] = jnp.zeros((DA, LANES), F32).at[:, :V_LORA].set(w1).astype(BF16)
        out["vw2_p"] = _pad_rows(p["rwkv_vres_w2"][l - 1], LANES, 0).astype(BF16)
    return out


def _pair_states(s):
    bsz = s.shape[0]
    out = jnp.zeros((bsz, N_PAIRS, PAIR, PAIR), s.dtype)
    out = out.at[:, :, :A_HEAD_DIM, :A_HEAD_DIM].set(s[:, 0::2])
    return out.at[:, :, A_HEAD_DIM:, A_HEAD_DIM:].set(s[:, 1::2])


def _unpair_states(s):
    bsz = s.shape[0]
    both = jnp.stack([s[:, :, :A_HEAD_DIM, :A_HEAD_DIM], s[:, :, A_HEAD_DIM:, A_HEAD_DIM:]], axis=2)
    return both.reshape(bsz, A_HEADS, A_HEAD_DIM, A_HEAD_DIM)


def _trunk(x, state_shift, state_rwkv, state_hgrn, layers, p, c_len, tb):
    bsz, t_len, _ = x.shape
    xf = x.reshape(bsz * t_len, D_MODEL)
    nfin = p["norm_final"].reshape(1, D_MODEL)
    vfirst = None
    shifts, rwkv_states, hgrn_states = [], [], []
    for l in range(DEPTH):
        lp = layers[l]
        proj = _in_proj(xf, lp["norm_mix"], lp["w_p"])
        shift0 = jnp.zeros((bsz, 1, GROUP_COLS), F32).at[:, 0, :RWKV_COLS].set(state_shift[l])
        res = _rwkv(l, proj, shift0, _pair_states(state_rwkv[l]), vfirst, lp["mu_p"], lp["pv"], lp["w2_p"],
                    lp["a2_p"], lp["g2_p"], lp["vw1_p"], lp["vw2_p"], bsz, t_len, c_len)
        if l == 0:
            yg, shift_out, s_rwkv, vfirst = res
        else:
            yg, shift_out, s_rwkv = res
        on, s_hgrn = _hgrn(l, proj, p["hgrn_lb_logits"], lp["hgrn_norm_w"], state_hgrn[l], bsz, t_len, tb)
        xf = _mix_ffn(l == DEPTH - 1, xf, yg, on, proj, lp["wa"], lp["wb"], lp["wo"], lp["nf"], lp["up"], lp["dn"],
                      nfin)
        shifts.append(shift_out[:, 0, :RWKV_COLS])
        rwkv_states.append(_unpair_states(s_rwkv))
        hgrn_states.append(s_hgrn)
    return xf.reshape(bsz, t_len, D_MODEL), jnp.stack(shifts), jnp.stack(rwkv_states), jnp.stack(hgrn_states)


def _chunk_len(t_len, target):
    return min(t_len, target)


def kernel(x_prompt, x_sample, state_shift, state_rwkv, state_hgrn, norm_mix, w_in, rwkv_mu, rwkv_w0, rwkv_w2, rwkv_a0, rwkv_a2, rwkv_g2, rwkv_v0, rwkv_vres_w1, rwkv_vres_w2, rwkv_k_k, rwkv_k_a, rwkv_r_k, rwkv_ln_w, rwkv_ln_b, hgrn_lb_logits, hgrn_norm_w, w_out_a, w_out_b, w_out, norm_ffn, w_ffn_up, w_ffn_down, norm_final):
    p = dict(norm_mix=norm_mix, w_in=w_in, rwkv_mu=rwkv_mu, rwkv_w0=rwkv_w0, rwkv_w2=rwkv_w2, rwkv_a0=rwkv_a0,
             rwkv_a2=rwkv_a2, rwkv_g2=rwkv_g2, rwkv_v0=rwkv_v0, rwkv_vres_w1=rwkv_vres_w1,
             rwkv_vres_w2=rwkv_vres_w2, rwkv_k_k=rwkv_k_k, rwkv_k_a=rwkv_k_a, rwkv_r_k=rwkv_r_k,
             rwkv_ln_w=rwkv_ln_w, rwkv_ln_b=rwkv_ln_b, hgrn_lb_logits=hgrn_lb_logits, hgrn_norm_w=hgrn_norm_w,
             w_out_a=w_out_a, w_out_b=w_out_b, w_out=w_out, norm_ffn=norm_ffn, w_ffn_up=w_ffn_up,
             w_ffn_down=w_ffn_down, norm_final=norm_final)
    layers = [_prep_layer(l, p) for l in range(DEPTH)]
    bp, tp, _ = x_prompt.shape
    bs, ts, _ = x_sample.shape
    dt = x_prompt.dtype
    zero_shift = jnp.zeros((DEPTH, bp, RWKV_COLS), dt)
    zero_rwkv = jnp.zeros((DEPTH, bp, A_HEADS, A_HEAD_DIM, A_HEAD_DIM), dt)
    zero_hgrn = jnp.zeros((DEPTH, bp, B_HEADS, B_KEY_DIM, B_VAL_DIM), dt)
    y_prompt, shift_p, rwkv_p, hgrn_p = _trunk(x_prompt, zero_shift, zero_rwkv, zero_hgrn, layers, p,
                                               _chunk_len(tp, 64), _chunk_len(tp, 256))
    y_sample, shift_s, rwkv_s, hgrn_s = _trunk(x_sample, state_shift, state_rwkv, state_hgrn, layers, p,
                                               _chunk_len(ts, 64), _chunk_len(ts, 256))
    return (y_prompt, y_sample, shift_p, rwkv_p, hgrn_p, shift_s, rwkv_s, hgrn_s)
```

```python
import functools
import math

import jax
import jax.numpy as jnp
from jax import lax
from jax.experimental import pallas as pl
from jax.experimental.pallas import tpu as pltpu

F32 = jnp.float32
BF16 = jnp.bfloat16

D_MODEL = 1024
DEPTH = 2
A_HEADS = 8
A_HEAD_DIM = 64
DA = A_HEADS * A_HEAD_DIM
W_LORA = 64
A_LORA = 64
V_LORA = 32
G_LORA = 160
RWKV_COLS = 3 * DA + W_LORA + A_LORA + G_LORA
B_HEADS = 4
B_KEY_DIM = 128
B_VAL_DIM = 128
BK = B_HEADS * B_KEY_DIM
DB = B_HEADS * B_VAL_DIM
HGRN_COLS = 2 * BK + 2 * DB
GATE_COLS = 2 * D_MODEL
D_FF = 4 * D_MODEL
HGRN_BLOCK = 16
RMS_EPS = 1e-6
GN_EPS = 64e-5

LANES = 128
GROUP_COLS = 2048
PROJ_COLS = 3 * GROUP_COLS
LORA_WA = 3 * DA
LORA_G = LORA_WA + W_LORA + A_LORA
LORA_G_PAD = 256
PAIR = 2 * A_HEAD_DIM
N_PAIRS = A_HEADS // 2
VMEM_LIMIT = 56 * 1024 * 1024


def _dot(a, b):
    return jnp.dot(a.astype(BF16), b.astype(BF16), preferred_element_type=F32)


def _dot_nt(a, b):
    return lax.dot_general(a.astype(BF16), b.astype(BF16), (((1,), (1,)), ((), ())), preferred_element_type=F32)


def _dot_tn(a, b):
    return lax.dot_general(a.astype(BF16), b.astype(BF16), (((0,), (0,)), ((), ())), preferred_element_type=F32)


def _dot_01(m01, x):
    hi = x.astype(BF16)
    r1 = x - hi.astype(F32)
    mid = r1.astype(BF16)
    lo = (r1 - mid.astype(F32)).astype(BF16)
    return (jnp.dot(m01, hi, preferred_element_type=F32) + jnp.dot(m01, mid, preferred_element_type=F32)
            + jnp.dot(m01, lo, preferred_element_type=F32))


def _rms(x, w):
    return x * lax.rsqrt(jnp.mean(x * x, axis=-1, keepdims=True) + RMS_EPS) * w


def _sigmoid(x):
    return 1.0 / (1.0 + jnp.exp(-x))


def _log2(n):
    assert n & (n - 1) == 0, n
    return n.bit_length() - 1


def _in_proj_kernel(x_ref, nw_ref, w_ref, o_ref, h_ref):
    j = pl.program_id(1)

    @pl.when(j == 0)
    def _():
        h_ref[...] = _rms(x_ref[...], nw_ref[...]).astype(BF16)

    acc = jnp.dot(h_ref[...], w_ref[...], preferred_element_type=F32)

    @pl.when(j < 2)
    def _():
        o_ref[...] = acc

    @pl.when(j == 2)
    def _():
        o_ref[...] = _sigmoid(acc)


def _in_proj(x, norm_w, w_p):
    n = x.shape[0]
    tm = min(n, 1024)
    return pl.pallas_call(
        _in_proj_kernel,
        grid=(n // tm, PROJ_COLS // GROUP_COLS),
        in_specs=[
            pl.BlockSpec((tm, D_MODEL), lambda i, j: (i, 0)),
            pl.BlockSpec((1, D_MODEL), lambda i, j: (0, 0)),
            pl.BlockSpec((D_MODEL, GROUP_COLS), lambda i, j: (0, j)),
        ],
        out_specs=pl.BlockSpec((tm, GROUP_COLS), lambda i, j: (i, j)),
        out_shape=jax.ShapeDtypeStruct((n, PROJ_COLS), F32),
        scratch_shapes=[pltpu.VMEM((tm, D_MODEL), BF16)],
        compiler_params=pltpu.CompilerParams(
            dimension_semantics=("arbitrary", "arbitrary"), vmem_limit_bytes=VMEM_LIMIT),
        name="in_proj",
    )(x, norm_w, w_p)


def _rwkv_kernel(layer, bt, c_len, *refs):
    if layer == 0:
        (rw_ref, shift_ref, s0_ref, mu_ref, pv_ref, w2_ref, a2_ref, g2_ref,
         yg_ref, shift_out_ref, s_ref, vfirst_out_ref, xs_ref) = refs
    else:
        (rw_ref, shift_ref, s0_ref, vfirst_ref, mu_ref, pv_ref, w2_ref, a2_ref, g2_ref, vw1_ref, vw2_ref,
         yg_ref, shift_out_ref, s_ref, xs_ref) = refs
    c = pl.program_id(1)
    rows = bt * c_len
    stk = 2 * c_len

    @pl.when(c == 0)
    def _():
        for b in range(bt):
            xs_ref[b, 7:8, :] = shift_ref[b]
        s_ref[...] = s0_ref[...]

    rws, prevs = [], []
    for b in range(bt):
        rw_b = rw_ref[b]
        xs_ref[b, 8:8 + c_len, :] = rw_b
        prevs.append(xs_ref[b, 7:7 + c_len, :])
        last = rw_b[c_len - 1:c_len, :]
        xs_ref[b, 7:8, :] = last
        shift_out_ref[b] = last
        rws.append(rw_b)
    rw = jnp.concatenate(rws, axis=0)
    prev = jnp.concatenate(prevs, axis=0)
    mix = rw + (prev - rw) * mu_ref[...]

    r = mix[:, 0:DA]
    k = mix[:, DA:2 * DA]
    v = mix[:, 2 * DA:3 * DA]
    wa = mix[:, LORA_WA:LORA_G]
    gl = mix[:, LORA_G:LORA_G + LORA_G_PAD]

    w0 = pv_ref[0:1, :]
    a0 = pv_ref[1:2, :]
    k_k = pv_ref[2:3, :]
    k_a = pv_ref[3:4, :]
    r_k = pv_ref[4:5, :]
    ln_w = pv_ref[5:6, :]
    ln_b = pv_ref[6:7, :]

    w_raw = w0 + _dot(jnp.tanh(wa), w2_ref[...])
    a_gate = _sigmoid(a0 + _dot(wa, a2_ref[...]))
    g = _dot(_sigmoid(gl), g2_ref[...])
    lw = (-math.exp(-0.5)) * _sigmoid(w_raw)
    if layer == 0:
        for b in range(bt):
            vfirst_out_ref[b] = v[b * c_len:(b + 1) * c_len]
    else:
        v0 = pv_ref[7:8, :]
        vg = _sigmoid(v0 + _dot(_dot(v, vw1_ref[...]), vw2_ref[...]))
        vfirst = jnp.concatenate([vfirst_ref[b] for b in range(bt)], axis=0)
        v = v + (vfirst - v) * vg

    lane = lax.broadcasted_iota(jnp.int32, (PAIR, PAIR), 1)
    sub = lax.broadcasted_iota(jnp.int32, (PAIR, PAIR), 0)
    head_ones = jnp.where((lane < A_HEAD_DIM) == (sub < A_HEAD_DIM), 1.0, 0.0).astype(BF16)

    def seg_sum(x):
        return jnp.concatenate(
            [jnp.dot(x[:, p * PAIR:(p + 1) * PAIR].astype(BF16), head_ones, preferred_element_type=F32)
             for p in range(N_PAIRS)], axis=1)

    kk = k * k_k
    kk = kk * lax.rsqrt(jnp.maximum(seg_sum(kk * kk), 1e-24))
    k_h = k * (1.0 + (a_gate - 1.0) * k_a)
    a_vec = -kk
    b_vec = kk * a_gate

    row = lax.broadcasted_iota(jnp.int32, (rows, rows), 0)
    col = lax.broadcasted_iota(jnp.int32, (rows, rows), 1)
    seq_shift = _log2(c_len)
    causal = jnp.logical_and(lax.shift_right_logical(row, seq_shift) == lax.shift_right_logical(col, seq_shift),
                             row >= col)
    g_cum = _dot_01(jnp.where(causal, 1.0, 0.0).astype(BF16), lw)
    g_end = jnp.concatenate(
        [jnp.broadcast_to(g_cum[(b + 1) * c_len - 1:(b + 1) * c_len, :], (c_len, DA)) for b in range(bt)], axis=0)
    e_pos = jnp.exp(g_cum)
    e_neg = jnp.exp(-g_cum)
    e_end = jnp.exp(g_end - g_cum)
    rq = r * e_pos
    aq = a_vec * jnp.exp(g_cum - lw)
    kn = k_h * e_neg
    bn = b_vec * e_neg
    k_end = k_h * e_end
    b_end = b_vec * e_end
    decay_end = jnp.exp(g_end)

    srow = lax.broadcasted_iota(jnp.int32, (stk, PAIR), 0)
    slane = lax.broadcasted_iota(jnp.int32, (stk, PAIR), 1)
    own_lanes = (srow < c_len) == (slane < A_HEAD_DIM)

    def stack(x):
        return jnp.where(own_lanes, jnp.concatenate([x, x], axis=0), 0.0).astype(BF16)

    mrow = lax.broadcasted_iota(jnp.int32, (stk, stk), 0) & (c_len - 1)
    mcol = lax.broadcasted_iota(jnp.int32, (stk, stk), 1) & (c_len - 1)
    strict = mrow > mcol
    lower = mrow >= mcol
    eye = jnp.where(lax.broadcasted_iota(jnp.int32, (stk, stk), 0) == lax.broadcasted_iota(jnp.int32, (stk, stk), 1),
                    1.0, 0.0).astype(F32)

    units = [(b, p) for b in range(bt) for p in range(N_PAIRS)]

    def cut(x, b, p):
        return x[b * c_len:(b + 1) * c_len, p * PAIR:(p + 1) * PAIR]

    lhs, v_st, low, a_ak, a_rk, a_rb = {}, {}, {}, {}, {}, {}
    for u in units:
        lhs[u] = jnp.concatenate([stack(cut(aq, *u)), stack(cut(rq, *u))], axis=0)
        v_st[u] = stack(cut(v, *u))
        sc_b = _dot_nt(lhs[u], stack(cut(bn, *u)))
        sc_k = _dot_nt(lhs[u], stack(cut(kn, *u)))
        low[u] = jnp.where(strict, sc_b[:stk], 0.0)
        a_rb[u] = jnp.where(lower, sc_b[stk:], 0.0)
        a_ak[u] = jnp.where(strict, sc_k[:stk], 0.0)
        a_rk[u] = jnp.where(lower, sc_k[stk:], 0.0)

    t_inv = {u: eye + low[u] for u in units}
    power = dict(low)
    for _ in range(_log2(c_len) - 1):
        power = {u: _dot(power[u], power[u]) for u in units}
        t_inv = {u: t_inv[u] + _dot(t_inv[u], power[u]) for u in units}

    from_state = {u: _dot_nt(lhs[u], s_ref[u[0], u[1]]) for u in units}
    x_in = {u: from_state[u][:stk] + _dot(a_ak[u], v_st[u]) for u in units}
    corr = {u: _dot(t_inv[u], x_in[u]) for u in units}

    y_parts = {}
    for u in units:
        vu = jnp.concatenate([v_st[u], corr[u].astype(BF16)], axis=0)
        if stk % LANES == 0:
            y_st = from_state[u][stk:] + _dot(jnp.concatenate([a_rk[u], a_rb[u]], axis=1), vu)
        else:
            y_st = from_state[u][stk:] + _dot(a_rk[u], v_st[u]) + _dot(a_rb[u], corr[u])
        y_parts[u] = y_st[:c_len] + y_st[c_len:]
        ends = jnp.concatenate([stack(cut(k_end, *u)), stack(cut(b_end, *u))], axis=0)
        b, p = u
        s_ref[b, p] = s_ref[b, p] * cut(decay_end, b, p)[0:1, :] + _dot_tn(vu, ends)

    y = jnp.concatenate(
        [jnp.concatenate([y_parts[(b, p)] for p in range(N_PAIRS)], axis=1) for b in range(bt)], axis=0)
    inv_n = 1.0 / A_HEAD_DIM
    mean = seg_sum(y) * inv_n
    bonus = seg_sum(r * k_h * r_k)
    dev = y - mean
    var = seg_sum(dev * dev) * inv_n
    y = dev * lax.rsqrt(var + GN_EPS) * ln_w + ln_b
    y = ((y + bonus * v) * g).astype(BF16)
    for b in range(bt):
        yg_ref[b] = y[b * c_len:(b + 1) * c_len]


def _rwkv(layer, proj, shift0, s0, vfirst, mu_p, pv, w2_p, a2_p, g2_p, vw1_p, vw2_p, bsz, t_len, c_len, bt):
    n_chunks = t_len // c_len
    const2 = lambda b, c: (0, 0)
    frames = lambda b, c: (b, c, 0)
    per_seq3 = lambda b, c: (b, 0, 0)
    per_seq4 = lambda b, c: (b, 0, 0, 0)
    in_specs = [
        pl.BlockSpec((bt, c_len, GROUP_COLS), frames),
        pl.BlockSpec((bt, 1, GROUP_COLS), per_seq3),
        pl.BlockSpec((bt, N_PAIRS, PAIR, PAIR), per_seq4),
    ]
    args = [proj.reshape(bsz, t_len, PROJ_COLS), shift0, s0]
    if layer > 0:
        in_specs.append(pl.BlockSpec((bt, c_len, DA), frames))
        args.append(vfirst)
    in_specs += [
        pl.BlockSpec((1, GROUP_COLS), const2),
        pl.BlockSpec((8, DA), const2),
        pl.BlockSpec((LANES, DA), const2),
        pl.BlockSpec((LANES, DA), const2),
        pl.BlockSpec((LORA_G_PAD, DA), const2),
    ]
    args += [mu_p, pv, w2_p, a2_p, g2_p]
    if layer > 0:
        in_specs += [pl.BlockSpec((DA, LANES), const2), pl.BlockSpec((LANES, DA), const2)]
        args += [vw1_p, vw2_p]
    out_specs = [
        pl.BlockSpec((bt, c_len, DA), frames),
        pl.BlockSpec((bt, 1, GROUP_COLS), per_seq3),
        pl.BlockSpec((bt, N_PAIRS, PAIR, PAIR), per_seq4),
    ]
    out_shape = [
        jax.ShapeDtypeStruct((bsz, t_len, DA), BF16),
        jax.ShapeDtypeStruct((bsz, 1, GROUP_COLS), F32),
        jax.ShapeDtypeStruct((bsz, N_PAIRS, PAIR, PAIR), F32),
    ]
    if layer == 0:
        out_specs.append(pl.BlockSpec((bt, c_len, DA), frames))
        out_shape.append(jax.ShapeDtypeStruct((bsz, t_len, DA), F32))
    return pl.pallas_call(
        functools.partial(_rwkv_kernel, layer, bt, c_len),
        grid=(bsz // bt, n_chunks),
        in_specs=in_specs,
        out_specs=out_specs,
        out_shape=out_shape,
        scratch_shapes=[pltpu.VMEM((bt, c_len + 8, GROUP_COLS), F32)],
        compiler_params=pltpu.CompilerParams(
            dimension_semantics=("arbitrary", "arbitrary"), vmem_limit_bytes=VMEM_LIMIT),
        name=f"rwkv{layer}",
    )(*args)


def _hgrn_kernel(layer, bt, tb, hg_ref, lg_ref, nw_ref, s0_ref, on_ref, s_out_ref, st_ref):
    t = pl.program_id(1)
    n_blk = tb // HGRN_BLOCK

    @pl.when(t == 0)
    def _():
        for b in range(bt):
            for h in range(B_HEADS):
                st_ref[b, h] = s0_ref[b, h].T

    lg = lg_ref[...]
    lrows = [lg[i:i + 1, :] for i in range(DEPTH)]
    mx = functools.reduce(jnp.maximum, lrows)
    ex = [jnp.exp(x - mx) for x in lrows]
    den = functools.reduce(lambda a, b: a + b, ex)
    sm = [e / den for e in ex]
    lb = functools.reduce(lambda a, b: a + b, sm[:layer + 1]) - sm[0]

    row = lax.broadcasted_iota(jnp.int32, (tb, tb), 0)
    col = lax.broadcasted_iota(jnp.int32, (tb, tb), 1)
    blk_shift = _log2(HGRN_BLOCK)
    causal = jnp.logical_and(lax.shift_right_logical(row, blk_shift) == lax.shift_right_logical(col, blk_shift),
                             row >= col)
    causal01 = jnp.where(causal, 1.0, 0.0).astype(BF16)

    qg, kg, k_end, decay, iv, og = [], [], [], [], [], []
    for b in range(bt):
        hg = hg_ref[b]
        q = hg[:, 0:BK]
        fz = hg[:, BK:2 * BK]
        iv.append(hg[:, 2 * BK:2 * BK + DB])
        og.append(hg[:, 2 * BK + DB:])
        f = lb + (1.0 - lb) * _sigmoid(fz)
        kx = 1.0 - f
        g_cum = _dot_01(causal01, jnp.log(f))
        g_end = jnp.concatenate(
            [jnp.broadcast_to(g_cum[(j + 1) * HGRN_BLOCK - 1:(j + 1) * HGRN_BLOCK, :], (HGRN_BLOCK, BK))
             for j in range(n_blk)], axis=0)
        qg.append((q * _sigmoid(q) * jnp.exp(g_cum)).astype(BF16))
        kg.append((kx * jnp.exp(-g_cum)).astype(BF16))
        k_end.append((kx * jnp.exp(g_end - g_cum)).astype(BF16))
        decay.append(jnp.exp(g_end))

    units = [(b, h) for b in range(bt) for h in range(B_HEADS)]

    def cut(x, u):
        return x[u[0]][:, u[1] * B_KEY_DIM:(u[1] + 1) * B_KEY_DIM]

    def blk(j):
        return slice(j * HGRN_BLOCK, (j + 1) * HGRN_BLOCK)

    att = {u: jnp.where(causal, _dot_nt(cut(qg, u), cut(kg, u)), 0.0) for u in units}
    o_acc = {u: _dot(att[u], cut(iv, u)) for u in units}
    upd = {(u, j): _dot_tn(cut(iv, u)[blk(j)], cut(k_end, u)[blk(j)]) for j in range(n_blk) for u in units}
    starts = {}
    for u in units:
        s = st_ref[u[0], u[1]]
        dec = cut(decay, u)
        for j in range(n_blk):
            starts[(u, j)] = s.astype(BF16)
            s = s * dec[j * HGRN_BLOCK:j * HGRN_BLOCK + 1, :] + upd[(u, j)]
        st_ref[u[0], u[1]] = s
    inter = {(u, j): _dot_nt(cut(qg, u)[blk(j)], starts[(u, j)]) for j in range(n_blk) for u in units}

    for b in range(bt):
        outs = []
        for h in range(B_HEADS):
            u = (b, h)
            o_h = o_acc[u] + jnp.concatenate([inter[(u, j)] for j in range(n_blk)], axis=0)
            outs.append(o_h * lax.rsqrt(jnp.mean(o_h * o_h, axis=-1, keepdims=True) + RMS_EPS))
        o = jnp.concatenate(outs, axis=1)
        on_ref[b] = (o * nw_ref[...] * (og[b] * _sigmoid(og[b]))).astype(BF16)

    @pl.when(t == pl.num_programs(1) - 1)
    def _():
        for b in range(bt):
            for h in range(B_HEADS):
                s_out_ref[b, h] = st_ref[b, h].T


def _hgrn(layer, proj, lb_logits, norm_w, s0, bsz, t_len, tb, bt):
    n_blocks = t_len // tb
    return pl.pallas_call(
        functools.partial(_hgrn_kernel, layer, bt, tb),
        grid=(bsz // bt, n_blocks),
        in_specs=[
            pl.BlockSpec((bt, tb, GROUP_COLS), lambda b, t: (b, t, 1)),
            pl.BlockSpec((DEPTH, BK), lambda b, t: (0, 0)),
            pl.BlockSpec((1, DB), lambda b, t: (0, 0)),
            pl.BlockSpec((bt, B_HEADS, B_KEY_DIM, B_VAL_DIM), lambda b, t: (b, 0, 0, 0)),
        ],
        out_specs=[
            pl.BlockSpec((bt, tb, DB), lambda b, t: (b, t, 0)),
            pl.BlockSpec((bt, B_HEADS, B_KEY_DIM, B_VAL_DIM), lambda b, t: (b, 0, 0, 0)),
        ],
        out_shape=[
            jax.ShapeDtypeStruct((bsz, t_len, DB), BF16),
            jax.ShapeDtypeStruct((bsz, B_HEADS, B_KEY_DIM, B_VAL_DIM), F32),
        ],
        scratch_shapes=[pltpu.VMEM((bt, B_HEADS, B_VAL_DIM, B_KEY_DIM), F32)],
        compiler_params=pltpu.CompilerParams(
            dimension_semantics=("arbitrary", "arbitrary"), vmem_limit_bytes=VMEM_LIMIT),
        name=f"hgrn{layer}",
    )(proj.reshape(bsz, t_len, PROJ_COLS), lb_logits, norm_w, s0)


FF_CHUNK = 1024


def _mix_ffn_kernel(final, *refs):
    if final:
        (x_ref, yg_ref, on_ref, gt_ref, wa_ref, wb_ref, wo_ref, nf_ref, up_ref, dn_ref, nfin_ref, o_ref) = refs
    else:
        (x_ref, yg_ref, on_ref, gt_ref, wa_ref, wb_ref, wo_ref, nf_ref, up_ref, dn_ref, o_ref) = refs
    ya = jnp.dot(yg_ref[...], wa_ref[...], preferred_element_type=F32)
    yb = jnp.dot(on_ref[...], wb_ref[...], preferred_element_type=F32)
    gt = gt_ref[...]
    merged = gt[:, 0:D_MODEL] * ya + gt[:, D_MODEL:] * yb
    x = x_ref[...] + jnp.dot(merged.astype(BF16), wo_ref[...], preferred_element_type=F32)
    h = _rms(x, nf_ref[...]).astype(BF16)
    acc = x
    for cf in range(D_FF // FF_CHUNK):
        u = jnp.dot(h, up_ref[:, cf * FF_CHUNK:(cf + 1) * FF_CHUNK], preferred_element_type=F32)
        u = jnp.maximum(u, 0.0)
        acc = acc + jnp.dot((u * u).astype(BF16), dn_ref[cf * FF_CHUNK:(cf + 1) * FF_CHUNK, :],
                            preferred_element_type=F32)
    if final:
        acc = _rms(acc, nfin_ref[...])
    o_ref[...] = acc


def _mix_ffn(final, x, yg, on, proj, wa, wb, wo, nf, up, dn, nfin):
    n = x.shape[0]
    tm = min(n, 512)
    const = lambda i: (0, 0)
    resident = functools.partial(pl.BlockSpec, index_map=const, pipeline_mode=pl.Buffered(1))
    in_specs = [
        pl.BlockSpec((tm, D_MODEL), lambda i: (i, 0)),
        pl.BlockSpec((tm, DA), lambda i: (i, 0)),
        pl.BlockSpec((tm, DB), lambda i: (i, 0)),
        pl.BlockSpec((tm, GROUP_COLS), lambda i: (i, 2)),
        resident((DA, D_MODEL)),
        resident((DB, D_MODEL)),
        resident((D_MODEL, D_MODEL)),
        resident((1, D_MODEL)),
        resident((D_MODEL, D_FF)),
        resident((D_FF, D_MODEL)),
    ]
    args = [x, yg, on, proj, wa, wb, wo, nf, up, dn]
    if final:
        in_specs.append(resident((1, D_MODEL)))
        args.append(nfin)
    return pl.pallas_call(
        functools.partial(_mix_ffn_kernel, final),
        grid=(n // tm,),
        in_specs=in_specs,
        out_specs=pl.BlockSpec((tm, D_MODEL), lambda i: (i, 0)),
        out_shape=jax.ShapeDtypeStruct((n, D_MODEL), F32),
        compiler_params=pltpu.CompilerParams(dimension_semantics=("arbitrary",), vmem_limit_bytes=VMEM_LIMIT),
        name="mix_ffn_final" if final else "mix_ffn",
    )(*args)


def _pad_rows(w, rows, at=0):
    out = jnp.zeros((rows, w.shape[1]), w.dtype)
    return out.at[at:at + w.shape[0]].set(w)


def _prep_layer(l, p):
    w_in = p["w_in"][l]
    pad = jnp.zeros((D_MODEL, GROUP_COLS - RWKV_COLS), w_in.dtype)
    w_p = jnp.concatenate([w_in[:, :RWKV_COLS], pad, w_in[:, RWKV_COLS:]], axis=1).astype(BF16)
    mu_p = jnp.zeros((1, GROUP_COLS), F32).at[0, :RWKV_COLS].set(p["rwkv_mu"][l])
    v0 = p["rwkv_v0"][l - 1] if l > 0 else jnp.zeros((DA,), F32)
    pv = jnp.stack([p["rwkv_w0"][l], p["rwkv_a0"][l], p["rwkv_k_k"][l], p["rwkv_k_a"][l],
                    p["rwkv_r_k"][l].reshape(DA), p["rwkv_ln_w"][l], p["rwkv_ln_b"][l], v0])
    out = dict(
        w_p=w_p, mu_p=mu_p, pv=pv,
        w2_p=_pad_rows(p["rwkv_w2"][l], LANES, 0).astype(BF16),
        a2_p=_pad_rows(p["rwkv_a2"][l], LANES, W_LORA).astype(BF16),
        g2_p=_pad_rows(p["rwkv_g2"][l], LORA_G_PAD, 0).astype(BF16),
        vw1_p=None, vw2_p=None,
        norm_mix=p["norm_mix"][l].reshape(1, D_MODEL),
        hgrn_norm_w=p["hgrn_norm_w"][l].reshape(1, DB),
        wa=p["w_out_a"][l].astype(BF16), wb=p["w_out_b"][l].astype(BF16), wo=p["w_out"][l].astype(BF16),
        nf=p["norm_ffn"][l].reshape(1, D_MODEL),
        up=p["w_ffn_up"][l].astype(BF16), dn=p["w_ffn_down"][l].astype(BF16),
    )
    if l > 0:
        w1 = p["rwkv_vres_w1"][l - 1]
        out["vw1_p"] = jnp.zeros((DA, LANES), F32).at[:, :V_LORA].set(w1).astype(BF16)
        out["vw2_p"] = _pad_rows(p["rwkv_vres_w2"][l - 1], LANES, 0).astype(BF16)
    return out


def _pair_states(s):
    bsz = s.shape[0]
    out = jnp.zeros((bsz, N_PAIRS, PAIR, PAIR), s.dtype)
    out = out.at[:, :, :A_HEAD_DIM, :A_HEAD_DIM].set(s[:, 0::2])
    return out.at[:, :, A_HEAD_DIM:, A_HEAD_DIM:].set(s[:, 1::2])


def _unpair_states(s):
    bsz = s.shape[0]
    both = jnp.stack([s[:, :, :A_HEAD_DIM, :A_HEAD_DIM], s[:, :, A_HEAD_DIM:, A_HEAD_DIM:]], axis=2)
    return both.reshape(bsz, A_HEADS, A_HEAD_DIM, A_HEAD_DIM)


def _trunk(x, state_shift, state_rwkv, state_hgrn, layers, p, c_len, tb, bt):
    bsz, t_len, _ = x.shape
    xf = x.reshape(bsz * t_len, D_MODEL)
    nfin = p["norm_final"].reshape(1, D_MODEL)
    vfirst = None
    shifts, rwkv_states, hgrn_states = [], [], []
    for l in range(DEPTH):
        lp = layers[l]
        proj = _in_proj(xf, lp["norm_mix"], lp["w_p"])
        shift0 = jnp.zeros((bsz, 1, GROUP_COLS), F32).at[:, 0, :RWKV_COLS].set(state_shift[l])
        res = _rwkv(l, proj, shift0, _pair_states(state_rwkv[l]), vfirst, lp["mu_p"], lp["pv"], lp["w2_p"],
                    lp["a2_p"], lp["g2_p"], lp["vw1_p"], lp["vw2_p"], bsz, t_len, c_len, bt)
        if l == 0:
            yg, shift_out, s_rwkv, vfirst = res
        else:
            yg, shift_out, s_rwkv = res
        on, s_hgrn = _hgrn(l, proj, p["hgrn_lb_logits"], lp["hgrn_norm_w"], state_hgrn[l], bsz, t_len, tb, bt)
        xf = _mix_ffn(l == DEPTH - 1, xf, yg.reshape(bsz * t_len, DA), on.reshape(bsz * t_len, DB), proj,
                      lp["wa"], lp["wb"], lp["wo"], lp["nf"], lp["up"], lp["dn"], nfin)
        shifts.append(shift_out[:, 0, :RWKV_COLS])
        rwkv_states.append(_unpair_states(s_rwkv))
        hgrn_states.append(s_hgrn)
    return xf.reshape(bsz, t_len, D_MODEL), jnp.stack(shifts), jnp.stack(rwkv_states), jnp.stack(hgrn_states)


RWKV_CHUNK = 64
HGRN_TILE = 256
SEQ_TILE = 2


def kernel(x_prompt, x_sample, state_shift, state_rwkv, state_hgrn, norm_mix, w_in, rwkv_mu, rwkv_w0, rwkv_w2, rwkv_a0, rwkv_a2, rwkv_g2, rwkv_v0, rwkv_vres_w1, rwkv_vres_w2, rwkv_k_k, rwkv_k_a, rwkv_r_k, rwkv_ln_w, rwkv_ln_b, hgrn_lb_logits, hgrn_norm_w, w_out_a, w_out_b, w_out, norm_ffn, w_ffn_up, w_ffn_down, norm_final):
    p = dict(norm_mix=norm_mix, w_in=w_in, rwkv_mu=rwkv_mu, rwkv_w0=rwkv_w0, rwkv_w2=rwkv_w2, rwkv_a0=rwkv_a0,
             rwkv_a2=rwkv_a2, rwkv_g2=rwkv_g2, rwkv_v0=rwkv_v0, rwkv_vres_w1=rwkv_vres_w1,
             rwkv_vres_w2=rwkv_vres_w2, rwkv_k_k=rwkv_k_k, rwkv_k_a=rwkv_k_a, rwkv_r_k=rwkv_r_k,
             rwkv_ln_w=rwkv_ln_w, rwkv_ln_b=rwkv_ln_b, hgrn_lb_logits=hgrn_lb_logits, hgrn_norm_w=hgrn_norm_w,
             w_out_a=w_out_a, w_out_b=w_out_b, w_out=w_out, norm_ffn=norm_ffn, w_ffn_up=w_ffn_up,
             w_ffn_down=w_ffn_down, norm_final=norm_final)
    layers = [_prep_layer(l, p) for l in range(DEPTH)]
    bp, tp, _ = x_prompt.shape
    bs, ts, _ = x_sample.shape
    dt = x_prompt.dtype
    zero_shift = jnp.zeros((DEPTH, bp, RWKV_COLS), dt)
    zero_rwkv = jnp.zeros((DEPTH, bp, A_HEADS, A_HEAD_DIM, A_HEAD_DIM), dt)
    zero_hgrn = jnp.zeros((DEPTH, bp, B_HEADS, B_KEY_DIM, B_VAL_DIM), dt)
    y_prompt, shift_p, rwkv_p, hgrn_p = _trunk(x_prompt, zero_shift, zero_rwkv, zero_hgrn, layers, p,
                                               min(tp, RWKV_CHUNK), min(tp, HGRN_TILE), SEQ_TILE)
    y_sample, shift_s, rwkv_s, hgrn_s = _trunk(x_sample, state_shift, state_rwkv, state_hgrn, layers, p,
                                               min(ts, RWKV_CHUNK), min(ts, HGRN_TILE), SEQ_TILE)
    return (y_prompt, y_sample, shift_p, rwkv_p, hgrn_p, shift_s, rwkv_s, hgrn_s)
```

```python
import functools
import math

import jax
import jax.numpy as jnp
from jax import lax
from jax.experimental import pallas as pl
from jax.experimental.pallas import tpu as pltpu

F32 = jnp.float32
BF16 = jnp.bfloat16

D_MODEL = 1024
DEPTH = 2
A_HEADS = 8
A_HEAD_DIM = 64
DA = A_HEADS * A_HEAD_DIM
W_LORA = 64
A_LORA = 64
V_LORA = 32
G_LORA = 160
RWKV_COLS = 3 * DA + W_LORA + A_LORA + G_LORA
B_HEADS = 4
B_KEY_DIM = 128
B_VAL_DIM = 128
BK = B_HEADS * B_KEY_DIM
DB = B_HEADS * B_VAL_DIM
HGRN_COLS = 2 * BK + 2 * DB
GATE_COLS = 2 * D_MODEL
D_FF = 4 * D_MODEL
HGRN_BLOCK = 16
RMS_EPS = 1e-6
GN_EPS = 64e-5

LANES = 128
GROUP_COLS = 2048
PROJ_COLS = 2 * GROUP_COLS
LORA_WA = 3 * DA
LORA_G = LORA_WA + W_LORA + A_LORA
LORA_G_PAD = 256
PAIR = 2 * A_HEAD_DIM
N_PAIRS = A_HEADS // 2
VMEM_LIMIT = 56 * 1024 * 1024


def _dot(a, b):
    return jnp.dot(a.astype(BF16), b.astype(BF16), preferred_element_type=F32)


def _dot_nt(a, b):
    return lax.dot_general(a.astype(BF16), b.astype(BF16), (((1,), (1,)), ((), ())), preferred_element_type=F32)


def _dot_tn(a, b):
    return lax.dot_general(a.astype(BF16), b.astype(BF16), (((0,), (0,)), ((), ())), preferred_element_type=F32)


def _split3(x):
    hi = x.astype(BF16)
    r1 = x - hi.astype(F32)
    mid = r1.astype(BF16)
    return hi, mid, (r1 - mid.astype(F32)).astype(BF16)


def _dot_01(m01, x):
    return functools.reduce(lambda a, b: a + b, [jnp.dot(m01, t, preferred_element_type=F32) for t in _split3(x)])


def _dot_x01(x, m01):
    return functools.reduce(lambda a, b: a + b, [jnp.dot(t, m01, preferred_element_type=F32) for t in _split3(x)])


def _rms(x, w):
    return x * lax.rsqrt(jnp.mean(x * x, axis=-1, keepdims=True) + RMS_EPS) * w


def _sigmoid(x):
    return 1.0 / (1.0 + jnp.exp(-x))


def _log2(n):
    assert n & (n - 1) == 0, n
    return n.bit_length() - 1


def _in_proj_kernel(x_ref, nw_ref, w_ref, o_ref, h_ref):
    @pl.when(pl.program_id(1) == 0)
    def _():
        h_ref[...] = _rms(x_ref[...], nw_ref[...]).astype(BF16)

    o_ref[...] = jnp.dot(h_ref[...], w_ref[...], preferred_element_type=F32)


def _in_proj(x, norm_w, w_p):
    n = x.shape[0]
    tm = min(n, 1024)
    return pl.pallas_call(
        _in_proj_kernel,
        grid=(n // tm, PROJ_COLS // GROUP_COLS),
        in_specs=[
            pl.BlockSpec((tm, D_MODEL), lambda i, j: (i, 0)),
            pl.BlockSpec((1, D_MODEL), lambda i, j: (0, 0)),
            pl.BlockSpec((D_MODEL, GROUP_COLS), lambda i, j: (0, j)),
        ],
        out_specs=pl.BlockSpec((tm, GROUP_COLS), lambda i, j: (i, j)),
        out_shape=jax.ShapeDtypeStruct((n, PROJ_COLS), F32),
        scratch_shapes=[pltpu.VMEM((tm, D_MODEL), BF16)],
        compiler_params=pltpu.CompilerParams(
            dimension_semantics=("arbitrary", "arbitrary"), vmem_limit_bytes=VMEM_LIMIT),
        name="in_proj",
    )(x, norm_w, w_p)


def _rwkv_kernel(layer, bt, nch, c_len, *refs):
    if layer == 0:
        (rw_ref, shift_ref, s0_ref, mu_ref, pv_ref, w2_ref, a2_ref, g2_ref,
         yg_ref, shift_out_ref, s_out_ref, vfirst_out_ref, xs_ref, s_ref) = refs
    else:
        (rw_ref, shift_ref, s0_ref, vfirst_ref, mu_ref, pv_ref, w2_ref, a2_ref, g2_ref, vw1_ref, vw2_ref,
         yg_ref, shift_out_ref, s_out_ref, xs_ref, s_ref) = refs
    c = pl.program_id(1)
    frames = nch * c_len
    rows = bt * frames
    stk = 2 * c_len

    hrow = lax.broadcasted_iota(jnp.int32, (A_HEAD_DIM, PAIR), 0)
    hlane = lax.broadcasted_iota(jnp.int32, (A_HEAD_DIM, PAIR), 1)
    to_lo = jnp.where(hlane == hrow, 1.0, 0.0).astype(BF16)
    to_hi = jnp.where(hlane == hrow + A_HEAD_DIM, 1.0, 0.0).astype(BF16)

    @pl.when(c == 0)
    def _():
        for b in range(bt):
            xs_ref[b, 7:8, :] = shift_ref[b]
            for p in range(N_PAIRS):
                s_ref[b, p] = jnp.concatenate(
                    [_dot_x01(s0_ref[b, 2 * p], to_lo), _dot_x01(s0_ref[b, 2 * p + 1], to_hi)], axis=0)

    rws, prevs = [], []
    for b in range(bt):
        rw_b = rw_ref[b]
        xs_ref[b, 8:8 + frames, :] = rw_b
        prevs.append(xs_ref[b, 7:7 + frames, :])
        last = rw_b[frames - 1:frames, :]
        xs_ref[b, 7:8, :] = last
        shift_out_ref[b] = last
        rws.append(rw_b)
    rw = jnp.concatenate(rws, axis=0)
    prev = jnp.concatenate(prevs, axis=0)
    mix = rw + (prev - rw) * mu_ref[...]

    r = mix[:, 0:DA]
    k = mix[:, DA:2 * DA]
    v = mix[:, 2 * DA:3 * DA]
    wa = mix[:, LORA_WA:LORA_G]
    gl = mix[:, LORA_G:LORA_G + LORA_G_PAD]

    w0 = pv_ref[0:1, :]
    a0 = pv_ref[1:2, :]
    k_k = pv_ref[2:3, :]
    k_a = pv_ref[3:4, :]
    r_k = pv_ref[4:5, :]
    ln_w = pv_ref[5:6, :]
    ln_b = pv_ref[6:7, :]

    w_raw = w0 + _dot(jnp.tanh(wa), w2_ref[...])
    a_gate = _sigmoid(a0 + _dot(wa, a2_ref[...]))
    g = _dot(_sigmoid(gl), g2_ref[...])
    lw = (-math.exp(-0.5)) * _sigmoid(w_raw)
    if layer == 0:
        for b in range(bt):
            vfirst_out_ref[b] = v[b * frames:(b + 1) * frames]
    else:
        v0 = pv_ref[7:8, :]
        vg = _sigmoid(v0 + _dot(_dot(v, vw1_ref[...]), vw2_ref[...]))
        vfirst = jnp.concatenate([vfirst_ref[b] for b in range(bt)], axis=0)
        v = v + (vfirst - v) * vg

    lane = lax.broadcasted_iota(jnp.int32, (PAIR, PAIR), 1)
    sub = lax.broadcasted_iota(jnp.int32, (PAIR, PAIR), 0)
    head_ones = jnp.where((lane < A_HEAD_DIM) == (sub < A_HEAD_DIM), 1.0, 0.0).astype(BF16)

    def seg_sum(x):
        return jnp.concatenate(
            [jnp.dot(x[:, p * PAIR:(p + 1) * PAIR].astype(BF16), head_ones, preferred_element_type=F32)
             for p in range(N_PAIRS)], axis=1)

    kk = k * k_k
    kk = kk * lax.rsqrt(jnp.maximum(seg_sum(kk * kk), 1e-24))
    k_h = k * (1.0 + (a_gate - 1.0) * k_a)
    a_vec = -kk
    b_vec = kk * a_gate

    row = lax.broadcasted_iota(jnp.int32, (rows, rows), 0)
    col = lax.broadcasted_iota(jnp.int32, (rows, rows), 1)
    chunk_shift = _log2(c_len)
    causal = jnp.logical_and(
        lax.shift_right_logical(row, chunk_shift) == lax.shift_right_logical(col, chunk_shift), row >= col)
    g_cum = _dot_01(jnp.where(causal, 1.0, 0.0).astype(BF16), lw)
    g_end = jnp.concatenate(
        [jnp.broadcast_to(g_cum[(q + 1) * c_len - 1:(q + 1) * c_len, :], (c_len, DA)) for q in range(bt * nch)],
        axis=0)
    e_pos = jnp.exp(g_cum)
    e_neg = jnp.exp(-g_cum)
    e_end = jnp.exp(g_end - g_cum)
    rq = r * e_pos
    aq = a_vec * jnp.exp(g_cum - lw)
    kn = k_h * e_neg
    bn = b_vec * e_neg
    k_end = k_h * e_end
    b_end = b_vec * e_end
    decay_end = jnp.exp(g_end)

    srow = lax.broadcasted_iota(jnp.int32, (stk, PAIR), 0)
    slane = lax.broadcasted_iota(jnp.int32, (stk, PAIR), 1)
    own_lanes = (srow < c_len) == (slane < A_HEAD_DIM)

    def stack(x):
        return jnp.where(own_lanes, jnp.concatenate([x, x], axis=0), 0.0).astype(BF16)

    mrow = lax.broadcasted_iota(jnp.int32, (stk, stk), 0)
    mcol = lax.broadcasted_iota(jnp.int32, (stk, stk), 1)
    strict = (mrow & (c_len - 1)) > (mcol & (c_len - 1))
    lower = (mrow & (c_len - 1)) >= (mcol & (c_len - 1))
    eye = jnp.where(mrow == mcol, 1.0, 0.0).astype(F32)

    def cut(x, u):
        b, ch, p = u
        r0 = (b * nch + ch) * c_len
        return x[r0:r0 + c_len, p * PAIR:(p + 1) * PAIR]

    lhs, v_st, a_ak_v, a_rk, a_rb, t_inv, y_parts = {}, {}, {}, {}, {}, {}, {}

    def scores_and_inverse(units):
        power = {}
        for u in units:
            lhs[u] = jnp.concatenate([stack(cut(aq, u)), stack(cut(rq, u))], axis=0)
            v_st[u] = stack(cut(v, u))
            sc_b = _dot_nt(lhs[u], stack(cut(bn, u)))
            sc_k = _dot_nt(lhs[u], stack(cut(kn, u)))
            power[u] = jnp.where(strict, sc_b[:stk], 0.0)
            a_rb[u] = jnp.where(lower, sc_b[stk:], 0.0)
            a_rk[u] = jnp.where(lower, sc_k[stk:], 0.0)
            a_ak_v[u] = jnp.where(strict, sc_k[:stk], 0.0)
        yield
        for u in units:
            t_inv[u] = eye + power[u]
            a_ak_v[u] = _dot(a_ak_v[u], v_st[u])
            power[u] = _dot(power[u], power[u])
        yield
        for lvl in range(1, _log2(c_len)):
            new_power = {}
            for u in units:
                t_inv[u] = t_inv[u] + _dot(power[u], t_inv[u])
                if lvl < _log2(c_len) - 1:
                    new_power[u] = _dot(power[u], power[u])
            power = new_power
            yield

    def apply_state(units):
        from_state = {u: _dot_nt(lhs[u], s_ref[u[0], u[2]]) for u in units}
        yield
        corr = {u: _dot(t_inv[u], from_state[u][:stk] + a_ak_v[u]) for u in units}
        yield
        for u in units:
            vu = jnp.concatenate([v_st[u], corr[u].astype(BF16)], axis=0)
            if stk % LANES == 0:
                y_st = from_state[u][stk:] + _dot(jnp.concatenate([a_rk[u], a_rb[u]], axis=1), vu)
            else:
                y_st = from_state[u][stk:] + _dot(a_rk[u], v_st[u]) + _dot(a_rb[u], corr[u])
            y_parts[u] = y_st[:c_len] + y_st[c_len:]
            ends = jnp.concatenate([stack(cut(k_end, u)), stack(cut(b_end, u))], axis=0)
            s_ref[u[0], u[2]] = s_ref[u[0], u[2]] * cut(decay_end, u)[0:1, :] + _dot_tn(vu, ends)
        yield

    def interleave(*gens):
        live = list(gens)
        while live:
            for gen in list(live):
                if next(gen, "done") == "done":
                    live.remove(gen)

    chunk_units = [[(b, ch, p) for b in range(bt) for p in range(N_PAIRS)] for ch in range(nch)]
    interleave(scores_and_inverse(chunk_units[0]))
    for ch in range(nch):
        if ch + 1 < nch:
            interleave(scores_and_inverse(chunk_units[ch + 1]), apply_state(chunk_units[ch]))
        else:
            interleave(apply_state(chunk_units[ch]))

    y = jnp.concatenate(
        [jnp.concatenate([y_parts[(b, ch, p)] for p in range(N_PAIRS)], axis=1)
         for b in range(bt) for ch in range(nch)], axis=0)
    inv_n = 1.0 / A_HEAD_DIM
    mean = seg_sum(y) * inv_n
    bonus = seg_sum(r * k_h * r_k)
    dev = y - mean
    var = seg_sum(dev * dev) * inv_n
    y = dev * lax.rsqrt(var + GN_EPS) * ln_w + ln_b
    y = ((y + bonus * v) * g).astype(BF16)
    for b in range(bt):
        yg_ref[b] = y[b * frames:(b + 1) * frames]

    @pl.when(c == pl.num_programs(1) - 1)
    def _():
        prow = lax.broadcasted_iota(jnp.int32, (PAIR, A_HEAD_DIM), 0)
        pcol = lax.broadcasted_iota(jnp.int32, (PAIR, A_HEAD_DIM), 1)
        from_lo = jnp.where(prow == pcol, 1.0, 0.0).astype(BF16)
        from_hi = jnp.where(prow == pcol + A_HEAD_DIM, 1.0, 0.0).astype(BF16)
        for b in range(bt):
            for p in range(N_PAIRS):
                s_pair = s_ref[b, p]
                s_out_ref[b, 2 * p] = _dot_x01(s_pair[:A_HEAD_DIM], from_lo)
                s_out_ref[b, 2 * p + 1] = _dot_x01(s_pair[A_HEAD_DIM:], from_hi)


def _rwkv(layer, proj, shift0, s0, vfirst, mu_p, pv, w2_p, a2_p, g2_p, vw1_p, vw2_p, bsz, t_len, c_len, nch, bt):
    frames = nch * c_len
    const2 = lambda b, c: (0, 0)
    step = lambda b, c: (b, c, 0)
    per_seq3 = lambda b, c: (b, 0, 0)
    per_seq4 = lambda b, c: (b, 0, 0, 0)
    in_specs = [
        pl.BlockSpec((bt, frames, GROUP_COLS), step),
        pl.BlockSpec((bt, 1, GROUP_COLS), per_seq3),
        pl.BlockSpec((bt, A_HEADS, A_HEAD_DIM, A_HEAD_DIM), per_seq4),
    ]
    args = [proj.reshape(bsz, t_len, PROJ_COLS), shift0, s0]
    if layer > 0:
        in_specs.append(pl.BlockSpec((bt, frames, DA), step))
        args.append(vfirst)
    in_specs += [
        pl.BlockSpec((1, GROUP_COLS), const2),
        pl.BlockSpec((8, DA), const2),
        pl.BlockSpec((LANES, DA), const2),
        pl.BlockSpec((LANES, DA), const2),
        pl.BlockSpec((LORA_G_PAD, DA), const2),
    ]
    args += [mu_p, pv, w2_p, a2_p, g2_p]
    if layer > 0:
        in_specs += [pl.BlockSpec((DA, LANES), const2), pl.BlockSpec((LANES, DA), const2)]
        args += [vw1_p, vw2_p]
    out_specs = [
        pl.BlockSpec((bt, frames, DA), step),
        pl.BlockSpec((bt, 1, GROUP_COLS), per_seq3),
        pl.BlockSpec((bt, A_HEADS, A_HEAD_DIM, A_HEAD_DIM), per_seq4),
    ]
    out_shape = [
        jax.ShapeDtypeStruct((bsz, t_len, DA), BF16),
        jax.ShapeDtypeStruct((bsz, 1, GROUP_COLS), F32),
        jax.ShapeDtypeStruct((bsz, A_HEADS, A_HEAD_DIM, A_HEAD_DIM), F32),
    ]
    if layer == 0:
        out_specs.append(pl.BlockSpec((bt, frames, DA), step))
        out_shape.append(jax.ShapeDtypeStruct((bsz, t_len, DA), F32))
    return pl.pallas_call(
        functools.partial(_rwkv_kernel, layer, bt, nch, c_len),
        grid=(bsz // bt, t_len // frames),
        in_specs=in_specs,
        out_specs=out_specs,
        out_shape=out_shape,
        scratch_shapes=[pltpu.VMEM((bt, frames + 8, GROUP_COLS), F32),
                        pltpu.VMEM((bt, N_PAIRS, PAIR, PAIR), F32)],
        compiler_params=pltpu.CompilerParams(
            dimension_semantics=("arbitrary", "arbitrary"), vmem_limit_bytes=VMEM_LIMIT),
        name=f"rwkv{layer}",
    )(*args)


def _hgrn_kernel(layer, bt, tb, hg_ref, lg_ref, nw_ref, s0_ref, on_ref, s_out_ref, st_ref):
    t = pl.program_id(1)
    n_blk = tb // HGRN_BLOCK

    @pl.when(t == 0)
    def _():
        for b in range(bt):
            for h in range(B_HEADS):
                st_ref[b, h] = s0_ref[b, h].T

    lg = lg_ref[...]
    lrows = [lg[i:i + 1, :] for i in range(DEPTH)]
    mx = functools.reduce(jnp.maximum, lrows)
    ex = [jnp.exp(x - mx) for x in lrows]
    den = functools.reduce(lambda a, b: a + b, ex)
    sm = [e / den for e in ex]
    lb = functools.reduce(lambda a, b: a + b, sm[:layer + 1]) - sm[0]

    row = lax.broadcasted_iota(jnp.int32, (tb, tb), 0)
    col = lax.broadcasted_iota(jnp.int32, (tb, tb), 1)
    blk_shift = _log2(HGRN_BLOCK)
    causal = jnp.logical_and(lax.shift_right_logical(row, blk_shift) == lax.shift_right_logical(col, blk_shift),
                             row >= col)
    causal01 = jnp.where(causal, 1.0, 0.0).astype(BF16)

    qg, kg, k_end, decay, iv, og = [], [], [], [], [], []
    for b in range(bt):
        hg = hg_ref[b]
        q = hg[:, 0:BK]
        fz = hg[:, BK:2 * BK]
        iv.append(hg[:, 2 * BK:2 * BK + DB])
        og.append(hg[:, 2 * BK + DB:])
        f = lb + (1.0 - lb) * _sigmoid(fz)
        kx = 1.0 - f
        g_cum = _dot_01(causal01, jnp.log(f))
        g_end = jnp.concatenate(
            [jnp.broadcast_to(g_cum[(j + 1) * HGRN_BLOCK - 1:(j + 1) * HGRN_BLOCK, :], (HGRN_BLOCK, BK))
             for j in range(n_blk)], axis=0)
        qg.append((q * _sigmoid(q) * jnp.exp(g_cum)).astype(BF16))
        kg.append((kx * jnp.exp(-g_cum)).astype(BF16))
        k_end.append((kx * jnp.exp(g_end - g_cum)).astype(BF16))
        decay.append(jnp.exp(g_end))

    units = [(b, h) for b in range(bt) for h in range(B_HEADS)]

    def cut(x, u):
        return x[u[0]][:, u[1] * B_KEY_DIM:(u[1] + 1) * B_KEY_DIM]

    def blk(j):
        return slice(j * HGRN_BLOCK, (j + 1) * HGRN_BLOCK)

    att = {u: jnp.where(causal, _dot_nt(cut(qg, u), cut(kg, u)), 0.0) for u in units}
    o_acc = {u: _dot(att[u], cut(iv, u)) for u in units}
    upd = {(u, j): _dot_tn(cut(iv, u)[blk(j)], cut(k_end, u)[blk(j)]) for j in range(n_blk) for u in units}
    starts = {}
    for u in units:
        s = st_ref[u[0], u[1]]
        dec = cut(decay, u)
        for j in range(n_blk):
            starts[(u, j)] = s.astype(BF16)
            s = s * dec[j * HGRN_BLOCK:j * HGRN_BLOCK + 1, :] + upd[(u, j)]
        st_ref[u[0], u[1]] = s
    inter = {(u, j): _dot_nt(cut(qg, u)[blk(j)], starts[(u, j)]) for j in range(n_blk) for u in units}

    for b in range(bt):
        outs = []
        for h in range(B_HEADS):
            u = (b, h)
            o_h = o_acc[u] + jnp.concatenate([inter[(u, j)] for j in range(n_blk)], axis=0)
            outs.append(o_h * lax.rsqrt(jnp.mean(o_h * o_h, axis=-1, keepdims=True) + RMS_EPS))
        o = jnp.concatenate(outs, axis=1)
        on_ref[b] = (o * nw_ref[...] * (og[b] * _sigmoid(og[b]))).astype(BF16)

    @pl.when(t == pl.num_programs(1) - 1)
    def _():
        for b in range(bt):
            for h in range(B_HEADS):
                s_out_ref[b, h] = st_ref[b, h].T


def _hgrn(layer, proj, lb_logits, norm_w, s0, bsz, t_len, tb, bt):
    n_blocks = t_len // tb
    return pl.pallas_call(
        functools.partial(_hgrn_kernel, layer, bt, tb),
        grid=(bsz // bt, n_blocks),
        in_specs=[
            pl.BlockSpec((bt, tb, GROUP_COLS), lambda b, t: (b, t, 1)),
            pl.BlockSpec((DEPTH, BK), lambda b, t: (0, 0)),
            pl.BlockSpec((1, DB), lambda b, t: (0, 0)),
            pl.BlockSpec((bt, B_HEADS, B_KEY_DIM, B_VAL_DIM), lambda b, t: (b, 0, 0, 0)),
        ],
        out_specs=[
            pl.BlockSpec((bt, tb, DB), lambda b, t: (b, t, 0)),
            pl.BlockSpec((bt, B_HEADS, B_KEY_DIM, B_VAL_DIM), lambda b, t: (b, 0, 0, 0)),
        ],
        out_shape=[
            jax.ShapeDtypeStruct((bsz, t_len, DB), BF16),
            jax.ShapeDtypeStruct((bsz, B_HEADS, B_KEY_DIM, B_VAL_DIM), F32),
        ],
        scratch_shapes=[pltpu.VMEM((bt, B_HEADS, B_VAL_DIM, B_KEY_DIM), F32)],
        compiler_params=pltpu.CompilerParams(
            dimension_semantics=("arbitrary", "arbitrary"), vmem_limit_bytes=VMEM_LIMIT),
        name=f"hgrn{layer}",
    )(proj.reshape(bsz, t_len, PROJ_COLS), lb_logits, norm_w, s0)


FF_CHUNK = 1024


def _mix_ffn_kernel(final, *refs):
    if final:
        (x_ref, yg_ref, on_ref, nm_ref, wg_ref, wa_ref, wb_ref, wo_ref, nf_ref, up_ref, dn_ref, nfin_ref,
         o_ref) = refs
    else:
        (x_ref, yg_ref, on_ref, nm_ref, wg_ref, wa_ref, wb_ref, wo_ref, nf_ref, up_ref, dn_ref, o_ref) = refs
    x = x_ref[...]
    gt = _sigmoid(jnp.dot(_rms(x, nm_ref[...]).astype(BF16), wg_ref[...], preferred_element_type=F32))
    ya = jnp.dot(yg_ref[...], wa_ref[...], preferred_element_type=F32)
    yb = jnp.dot(on_ref[...], wb_ref[...], preferred_element_type=F32)
    merged = gt[:, 0:D_MODEL] * ya + gt[:, D_MODEL:] * yb
    x = x + jnp.dot(merged.astype(BF16), wo_ref[...], preferred_element_type=F32)
    h = _rms(x, nf_ref[...]).astype(BF16)
    acc = x
    for cf in range(D_FF // FF_CHUNK):
        u = jnp.dot(h, up_ref[:, cf * FF_CHUNK:(cf + 1) * FF_CHUNK], preferred_element_type=F32)
        u = jnp.maximum(u, 0.0)
        acc = acc + jnp.dot((u * u).astype(BF16), dn_ref[cf * FF_CHUNK:(cf + 1) * FF_CHUNK, :],
                            preferred_element_type=F32)
    if final:
        acc = _rms(acc, nfin_ref[...])
    o_ref[...] = acc


def _mix_ffn(final, x, yg, on, nm, wg, wa, wb, wo, nf, up, dn, nfin):
    n = x.shape[0]
    tm = min(n, 512)
    const = lambda i: (0, 0)
    resident = functools.partial(pl.BlockSpec, index_map=const, pipeline_mode=pl.Buffered(1))
    in_specs = [
        pl.BlockSpec((tm, D_MODEL), lambda i: (i, 0)),
        pl.BlockSpec((tm, DA), lambda i: (i, 0)),
        pl.BlockSpec((tm, DB), lambda i: (i, 0)),
        resident((1, D_MODEL)),
        resident((D_MODEL, GATE_COLS)),
        resident((DA, D_MODEL)),
        resident((DB, D_MODEL)),
        resident((D_MODEL, D_MODEL)),
        resident((1, D_MODEL)),
        resident((D_MODEL, D_FF)),
        resident((D_FF, D_MODEL)),
    ]
    args = [x, yg, on, nm, wg, wa, wb, wo, nf, up, dn]
    if final:
        in_specs.append(resident((1, D_MODEL)))
        args.append(nfin)
    return pl.pallas_call(
        functools.partial(_mix_ffn_kernel, final),
        grid=(n // tm,),
        in_specs=in_specs,
        out_specs=pl.BlockSpec((tm, D_MODEL), lambda i: (i, 0)),
        out_shape=jax.ShapeDtypeStruct((n, D_MODEL), F32),
        compiler_params=pltpu.CompilerParams(dimension_semantics=("arbitrary",), vmem_limit_bytes=VMEM_LIMIT),
        name="mix_ffn_final" if final else "mix_ffn",
    )(*args)


def _pad_rows(w, rows, at=0):
    out = jnp.zeros((rows, w.shape[1]), w.dtype)
    return out.at[at:at + w.shape[0]].set(w)


def _prep_layer(l, p):
    w_in = p["w_in"][l]
    pad = jnp.zeros((D_MODEL, GROUP_COLS - RWKV_COLS), w_in.dtype)
    w_p = jnp.concatenate([w_in[:, :RWKV_COLS], pad, w_in[:, RWKV_COLS:RWKV_COLS + HGRN_COLS]], axis=1).astype(BF16)
    mu_p = jnp.zeros((1, GROUP_COLS), F32).at[0, :RWKV_COLS].set(p["rwkv_mu"][l])
    v0 = p["rwkv_v0"][l - 1] if l > 0 else jnp.zeros((DA,), F32)
    pv = jnp.stack([p["rwkv_w0"][l], p["rwkv_a0"][l], p["rwkv_k_k"][l], p["rwkv_k_a"][l],
                    p["rwkv_r_k"][l].reshape(DA), p["rwkv_ln_w"][l], p["rwkv_ln_b"][l], v0])
    out = dict(
        w_p=w_p, mu_p=mu_p, pv=pv,
        wg=w_in[:, RWKV_COLS + HGRN_COLS:].astype(BF16),
        w2_p=_pad_rows(p["rwkv_w2"][l], LANES, 0).astype(BF16),
        a2_p=_pad_rows(p["rwkv_a2"][l], LANES, W_LORA).astype(BF16),
        g2_p=_pad_rows(p["rwkv_g2"][l], LORA_G_PAD, 0).astype(BF16),
        vw1_p=None, vw2_p=None,
        norm_mix=p["norm_mix"][l].reshape(1, D_MODEL),
        hgrn_norm_w=p["hgrn_norm_w"][l].reshape(1, DB),
        wa=p["w_out_a"][l].astype(BF16), wb=p["w_out_b"][l].astype(BF16), wo=p["w_out"][l].astype(BF16),
        nf=p["norm_ffn"][l].reshape(1, D_MODEL),
        up=p["w_ffn_up"][l].astype(BF16), dn=p["w_ffn_down"][l].astype(BF16),
    )
    if l > 0:
        w1 = p["rwkv_vres_w1"][l - 1]
        out["vw1_p"] = jnp.zeros((DA, LANES), F32).at[:, :V_LORA].set(w1).astype(BF16)
        out["vw2_p"] = _pad_rows(p["rwkv_vres_w2"][l - 1], LANES, 0).astype(BF16)
    return out


def _trunk(x, state_shift, state_rwkv, state_hgrn, layers, p, c_len, nch, tb, bt):
    bsz, t_len, _ = x.shape
    xf = x.reshape(bsz * t_len, D_MODEL)
    nfin = p["norm_final"].reshape(1, D_MODEL)
    vfirst = None
    shifts, rwkv_states, hgrn_states = [], [], []
    for l in range(DEPTH):
        lp = layers[l]
        proj = _in_proj(xf, lp["norm_mix"], lp["w_p"])
        shift0 = jnp.zeros((bsz, 1, GROUP_COLS), F32).at[:, 0, :RWKV_COLS].set(state_shift[l])
        res = _rwkv(l, proj, shift0, state_rwkv[l], vfirst, lp["mu_p"], lp["pv"], lp["w2_p"],
                    lp["a2_p"], lp["g2_p"], lp["vw1_p"], lp["vw2_p"], bsz, t_len, c_len, nch, bt)
        if l == 0:
            yg, shift_out, s_rwkv, vfirst = res
        else:
            yg, shift_out, s_rwkv = res
        on, s_hgrn = _hgrn(l, proj, p["hgrn_lb_logits"], lp["hgrn_norm_w"], state_hgrn[l], bsz, t_len, tb, bt)
        xf = _mix_ffn(l == DEPTH - 1, xf, yg.reshape(bsz * t_len, DA), on.reshape(bsz * t_len, DB),
                      lp["norm_mix"], lp["wg"], lp["wa"], lp["wb"], lp["wo"], lp["nf"], lp["up"], lp["dn"], nfin)
        shifts.append(shift_out[:, 0, :RWKV_COLS])
        rwkv_states.append(s_rwkv)
        hgrn_states.append(s_hgrn)
    return xf.reshape(bsz, t_len, D_MODEL), jnp.stack(shifts), jnp.stack(rwkv_states), jnp.stack(hgrn_states)


RWKV_CHUNK = 64
RWKV_CHUNKS = 2
HGRN_TILE = 256
SEQ_TILE = 2


def _run_trunk(x, state_shift, state_rwkv, state_hgrn, layers, p):
    t_len = x.shape[1]
    c_len = min(t_len, RWKV_CHUNK)
    nch = min(t_len // c_len, RWKV_CHUNKS)
    return _trunk(x, state_shift, state_rwkv, state_hgrn, layers, p, c_len, nch, min(t_len, HGRN_TILE), SEQ_TILE)


def kernel(x_prompt, x_sample, state_shift, state_rwkv, state_hgrn, norm_mix, w_in, rwkv_mu, rwkv_w0, rwkv_w2, rwkv_a0, rwkv_a2, rwkv_g2, rwkv_v0, rwkv_vres_w1, rwkv_vres_w2, rwkv_k_k, rwkv_k_a, rwkv_r_k, rwkv_ln_w, rwkv_ln_b, hgrn_lb_logits, hgrn_norm_w, w_out_a, w_out_b, w_out, norm_ffn, w_ffn_up, w_ffn_down, norm_final):
    p = dict(norm_mix=norm_mix, w_in=w_in, rwkv_mu=rwkv_mu, rwkv_w0=rwkv_w0, rwkv_w2=rwkv_w2, rwkv_a0=rwkv_a0,
             rwkv_a2=rwkv_a2, rwkv_g2=rwkv_g2, rwkv_v0=rwkv_v0, rwkv_vres_w1=rwkv_vres_w1,
             rwkv_vres_w2=rwkv_vres_w2, rwkv_k_k=rwkv_k_k, rwkv_k_a=rwkv_k_a, rwkv_r_k=rwkv_r_k,
             rwkv_ln_w=rwkv_ln_w, rwkv_ln_b=rwkv_ln_b, hgrn_lb_logits=hgrn_lb_logits, hgrn_norm_w=hgrn_norm_w,
             w_out_a=w_out_a, w_out_b=w_out_b, w_out=w_out, norm_ffn=norm_ffn, w_ffn_up=w_ffn_up,
             w_ffn_down=w_ffn_down, norm_final=norm_final)
    layers = [_prep_layer(l, p) for l in range(DEPTH)]
    bp = x_prompt.shape[0]
    dt = x_prompt.dtype
    zero_shift = jnp.zeros((DEPTH, bp, RWKV_COLS), dt)
    zero_rwkv = jnp.zeros((DEPTH, bp, A_HEADS, A_HEAD_DIM, A_HEAD_DIM), dt)
    zero_hgrn = jnp.zeros((DEPTH, bp, B_HEADS, B_KEY_DIM, B_VAL_DIM), dt)
    y_prompt, shift_p, rwkv_p, hgrn_p = _run_trunk(x_prompt, zero_shift, zero_rwkv, zero_hgrn, layers, p)
    y_sample, shift_s, rwkv_s, hgrn_s = _run_trunk(x_sample, state_shift, state_rwkv, state_hgrn, layers, p)
    return (y_prompt, y_sample, shift_p, rwkv_p, hgrn_p, shift_s, rwkv_s, hgrn_s)
```

```python
import functools
import math

import jax
import jax.numpy as jnp
from jax import lax
from jax.experimental import pallas as pl
from jax.experimental.pallas import tpu as pltpu

F32 = jnp.float32
BF16 = jnp.bfloat16

D_MODEL = 1024
DEPTH = 2
A_HEADS = 8
A_HEAD_DIM = 64
DA = A_HEADS * A_HEAD_DIM
W_LORA = 64
A_LORA = 64
V_LORA = 32
G_LORA = 160
RWKV_COLS = 3 * DA + W_LORA + A_LORA + G_LORA
B_HEADS = 4
B_KEY_DIM = 128
B_VAL_DIM = 128
BK = B_HEADS * B_KEY_DIM
DB = B_HEADS * B_VAL_DIM
HGRN_COLS = 2 * BK + 2 * DB
GATE_COLS = 2 * D_MODEL
D_FF = 4 * D_MODEL
HGRN_BLOCK = 16
RMS_EPS = 1e-6
GN_EPS = 64e-5

LANES = 128
GROUP_COLS = 2048
PROJ_COLS = 2 * GROUP_COLS
LORA_WA = 3 * DA
LORA_G = LORA_WA + W_LORA + A_LORA
LORA_G_PAD = 256
PAIR = 2 * A_HEAD_DIM
N_PAIRS = A_HEADS // 2
VMEM_LIMIT = 56 * 1024 * 1024


def _dot(a, b):
    return jnp.dot(a.astype(BF16), b.astype(BF16), preferred_element_type=F32)


def _dot_nt(a, b):
    return lax.dot_general(a.astype(BF16), b.astype(BF16), (((1,), (1,)), ((), ())), preferred_element_type=F32)


def _dot_tn(a, b):
    return lax.dot_general(a.astype(BF16), b.astype(BF16), (((0,), (0,)), ((), ())), preferred_element_type=F32)


def _split3(x):
    hi = x.astype(BF16)
    r1 = x - hi.astype(F32)
    mid = r1.astype(BF16)
    return hi, mid, (r1 - mid.astype(F32)).astype(BF16)


def _dot_01(m01, x):
    return functools.reduce(lambda a, b: a + b, [jnp.dot(m01, t, preferred_element_type=F32) for t in _split3(x)])


def _dot_x01(x, m01):
    return functools.reduce(lambda a, b: a + b, [jnp.dot(t, m01, preferred_element_type=F32) for t in _split3(x)])


def _rms(x, w):
    return x * lax.rsqrt(jnp.mean(x * x, axis=-1, keepdims=True) + RMS_EPS) * w


def _sigmoid(x):
    return 1.0 / (1.0 + jnp.exp(-x))


def _log2(n):
    assert n & (n - 1) == 0, n
    return n.bit_length() - 1


def _resident(shape, index_map):
    return pl.BlockSpec(shape, index_map, pipeline_mode=pl.Buffered(1))


def _rwkv_kernel(layer, bt, nch, c_len, *refs):
    if layer == 0:
        (x_ref, shift_ref, s0_ref, nm_ref, w_ref, mu_ref, pv_ref, w2_ref, a2_ref, g2_ref,
         yg_ref, shift_out_ref, s_out_ref, vfirst_out_ref, xs_ref, s_ref) = refs
    else:
        (x_ref, shift_ref, s0_ref, vfirst_ref, nm_ref, w_ref, mu_ref, pv_ref, w2_ref, a2_ref, g2_ref, vw1_ref,
         vw2_ref, yg_ref, shift_out_ref, s_out_ref, xs_ref, s_ref) = refs
    c = pl.program_id(1)
    frames = nch * c_len
    rows = bt * frames
    stk = 2 * c_len

    hrow = lax.broadcasted_iota(jnp.int32, (A_HEAD_DIM, PAIR), 0)
    hlane = lax.broadcasted_iota(jnp.int32, (A_HEAD_DIM, PAIR), 1)
    to_lo = jnp.where(hlane == hrow, 1.0, 0.0).astype(BF16)
    to_hi = jnp.where(hlane == hrow + A_HEAD_DIM, 1.0, 0.0).astype(BF16)

    @pl.when(c == 0)
    def _():
        for b in range(bt):
            xs_ref[b, 7:8, :] = shift_ref[b]
            for p in range(N_PAIRS):
                s_ref[b, p] = jnp.concatenate(
                    [_dot_x01(s0_ref[b, 2 * p], to_lo), _dot_x01(s0_ref[b, 2 * p + 1], to_hi)], axis=0)

    h = jnp.concatenate([_rms(x_ref[b], nm_ref[...]).astype(BF16) for b in range(bt)], axis=0)

    def project(c0, c1):
        rw = jnp.dot(h, w_ref[:, c0:c1], preferred_element_type=F32)
        prevs = []
        for b in range(bt):
            rw_b = rw[b * frames:(b + 1) * frames]
            xs_ref[b, 8:8 + frames, c0:c1] = rw_b
            prevs.append(xs_ref[b, 7:7 + frames, c0:c1])
            last = rw_b[frames - 1:frames, :]
            xs_ref[b, 7:8, c0:c1] = last
            shift_out_ref[b, :, c0:c1] = last
        return rw + (jnp.concatenate(prevs, axis=0) - rw) * mu_ref[:, c0:c1]

    w0 = pv_ref[0:1, :]
    a0 = pv_ref[1:2, :]
    k_k = pv_ref[2:3, :]
    k_a = pv_ref[3:4, :]
    r_k = pv_ref[4:5, :]
    ln_w = pv_ref[5:6, :]
    ln_b = pv_ref[6:7, :]

    row = lax.broadcasted_iota(jnp.int32, (rows, rows), 0)
    col = lax.broadcasted_iota(jnp.int32, (rows, rows), 1)
    chunk_shift = _log2(c_len)
    causal = jnp.logical_and(
        lax.shift_right_logical(row, chunk_shift) == lax.shift_right_logical(col, chunk_shift), row >= col)

    lora = project(LORA_WA, GROUP_COLS)
    k = project(DA, 2 * DA)
    wa = lora[:, 0:LORA_G - LORA_WA]
    gl = lora[:, LORA_G - LORA_WA:LORA_G - LORA_WA + LORA_G_PAD]
    w_raw = w0 + _dot(jnp.tanh(wa), w2_ref[...])
    a_gate = _sigmoid(a0 + _dot(wa, a2_ref[...]))
    g = _dot(_sigmoid(gl), g2_ref[...])
    r = project(0, DA)
    lw = (-math.exp(-0.5)) * _sigmoid(w_raw)
    g_cum = _dot_01(jnp.where(causal, 1.0, 0.0).astype(BF16), lw)
    v = project(2 * DA, 3 * DA)
    if layer == 0:
        for b in range(bt):
            vfirst_out_ref[b] = v[b * frames:(b + 1) * frames]
    else:
        v0 = pv_ref[7:8, :]
        vg = _sigmoid(v0 + _dot(_dot(v, vw1_ref[...]), vw2_ref[...]))
        vfirst = jnp.concatenate([vfirst_ref[b] for b in range(bt)], axis=0)
        v = v + (vfirst - v) * vg

    lane = lax.broadcasted_iota(jnp.int32, (PAIR, PAIR), 1)
    sub = lax.broadcasted_iota(jnp.int32, (PAIR, PAIR), 0)
    head_ones = jnp.where((lane < A_HEAD_DIM) == (sub < A_HEAD_DIM), 1.0, 0.0).astype(BF16)

    def seg_sum(x):
        return jnp.concatenate(
            [jnp.dot(x[:, p * PAIR:(p + 1) * PAIR].astype(BF16), head_ones, preferred_element_type=F32)
             for p in range(N_PAIRS)], axis=1)

    kk = k * k_k
    kk = kk * lax.rsqrt(jnp.maximum(seg_sum(kk * kk), 1e-24))
    k_h = k * (1.0 + (a_gate - 1.0) * k_a)
    a_vec = -kk
    b_vec = kk * a_gate

    g_end = jnp.concatenate(
        [jnp.broadcast_to(g_cum[(q + 1) * c_len - 1:(q + 1) * c_len, :], (c_len, DA)) for q in range(bt * nch)],
        axis=0)
    e_pos = jnp.exp(g_cum)
    e_neg = jnp.exp(-g_cum)
    e_end = jnp.exp(g_end - g_cum)
    rq = r * e_pos
    aq = a_vec * jnp.exp(g_cum - lw)
    kn = k_h * e_neg
    bn = b_vec * e_neg
    k_end = k_h * e_end
    b_end = b_vec * e_end
    decay_end = jnp.exp(g_end)

    srow = lax.broadcasted_iota(jnp.int32, (stk, PAIR), 0)
    slane = lax.broadcasted_iota(jnp.int32, (stk, PAIR), 1)
    own_lanes = (srow < c_len) == (slane < A_HEAD_DIM)

    def stack(x):
        return jnp.where(own_lanes, jnp.concatenate([x, x], axis=0), 0.0).astype(BF16)

    mrow = lax.broadcasted_iota(jnp.int32, (stk, stk), 0)
    mcol = lax.broadcasted_iota(jnp.int32, (stk, stk), 1)
    strict = (mrow & (c_len - 1)) > (mcol & (c_len - 1))
    lower = (mrow & (c_len - 1)) >= (mcol & (c_len - 1))
    eye = jnp.where(mrow == mcol, 1.0, 0.0).astype(F32)

    def cut(x, u):
        b, ch, p = u
        r0 = (b * nch + ch) * c_len
        return x[r0:r0 + c_len, p * PAIR:(p + 1) * PAIR]

    lhs, v_st, a_ak_v, a_rk, a_rb, t_inv, y_parts = {}, {}, {}, {}, {}, {}, {}

    def scores_and_inverse(units):
        power = {}
        for u in units:
            lhs[u] = jnp.concatenate([stack(cut(aq, u)), stack(cut(rq, u))], axis=0)
            v_st[u] = stack(cut(v, u))
            sc_b = _dot_nt(lhs[u], stack(cut(bn, u)))
            sc_k = _dot_nt(lhs[u], stack(cut(kn, u)))
            power[u] = jnp.where(strict, sc_b[:stk], 0.0)
            a_rb[u] = jnp.where(lower, sc_b[stk:], 0.0)
            a_rk[u] = jnp.where(lower, sc_k[stk:], 0.0)
            a_ak_v[u] = jnp.where(strict, sc_k[:stk], 0.0)
        yield
        for u in units:
            t_inv[u] = eye + power[u]
            a_ak_v[u] = _dot(a_ak_v[u], v_st[u])
            power[u] = _dot(power[u], power[u])
        yield
        for lvl in range(1, _log2(c_len)):
            new_power = {}
            for u in units:
                t_inv[u] = t_inv[u] + _dot(power[u], t_inv[u])
                if lvl < _log2(c_len) - 1:
                    new_power[u] = _dot(power[u], power[u])
            power = new_power
            yield

    def apply_state(units):
        from_state = {u: _dot_nt(lhs[u], s_ref[u[0], u[2]]) for u in units}
        yield
        corr = {u: _dot(t_inv[u], from_state[u][:stk] + a_ak_v[u]) for u in units}
        yield
        for u in units:
            vu = jnp.concatenate([v_st[u], corr[u].astype(BF16)], axis=0)
            if stk % LANES == 0:
                y_st = from_state[u][stk:] + _dot(jnp.concatenate([a_rk[u], a_rb[u]], axis=1), vu)
            else:
                y_st = from_state[u][stk:] + _dot(a_rk[u], v_st[u]) + _dot(a_rb[u], corr[u])
            y_parts[u] = y_st[:c_len] + y_st[c_len:]
            ends = jnp.concatenate([stack(cut(k_end, u)), stack(cut(b_end, u))], axis=0)
            s_ref[u[0], u[2]] = s_ref[u[0], u[2]] * cut(decay_end, u)[0:1, :] + _dot_tn(vu, ends)
        yield

    def interleave(*gens):
        live = list(gens)
        while live:
            for gen in list(live):
                if next(gen, "done") == "done":
                    live.remove(gen)

    chunk_units = [[(b, ch, p) for b in range(bt) for p in range(N_PAIRS)] for ch in range(nch)]
    interleave(scores_and_inverse(chunk_units[0]))
    for ch in range(nch):
        if ch + 1 < nch:
            interleave(scores_and_inverse(chunk_units[ch + 1]), apply_state(chunk_units[ch]))
        else:
            interleave(apply_state(chunk_units[ch]))

    y = jnp.concatenate(
        [jnp.concatenate([y_parts[(b, ch, p)] for p in range(N_PAIRS)], axis=1)
         for b in range(bt) for ch in range(nch)], axis=0)
    inv_n = 1.0 / A_HEAD_DIM
    mean = seg_sum(y) * inv_n
    bonus = seg_sum(r * k_h * r_k)
    dev = y - mean
    var = seg_sum(dev * dev) * inv_n
    y = dev * lax.rsqrt(var + GN_EPS) * ln_w + ln_b
    y = ((y + bonus * v) * g).astype(BF16)
    for b in range(bt):
        yg_ref[b] = y[b * frames:(b + 1) * frames]

    @pl.when(c == pl.num_programs(1) - 1)
    def _():
        prow = lax.broadcasted_iota(jnp.int32, (PAIR, A_HEAD_DIM), 0)
        pcol = lax.broadcasted_iota(jnp.int32, (PAIR, A_HEAD_DIM), 1)
        from_lo = jnp.where(prow == pcol, 1.0, 0.0).astype(BF16)
        from_hi = jnp.where(prow == pcol + A_HEAD_DIM, 1.0, 0.0).astype(BF16)
        for b in range(bt):
            for p in range(N_PAIRS):
                s_pair = s_ref[b, p]
                s_out_ref[b, 2 * p] = _dot_x01(s_pair[:A_HEAD_DIM], from_lo)
                s_out_ref[b, 2 * p + 1] = _dot_x01(s_pair[A_HEAD_DIM:], from_hi)


def _rwkv(layer, x, shift0, s0_all, vfirst, nm, w_rw_all, mu_p, pv, w2_p, a2_p, g2_p, vw1_p, vw2_p, c_len, nch, bt):
    bsz, t_len, _ = x.shape
    frames = nch * c_len
    const2 = lambda b, c: (0, 0)
    step = lambda b, c: (b, c, 0)
    per_seq3 = lambda b, c: (b, 0, 0)
    per_seq4 = lambda b, c: (b, 0, 0, 0)
    in_specs = [
        pl.BlockSpec((bt, frames, D_MODEL), step),
        pl.BlockSpec((bt, 1, GROUP_COLS), per_seq3),
        pl.BlockSpec((None, bt, A_HEADS, A_HEAD_DIM, A_HEAD_DIM), lambda b, c: (layer, b, 0, 0, 0)),
    ]
    args = [x, shift0, s0_all]
    if layer > 0:
        in_specs.append(pl.BlockSpec((bt, frames, DA), step))
        args.append(vfirst)
    in_specs += [
        pl.BlockSpec((1, D_MODEL), const2),
        _resident((None, D_MODEL, GROUP_COLS), lambda b, c: (layer, 0, 0)),
        pl.BlockSpec((1, GROUP_COLS), const2),
        pl.BlockSpec((8, DA), const2),
        pl.BlockSpec((LANES, DA), const2),
        pl.BlockSpec((LANES, DA), const2),
        pl.BlockSpec((LORA_G_PAD, DA), const2),
    ]
    args += [nm, w_rw_all, mu_p, pv, w2_p, a2_p, g2_p]
    if layer > 0:
        in_specs += [pl.BlockSpec((DA, LANES), const2), pl.BlockSpec((LANES, DA), const2)]
        args += [vw1_p, vw2_p]
    out_specs = [
        pl.BlockSpec((bt, frames, DA), step),
        pl.BlockSpec((bt, 1, GROUP_COLS), per_seq3),
        pl.BlockSpec((bt, A_HEADS, A_HEAD_DIM, A_HEAD_DIM), per_seq4),
    ]
    out_shape = [
        jax.ShapeDtypeStruct((bsz, t_len, DA), BF16),
        jax.ShapeDtypeStruct((bsz, 1, GROUP_COLS), F32),
        jax.ShapeDtypeStruct((bsz, A_HEADS, A_HEAD_DIM, A_HEAD_DIM), F32),
    ]
    if layer == 0:
        out_specs.append(pl.BlockSpec((bt, frames, DA), step))
        out_shape.append(jax.ShapeDtypeStruct((bsz, t_len, DA), F32))
    return pl.pallas_call(
        functools.partial(_rwkv_kernel, layer, bt, nch, c_len),
        grid=(bsz // bt, t_len // frames),
        in_specs=in_specs,
        out_specs=out_specs,
        out_shape=out_shape,
        scratch_shapes=[pltpu.VMEM((bt, frames + 8, GROUP_COLS), F32),
                        pltpu.VMEM((bt, N_PAIRS, PAIR, PAIR), F32)],
        compiler_params=pltpu.CompilerParams(
            dimension_semantics=("arbitrary", "arbitrary"), vmem_limit_bytes=VMEM_LIMIT),
        name=f"rwkv{layer}",
    )(*args)


def _hgrn_kernel(layer, bt, tb, x_ref, nm_ref, w_ref, lg_ref, nw_ref, s0_ref, on_ref, s_out_ref, st_ref):
    t = pl.program_id(1)
    n_blk = tb // HGRN_BLOCK

    hn = [_rms(x_ref[b], nm_ref[...]).astype(BF16) for b in range(bt)]

    def project(c0):
        return [jnp.dot(hn[b], w_ref[:, c0:c0 + BK], preferred_element_type=F32) for b in range(bt)]

    fz_all = project(BK)
    q_all = project(0)

    @pl.when(t == 0)
    def _():
        for b in range(bt):
            for h in range(B_HEADS):
                st_ref[b, h] = s0_ref[b, h].T

    lg = lg_ref[...]
    lrows = [lg[i:i + 1, :] for i in range(DEPTH)]
    mx = functools.reduce(jnp.maximum, lrows)
    ex = [jnp.exp(x - mx) for x in lrows]
    den = functools.reduce(lambda a, b: a + b, ex)
    sm = [e / den for e in ex]
    lb = functools.reduce(lambda a, b: a + b, sm[:layer + 1]) - sm[0]

    row = lax.broadcasted_iota(jnp.int32, (tb, tb), 0)
    col = lax.broadcasted_iota(jnp.int32, (tb, tb), 1)
    blk_shift = _log2(HGRN_BLOCK)
    causal = jnp.logical_and(lax.shift_right_logical(row, blk_shift) == lax.shift_right_logical(col, blk_shift),
                             row >= col)
    causal01 = jnp.where(causal, 1.0, 0.0).astype(BF16)

    f_all = [lb + (1.0 - lb) * _sigmoid(fz_all[b]) for b in range(bt)]
    g_cum_all = [_dot_01(causal01, jnp.log(f_all[b])) for b in range(bt)]
    iv = project(2 * BK)
    og = project(2 * BK + DB)

    qg, kg, k_end, decay = [], [], [], []
    for b in range(bt):
        q = q_all[b]
        kx = 1.0 - f_all[b]
        g_cum = g_cum_all[b]
        g_end = jnp.concatenate(
            [jnp.broadcast_to(g_cum[(j + 1) * HGRN_BLOCK - 1:(j + 1) * HGRN_BLOCK, :], (HGRN_BLOCK, BK))
             for j in range(n_blk)], axis=0)
        qg.append((q * _sigmoid(q) * jnp.exp(g_cum)).astype(BF16))
        kg.append((kx * jnp.exp(-g_cum)).astype(BF16))
        k_end.append((kx * jnp.exp(g_end - g_cum)).astype(BF16))
        decay.append(jnp.exp(g_end))

    units = [(b, h) for b in range(bt) for h in range(B_HEADS)]

    def cut(x, u):
        return x[u[0]][:, u[1] * B_KEY_DIM:(u[1] + 1) * B_KEY_DIM]

    def blk(j):
        return slice(j * HGRN_BLOCK, (j + 1) * HGRN_BLOCK)

    att = {u: jnp.where(causal, _dot_nt(cut(qg, u), cut(kg, u)), 0.0) for u in units}
    o_acc = {u: _dot(att[u], cut(iv, u)) for u in units}
    upd = {(u, j): _dot_tn(cut(iv, u)[blk(j)], cut(k_end, u)[blk(j)]) for j in range(n_blk) for u in units}
    starts = {}
    for u in units:
        s = st_ref[u[0], u[1]]
        dec = cut(decay, u)
        for j in range(n_blk):
            starts[(u, j)] = s.astype(BF16)
            s = s * dec[j * HGRN_BLOCK:j * HGRN_BLOCK + 1, :] + upd[(u, j)]
        st_ref[u[0], u[1]] = s
    inter = {(u, j): _dot_nt(cut(qg, u)[blk(j)], starts[(u, j)]) for j in range(n_blk) for u in units}

    for b in range(bt):
        outs = []
        for h in range(B_HEADS):
            u = (b, h)
            o_h = o_acc[u] + jnp.concatenate([inter[(u, j)] for j in range(n_blk)], axis=0)
            outs.append(o_h * lax.rsqrt(jnp.mean(o_h * o_h, axis=-1, keepdims=True) + RMS_EPS))
        o = jnp.concatenate(outs, axis=1)
        on_ref[b] = (o * nw_ref[...] * (og[b] * _sigmoid(og[b]))).astype(BF16)

    @pl.when(t == pl.num_programs(1) - 1)
    def _():
        for b in range(bt):
            for h in range(B_HEADS):
                s_out_ref[b, h] = st_ref[b, h].T


def _hgrn(layer, x, nm, w_hg_all, lb_logits, norm_w, s0_all, tb, bt):
    bsz, t_len, _ = x.shape
    n_blocks = t_len // tb
    return pl.pallas_call(
        functools.partial(_hgrn_kernel, layer, bt, tb),
        grid=(bsz // bt, n_blocks),
        in_specs=[
            pl.BlockSpec((bt, tb, D_MODEL), lambda b, t: (b, t, 0)),
            pl.BlockSpec((1, D_MODEL), lambda b, t: (0, 0)),
            _resident((None, D_MODEL, HGRN_COLS), lambda b, t: (layer, 0, 0)),
            pl.BlockSpec((DEPTH, BK), lambda b, t: (0, 0)),
            pl.BlockSpec((1, DB), lambda b, t: (0, 0)),
            pl.BlockSpec((None, bt, B_HEADS, B_KEY_DIM, B_VAL_DIM), lambda b, t: (layer, b, 0, 0, 0)),
        ],
        out_specs=[
            pl.BlockSpec((bt, tb, DB), lambda b, t: (b, t, 0)),
            pl.BlockSpec((bt, B_HEADS, B_KEY_DIM, B_VAL_DIM), lambda b, t: (b, 0, 0, 0)),
        ],
        out_shape=[
            jax.ShapeDtypeStruct((bsz, t_len, DB), BF16),
            jax.ShapeDtypeStruct((bsz, B_HEADS, B_KEY_DIM, B_VAL_DIM), F32),
        ],
        scratch_shapes=[pltpu.VMEM((bt, B_HEADS, B_VAL_DIM, B_KEY_DIM), F32)],
        compiler_params=pltpu.CompilerParams(
            dimension_semantics=("arbitrary", "arbitrary"), vmem_limit_bytes=VMEM_LIMIT),
        name=f"hgrn{layer}",
    )(x, nm, w_hg_all, lb_logits, norm_w, s0_all)


FF_CHUNK = 1024


def _mix_ffn_kernel(final, *refs):
    if final:
        (x_ref, yg_ref, on_ref, nm_ref, wg_ref, wa_ref, wb_ref, wo_ref, nf_ref, up_ref, dn_ref, nfin_ref,
         o_ref) = refs
    else:
        (x_ref, yg_ref, on_ref, nm_ref, wg_ref, wa_ref, wb_ref, wo_ref, nf_ref, up_ref, dn_ref, o_ref) = refs
    x = x_ref[...]
    gt = _sigmoid(jnp.dot(_rms(x, nm_ref[...]).astype(BF16), wg_ref[...], preferred_element_type=F32))
    ya = jnp.dot(yg_ref[...], wa_ref[...], preferred_element_type=F32)
    yb = jnp.dot(on_ref[...], wb_ref[...], preferred_element_type=F32)
    merged = gt[:, 0:D_MODEL] * ya + gt[:, D_MODEL:] * yb
    x = x + jnp.dot(merged.astype(BF16), wo_ref[...], preferred_element_type=F32)
    h = _rms(x, nf_ref[...]).astype(BF16)
    acc = x
    for cf in range(D_FF // FF_CHUNK):
        u = jnp.dot(h, up_ref[:, cf * FF_CHUNK:(cf + 1) * FF_CHUNK], preferred_element_type=F32)
        u = jnp.maximum(u, 0.0)
        acc = acc + jnp.dot((u * u).astype(BF16), dn_ref[cf * FF_CHUNK:(cf + 1) * FF_CHUNK, :],
                            preferred_element_type=F32)
    if final:
        acc = _rms(acc, nfin_ref[...])
    o_ref[...] = acc


def _mix_ffn(layer, x, yg, on, nm, nf, nfin, w):
    final = layer == DEPTH - 1
    n = x.shape[0]
    tm = min(n, 512)
    small = lambda i: (0, 0)
    mine = lambda i: (layer, 0, 0)
    in_specs = [
        pl.BlockSpec((tm, D_MODEL), lambda i: (i, 0)),
        pl.BlockSpec((tm, DA), lambda i: (i, 0)),
        pl.BlockSpec((tm, DB), lambda i: (i, 0)),
        _resident((1, D_MODEL), small),
        _resident((None, D_MODEL, GATE_COLS), mine),
        _resident((None, DA, D_MODEL), mine),
        _resident((None, DB, D_MODEL), mine),
        _resident((None, D_MODEL, D_MODEL), mine),
        _resident((1, D_MODEL), small),
        _resident((None, D_MODEL, D_FF), mine),
        _resident((None, D_FF, D_MODEL), mine),
    ]
    args = [x, yg, on, nm, w["wg"], w["wa"], w["wb"], w["wo"], nf, w["up"], w["dn"]]
    if final:
        in_specs.append(_resident((1, D_MODEL), small))
        args.append(nfin)
    return pl.pallas_call(
        functools.partial(_mix_ffn_kernel, final),
        grid=(n // tm,),
        in_specs=in_specs,
        out_specs=pl.BlockSpec((tm, D_MODEL), lambda i: (i, 0)),
        out_shape=jax.ShapeDtypeStruct((n, D_MODEL), F32),
        compiler_params=pltpu.CompilerParams(dimension_semantics=("arbitrary",), vmem_limit_bytes=VMEM_LIMIT),
        name="mix_ffn_final" if final else "mix_ffn",
    )(*args)


def _pad_rows(w, rows, at=0):
    out = jnp.zeros((rows, w.shape[1]), w.dtype)
    return out.at[at:at + w.shape[0]].set(w)


def _prep_weights(p):
    w_in = p["w_in"]
    pad = ((0, 0), (0, 0), (0, GROUP_COLS - RWKV_COLS))
    return dict(
        w_rw=jnp.pad(w_in[:, :, :RWKV_COLS], pad).astype(BF16),
        w_hg=w_in[:, :, RWKV_COLS:RWKV_COLS + HGRN_COLS].astype(BF16),
        wg=w_in[:, :, RWKV_COLS + HGRN_COLS:].astype(BF16),
        wa=p["w_out_a"].astype(BF16), wb=p["w_out_b"].astype(BF16), wo=p["w_out"].astype(BF16),
        up=p["w_ffn_up"].astype(BF16), dn=p["w_ffn_down"].astype(BF16),
    )


def _prep_layer(l, p):
    mu_p = jnp.zeros((1, GROUP_COLS), F32).at[0, :RWKV_COLS].set(p["rwkv_mu"][l])
    v0 = p["rwkv_v0"][l - 1] if l > 0 else jnp.zeros((DA,), F32)
    pv = jnp.stack([p["rwkv_w0"][l], p["rwkv_a0"][l], p["rwkv_k_k"][l], p["rwkv_k_a"][l],
                    p["rwkv_r_k"][l].reshape(DA), p["rwkv_ln_w"][l], p["rwkv_ln_b"][l], v0])
    out = dict(
        mu_p=mu_p, pv=pv,
        w2_p=_pad_rows(p["rwkv_w2"][l], LANES, 0).astype(BF16),
        a2_p=_pad_rows(p["rwkv_a2"][l], LANES, W_LORA).astype(BF16),
        g2_p=_pad_rows(p["rwkv_g2"][l], LORA_G_PAD, 0).astype(BF16),
        vw1_p=None, vw2_p=None,
        norm_mix=p["norm_mix"][l].reshape(1, D_MODEL),
        hgrn_norm_w=p["hgrn_norm_w"][l].reshape(1, DB),
        nf=p["norm_ffn"][l].reshape(1, D_MODEL),
    )
    if l > 0:
        w1 = p["rwkv_vres_w1"][l - 1]
        out["vw1_p"] = jnp.zeros((DA, LANES), F32).at[:, :V_LORA].set(w1).astype(BF16)
        out["vw2_p"] = _pad_rows(p["rwkv_vres_w2"][l - 1], LANES, 0).astype(BF16)
    return out


def _trunk(x, state_shift, state_rwkv, state_hgrn, layers, weights, p, c_len, nch, tb, bt):
    bsz, t_len, _ = x.shape
    nfin = p["norm_final"].reshape(1, D_MODEL)
    vfirst = None
    shifts, rwkv_states, hgrn_states = [], [], []
    for l in range(DEPTH):
        lp = layers[l]
        shift0 = jnp.zeros((bsz, 1, GROUP_COLS), F32).at[:, 0, :RWKV_COLS].set(state_shift[l])
        res = _rwkv(l, x, shift0, state_rwkv, vfirst, lp["norm_mix"], weights["w_rw"], lp["mu_p"], lp["pv"],
                    lp["w2_p"], lp["a2_p"], lp["g2_p"], lp["vw1_p"], lp["vw2_p"], c_len, nch, bt)
        if l == 0:
            yg, shift_out, s_rwkv, vfirst = res
        else:
            yg, shift_out, s_rwkv = res
        on, s_hgrn = _hgrn(l, x, lp["norm_mix"], weights["w_hg"], p["hgrn_lb_logits"], lp["hgrn_norm_w"],
                           state_hgrn, tb, bt)
        n = bsz * t_len
        x = _mix_ffn(l, x.reshape(n, D_MODEL), yg.reshape(n, DA), on.reshape(n, DB), lp["norm_mix"], lp["nf"],
                     nfin, weights).reshape(bsz, t_len, D_MODEL)
        shifts.append(shift_out[:, 0, :RWKV_COLS])
        rwkv_states.append(s_rwkv)
        hgrn_states.append(s_hgrn)
    return x, jnp.stack(shifts), jnp.stack(rwkv_states), jnp.stack(hgrn_states)


RWKV_CHUNK = 64
RWKV_CHUNKS = 2
HGRN_TILE = 256
SEQ_TILE = 2


def _run_trunk(x, state_shift, state_rwkv, state_hgrn, layers, weights, p):
    t_len = x.shape[1]
    c_len = min(t_len, RWKV_CHUNK)
    nch = min(t_len // c_len, RWKV_CHUNKS)
    return _trunk(x, state_shift, state_rwkv, state_hgrn, layers, weights, p, c_len, nch, min(t_len, HGRN_TILE),
                  SEQ_TILE)


def kernel(x_prompt, x_sample, state_shift, state_rwkv, state_hgrn, norm_mix, w_in, rwkv_mu, rwkv_w0, rwkv_w2, rwkv_a0, rwkv_a2, rwkv_g2, rwkv_v0, rwkv_vres_w1, rwkv_vres_w2, rwkv_k_k, rwkv_k_a, rwkv_r_k, rwkv_ln_w, rwkv_ln_b, hgrn_lb_logits, hgrn_norm_w, w_out_a, w_out_b, w_out, norm_ffn, w_ffn_up, w_ffn_down, norm_final):
    p = dict(norm_mix=norm_mix, w_in=w_in, rwkv_mu=rwkv_mu, rwkv_w0=rwkv_w0, rwkv_w2=rwkv_w2, rwkv_a0=rwkv_a0,
             rwkv_a2=rwkv_a2, rwkv_g2=rwkv_g2, rwkv_v0=rwkv_v0, rwkv_vres_w1=rwkv_vres_w1,
             rwkv_vres_w2=rwkv_vres_w2, rwkv_k_k=rwkv_k_k, rwkv_k_a=rwkv_k_a, rwkv_r_k=rwkv_r_k,
             rwkv_ln_w=rwkv_ln_w, rwkv_ln_b=rwkv_ln_b, hgrn_lb_logits=hgrn_lb_logits, hgrn_norm_w=hgrn_norm_w,
             w_out_a=w_out_a, w_out_b=w_out_b, w_out=w_out, norm_ffn=norm_ffn, w_ffn_up=w_ffn_up,
             w_ffn_down=w_ffn_down, norm_final=norm_final)
    layers = [_prep_layer(l, p) for l in range(DEPTH)]
    weights = _prep_weights(p)
    bp = x_prompt.shape[0]
    dt = x_prompt.dtype
    zero_shift = jnp.zeros((DEPTH, bp, RWKV_COLS), dt)
    zero_rwkv = jnp.zeros((DEPTH, bp, A_HEADS, A_HEAD_DIM, A_HEAD_DIM), dt)
    zero_hgrn = jnp.zeros((DEPTH, bp, B_HEADS, B_KEY_DIM, B_VAL_DIM), dt)
    y_prompt, shift_p, rwkv_p, hgrn_p = _run_trunk(x_prompt, zero_shift, zero_rwkv, zero_hgrn, layers, weights, p)
    y_sample, shift_s, rwkv_s, hgrn_s = _run_trunk(x_sample, state_shift, state_rwkv, state_hgrn, layers, weights,
                                                   p)
    return (y_prompt, y_sample, shift_p, rwkv_p, hgrn_p, shift_s, rwkv_s, hgrn_s)
```

```python
import functools
import math

import jax
import jax.numpy as jnp
from jax import lax
from jax.experimental import pallas as pl
from jax.experimental.pallas import tpu as pltpu

F32 = jnp.float32
BF16 = jnp.bfloat16

D_MODEL = 1024
DEPTH = 2
A_HEADS = 8
A_HEAD_DIM = 64
DA = A_HEADS * A_HEAD_DIM
W_LORA = 64
A_LORA = 64
V_LORA = 32
G_LORA = 160
RWKV_COLS = 3 * DA + W_LORA + A_LORA + G_LORA
B_HEADS = 4
B_KEY_DIM = 128
B_VAL_DIM = 128
BK = B_HEADS * B_KEY_DIM
DB = B_HEADS * B_VAL_DIM
HGRN_COLS = 2 * BK + 2 * DB
GATE_COLS = 2 * D_MODEL
D_FF = 4 * D_MODEL
HGRN_BLOCK = 16
RMS_EPS = 1e-6
GN_EPS = 64e-5

LANES = 128
GROUP_COLS = 2048
LORA_WA = 3 * DA
LORA_G = LORA_WA + W_LORA + A_LORA
LORA_G_PAD = 256
PAIR = 2 * A_HEAD_DIM
N_PAIRS = A_HEADS // 2
VMEM_LIMIT = 56 * 1024 * 1024


def _dot(a, b):
    return jnp.dot(a.astype(BF16), b.astype(BF16), preferred_element_type=F32)


def _dot_nt(a, b):
    return lax.dot_general(a.astype(BF16), b.astype(BF16), (((1,), (1,)), ((), ())), preferred_element_type=F32)


def _dot_tn(a, b):
    return lax.dot_general(a.astype(BF16), b.astype(BF16), (((0,), (0,)), ((), ())), preferred_element_type=F32)


def _split3(x):
    hi = x.astype(BF16)
    r1 = x - hi.astype(F32)
    mid = r1.astype(BF16)
    return hi, mid, (r1 - mid.astype(F32)).astype(BF16)


def _dot_01(m01, x):
    return functools.reduce(lambda a, b: a + b, [jnp.dot(m01, t, preferred_element_type=F32) for t in _split3(x)])


def _dot_x01(x, m01):
    return functools.reduce(lambda a, b: a + b, [jnp.dot(t, m01, preferred_element_type=F32) for t in _split3(x)])


def _rms(x, w):
    return x * lax.rsqrt(jnp.mean(x * x, axis=-1, keepdims=True) + RMS_EPS) * w


def _sigmoid(x):
    return 1.0 / (1.0 + jnp.exp(-x))


def _log2(n):
    assert n & (n - 1) == 0, n
    return n.bit_length() - 1


def _resident(shape, index_map):
    return pl.BlockSpec(shape, index_map, pipeline_mode=pl.Buffered(1))


_DONE = object()


def _round_robin(*gens):
    live = list(gens)
    while live:
        for gen in list(live):
            if next(gen, _DONE) is _DONE:
                live.remove(gen)
            else:
                yield


def _rwkv_part(layer, bt, nch, c_len, c, h, refs):
    if layer == 0:
        (shift_ref, s0_ref, w_ref, mu_ref, pv_ref, w2_ref, a2_ref, g2_ref,
         yg_ref, shift_out_ref, s_out_ref, vfirst_out_ref, xs_ref, s_ref) = refs
    else:
        (shift_ref, s0_ref, vfirst_ref, w_ref, mu_ref, pv_ref, w2_ref, a2_ref, g2_ref, vw1_ref, vw2_ref,
         yg_ref, shift_out_ref, s_out_ref, xs_ref, s_ref) = refs
    frames = nch * c_len
    rows = bt * frames
    stk = 2 * c_len

    hrow = lax.broadcasted_iota(jnp.int32, (A_HEAD_DIM, PAIR), 0)
    hlane = lax.broadcasted_iota(jnp.int32, (A_HEAD_DIM, PAIR), 1)
    to_lo = jnp.where(hlane == hrow, 1.0, 0.0).astype(BF16)
    to_hi = jnp.where(hlane == hrow + A_HEAD_DIM, 1.0, 0.0).astype(BF16)

    @pl.when(c == 0)
    def _():
        for b in range(bt):
            xs_ref[b, 7:8, :] = shift_ref[b]
            for p in range(N_PAIRS):
                s_ref[b, p] = jnp.concatenate(
                    [_dot_x01(s0_ref[b, 2 * p], to_lo), _dot_x01(s0_ref[b, 2 * p + 1], to_hi)], axis=0)

    def project(c0, c1):
        rw = jnp.dot(h, w_ref[:, c0:c1], preferred_element_type=F32)
        prevs = []
        for b in range(bt):
            rw_b = rw[b * frames:(b + 1) * frames]
            xs_ref[b, 8:8 + frames, c0:c1] = rw_b
            prevs.append(xs_ref[b, 7:7 + frames, c0:c1])
            last = rw_b[frames - 1:frames, :]
            xs_ref[b, 7:8, c0:c1] = last
            shift_out_ref[b, :, c0:c1] = last
        return rw + (jnp.concatenate(prevs, axis=0) - rw) * mu_ref[:, c0:c1]

    w0 = pv_ref[0:1, :]
    a0 = pv_ref[1:2, :]
    k_k = pv_ref[2:3, :]
    k_a = pv_ref[3:4, :]
    r_k = pv_ref[4:5, :]
    ln_w = pv_ref[5:6, :]
    ln_b = pv_ref[6:7, :]

    row = lax.broadcasted_iota(jnp.int32, (rows, rows), 0)
    col = lax.broadcasted_iota(jnp.int32, (rows, rows), 1)
    chunk_shift = _log2(c_len)
    causal = jnp.logical_and(
        lax.shift_right_logical(row, chunk_shift) == lax.shift_right_logical(col, chunk_shift), row >= col)

    lora = project(LORA_WA, GROUP_COLS)
    yield
    k = project(DA, 2 * DA)
    yield
    wa = lora[:, 0:LORA_G - LORA_WA]
    gl = lora[:, LORA_G - LORA_WA:LORA_G - LORA_WA + LORA_G_PAD]
    w_raw = w0 + _dot(jnp.tanh(wa), w2_ref[...])
    a_gate = _sigmoid(a0 + _dot(wa, a2_ref[...]))
    g = _dot(_sigmoid(gl), g2_ref[...])
    yield
    r = project(0, DA)
    yield
    lw = (-math.exp(-0.5)) * _sigmoid(w_raw)
    g_cum = _dot_01(jnp.where(causal, 1.0, 0.0).astype(BF16), lw)
    yield
    v = project(2 * DA, 3 * DA)
    yield
    if layer == 0:
        for b in range(bt):
            vfirst_out_ref[b] = v[b * frames:(b + 1) * frames]
    else:
        v0 = pv_ref[7:8, :]
        vg = _sigmoid(v0 + _dot(_dot(v, vw1_ref[...]), vw2_ref[...]))
        vfirst = jnp.concatenate([vfirst_ref[b] for b in range(bt)], axis=0)
        v = v + (vfirst - v) * vg

    lane = lax.broadcasted_iota(jnp.int32, (PAIR, PAIR), 1)
    sub = lax.broadcasted_iota(jnp.int32, (PAIR, PAIR), 0)
    head_ones = jnp.where((lane < A_HEAD_DIM) == (sub < A_HEAD_DIM), 1.0, 0.0).astype(BF16)

    def seg_sum(x):
        return jnp.concatenate(
            [jnp.dot(x[:, p * PAIR:(p + 1) * PAIR].astype(BF16), head_ones, preferred_element_type=F32)
             for p in range(N_PAIRS)], axis=1)

    kk = k * k_k
    kk = kk * lax.rsqrt(jnp.maximum(seg_sum(kk * kk), 1e-24))
    yield
    k_h = k * (1.0 + (a_gate - 1.0) * k_a)
    a_vec = -kk
    b_vec = kk * a_gate

    g_end = jnp.concatenate(
        [jnp.broadcast_to(g_cum[(q + 1) * c_len - 1:(q + 1) * c_len, :], (c_len, DA)) for q in range(bt * nch)],
        axis=0)
    e_pos = jnp.exp(g_cum)
    e_neg = jnp.exp(-g_cum)
    e_end = jnp.exp(g_end - g_cum)
    rq = r * e_pos
    aq = a_vec * jnp.exp(g_cum - lw)
    kn = k_h * e_neg
    bn = b_vec * e_neg
    k_end = k_h * e_end
    b_end = b_vec * e_end
    decay_end = jnp.exp(g_end)

    srow = lax.broadcasted_iota(jnp.int32, (stk, PAIR), 0)
    slane = lax.broadcasted_iota(jnp.int32, (stk, PAIR), 1)
    own_lanes = (srow < c_len) == (slane < A_HEAD_DIM)

    def stack(x):
        return jnp.where(own_lanes, jnp.concatenate([x, x], axis=0), 0.0).astype(BF16)

    mrow = lax.broadcasted_iota(jnp.int32, (stk, stk), 0)
    mcol = lax.broadcasted_iota(jnp.int32, (stk, stk), 1)
    strict = (mrow & (c_len - 1)) > (mcol & (c_len - 1))
    lower = (mrow & (c_len - 1)) >= (mcol & (c_len - 1))
    eye = jnp.where(mrow == mcol, 1.0, 0.0).astype(F32)

    def cut(x, u):
        b, ch, p = u
        r0 = (b * nch + ch) * c_len
        return x[r0:r0 + c_len, p * PAIR:(p + 1) * PAIR]

    lhs, v_st, a_ak_v, a_rk, a_rb, t_inv, y_parts = {}, {}, {}, {}, {}, {}, {}

    def scores_and_inverse(units):
        power = {}
        for u in units:
            lhs[u] = jnp.concatenate([stack(cut(aq, u)), stack(cut(rq, u))], axis=0)
            v_st[u] = stack(cut(v, u))
            sc_b = _dot_nt(lhs[u], stack(cut(bn, u)))
            sc_k = _dot_nt(lhs[u], stack(cut(kn, u)))
            power[u] = jnp.where(strict, sc_b[:stk], 0.0)
            a_rb[u] = jnp.where(lower, sc_b[stk:], 0.0)
            a_rk[u] = jnp.where(lower, sc_k[stk:], 0.0)
            a_ak_v[u] = jnp.where(strict, sc_k[:stk], 0.0)
        yield
        for u in units:
            t_inv[u] = eye + power[u]
            a_ak_v[u] = _dot(a_ak_v[u], v_st[u])
            power[u] = _dot(power[u], power[u])
        yield
        for lvl in range(1, _log2(c_len)):
            new_power = {}
            for u in units:
                t_inv[u] = t_inv[u] + _dot(power[u], t_inv[u])
                if lvl < _log2(c_len) - 1:
                    new_power[u] = _dot(power[u], power[u])
            power = new_power
            yield

    def apply_state(units):
        from_state = {u: _dot_nt(lhs[u], s_ref[u[0], u[2]]) for u in units}
        yield
        corr = {u: _dot(t_inv[u], from_state[u][:stk] + a_ak_v[u]) for u in units}
        yield
        for u in units:
            vu = jnp.concatenate([v_st[u], corr[u].astype(BF16)], axis=0)
            if stk % LANES == 0:
                y_st = from_state[u][stk:] + _dot(jnp.concatenate([a_rk[u], a_rb[u]], axis=1), vu)
            else:
                y_st = from_state[u][stk:] + _dot(a_rk[u], v_st[u]) + _dot(a_rb[u], corr[u])
            y_parts[u] = y_st[:c_len] + y_st[c_len:]
            ends = jnp.concatenate([stack(cut(k_end, u)), stack(cut(b_end, u))], axis=0)
            s_ref[u[0], u[2]] = s_ref[u[0], u[2]] * cut(decay_end, u)[0:1, :] + _dot_tn(vu, ends)
        yield

    chunk_units = [[(b, ch, p) for b in range(bt) for p in range(N_PAIRS)] for ch in range(nch)]
    yield from scores_and_inverse([u for units in chunk_units for u in units])
    for ch in range(nch):
        yield from apply_state(chunk_units[ch])

    y = jnp.concatenate(
        [jnp.concatenate([y_parts[(b, ch, p)] for p in range(N_PAIRS)], axis=1)
         for b in range(bt) for ch in range(nch)], axis=0)
    inv_n = 1.0 / A_HEAD_DIM
    mean = seg_sum(y) * inv_n
    bonus = seg_sum(r * k_h * r_k)
    yield
    dev = y - mean
    var = seg_sum(dev * dev) * inv_n
    y = dev * lax.rsqrt(var + GN_EPS) * ln_w + ln_b
    y = ((y + bonus * v) * g).astype(BF16)
    for b in range(bt):
        yg_ref[b] = y[b * frames:(b + 1) * frames]

    @pl.when(c == pl.num_programs(1) - 1)
    def _():
        prow = lax.broadcasted_iota(jnp.int32, (PAIR, A_HEAD_DIM), 0)
        pcol = lax.broadcasted_iota(jnp.int32, (PAIR, A_HEAD_DIM), 1)
        from_lo = jnp.where(prow == pcol, 1.0, 0.0).astype(BF16)
        from_hi = jnp.where(prow == pcol + A_HEAD_DIM, 1.0, 0.0).astype(BF16)
        for b in range(bt):
            for p in range(N_PAIRS):
                s_pair = s_ref[b, p]
                s_out_ref[b, 2 * p] = _dot_x01(s_pair[:A_HEAD_DIM], from_lo)
                s_out_ref[b, 2 * p + 1] = _dot_x01(s_pair[A_HEAD_DIM:], from_hi)


def _hgrn_part(layer, bt, tb, c, h, refs):
    (w_ref, lg_ref, nw_ref, s0_ref, on_ref, s_out_ref, st_ref) = refs
    n_blk = tb // HGRN_BLOCK

    def project(c0):
        return [jnp.dot(h[b * tb:(b + 1) * tb], w_ref[:, c0:c0 + BK], preferred_element_type=F32)
                for b in range(bt)]

    fz_all = project(BK)
    yield
    q_all = project(0)
    yield

    @pl.when(c == 0)
    def _():
        for b in range(bt):
            for hd in range(B_HEADS):
                st_ref[b, hd] = s0_ref[b, hd].T

    lg = lg_ref[...]
    lrows = [lg[i:i + 1, :] for i in range(DEPTH)]
    mx = functools.reduce(jnp.maximum, lrows)
    ex = [jnp.exp(x - mx) for x in lrows]
    den = functools.reduce(lambda a, b: a + b, ex)
    sm = [e / den for e in ex]
    lb = functools.reduce(lambda a, b: a + b, sm[:layer + 1]) - sm[0]

    row = lax.broadcasted_iota(jnp.int32, (tb, tb), 0)
    col = lax.broadcasted_iota(jnp.int32, (tb, tb), 1)
    blk_shift = _log2(HGRN_BLOCK)
    causal = jnp.logical_and(lax.shift_right_logical(row, blk_shift) == lax.shift_right_logical(col, blk_shift),
                             row >= col)
    causal01 = jnp.where(causal, 1.0, 0.0).astype(BF16)

    f_all = [lb + (1.0 - lb) * _sigmoid(fz_all[b]) for b in range(bt)]
    g_cum_all = [_dot_01(causal01, jnp.log(f_all[b])) for b in range(bt)]
    yield
    iv = project(2 * BK)
    yield

    qg, kg, k_end, decay = [], [], [], []
    for b in range(bt):
        q = q_all[b]
        kx = 1.0 - f_all[b]
        g_cum = g_cum_all[b]
        g_end = jnp.concatenate(
            [jnp.broadcast_to(g_cum[(j + 1) * HGRN_BLOCK - 1:(j + 1) * HGRN_BLOCK, :], (HGRN_BLOCK, BK))
             for j in range(n_blk)], axis=0)
        qg.append((q * _sigmoid(q) * jnp.exp(g_cum)).astype(BF16))
        kg.append((kx * jnp.exp(-g_cum)).astype(BF16))
        k_end.append((kx * jnp.exp(g_end - g_cum)).astype(BF16))
        decay.append(jnp.exp(g_end))

    units = [(b, hd) for b in range(bt) for hd in range(B_HEADS)]

    def cut(x, u):
        return x[u[0]][:, u[1] * B_KEY_DIM:(u[1] + 1) * B_KEY_DIM]

    def blk(j):
        return slice(j * HGRN_BLOCK, (j + 1) * HGRN_BLOCK)

    att = {u: jnp.where(causal, _dot_nt(cut(qg, u), cut(kg, u)), 0.0) for u in units}
    yield
    upd = {}
    for j in range(n_blk):
        for u in units:
            upd[(u, j)] = _dot_tn(cut(iv, u)[blk(j)], cut(k_end, u)[blk(j)])
        yield
    starts = {}
    for u in units:
        s = st_ref[u[0], u[1]]
        dec = cut(decay, u)
        for j in range(n_blk):
            starts[(u, j)] = s.astype(BF16)
            s = s * dec[j * HGRN_BLOCK:j * HGRN_BLOCK + 1, :] + upd[(u, j)]
        st_ref[u[0], u[1]] = s
    og = project(2 * BK + DB)
    yield
    o_acc = {u: _dot(att[u], cut(iv, u)) for u in units}
    yield
    inter = {}
    for j in range(n_blk):
        for u in units:
            inter[(u, j)] = _dot_nt(cut(qg, u)[blk(j)], starts[(u, j)])
        yield

    for b in range(bt):
        outs = []
        for hd in range(B_HEADS):
            u = (b, hd)
            o_h = o_acc[u] + jnp.concatenate([inter[(u, j)] for j in range(n_blk)], axis=0)
            outs.append(o_h * lax.rsqrt(jnp.mean(o_h * o_h, axis=-1, keepdims=True) + RMS_EPS))
        o = jnp.concatenate(outs, axis=1)
        on_ref[b] = (o * nw_ref[...] * (og[b] * _sigmoid(og[b]))).astype(BF16)

    @pl.when(c == pl.num_programs(1) - 1)
    def _():
        for b in range(bt):
            for hd in range(B_HEADS):
                s_out_ref[b, hd] = st_ref[b, hd].T


N_RWKV_IN = {0: 8, 1: 11}
N_HGRN_IN = 4


def _mixer_kernel(layer, bt, nch, c_len, *refs):
    n_ri = N_RWKV_IN[min(layer, 1)]
    n_ro = 4 if layer == 0 else 3
    x_ref, nm_ref = refs[0], refs[1]
    rwkv_in = refs[2:2 + n_ri]
    hgrn_in = refs[2 + n_ri:2 + n_ri + N_HGRN_IN]
    outs = refs[2 + n_ri + N_HGRN_IN:]
    rwkv_out, hgrn_out, scratch = outs[:n_ro], outs[n_ro:n_ro + 2], outs[n_ro + 2:]
    xs_ref, s_ref, st_ref = scratch
    c = pl.program_id(1)
    frames = nch * c_len
    h = jnp.concatenate([_rms(x_ref[b], nm_ref[...]).astype(BF16) for b in range(bt)], axis=0)
    rwkv = _rwkv_part(layer, bt, nch, c_len, c, h, (*rwkv_in, *rwkv_out, xs_ref, s_ref))
    hgrn = _hgrn_part(layer, bt, frames, c, h, (*hgrn_in, *hgrn_out, st_ref))
    for _ in _round_robin(rwkv, hgrn):
        pass


def _mixers(layer, x, shift0, s_rwkv_all, vfirst, nm, w_rw_all, mu_p, pv, w2_p, a2_p, g2_p, vw1_p, vw2_p,
            w_hg_all, lb_logits, hgrn_norm_w, s_hgrn_all, c_len, nch, bt):
    bsz, t_len, _ = x.shape
    frames = nch * c_len
    const2 = lambda b, c: (0, 0)
    step = lambda b, c: (b, c, 0)
    per_seq3 = lambda b, c: (b, 0, 0)
    per_seq4 = lambda b, c: (b, 0, 0, 0)
    of_layer3 = lambda b, c: (layer, 0, 0)
    of_layer5 = lambda b, c: (layer, b, 0, 0, 0)
    in_specs = [
        pl.BlockSpec((bt, frames, D_MODEL), step),
        pl.BlockSpec((1, D_MODEL), const2),
        pl.BlockSpec((bt, 1, GROUP_COLS), per_seq3),
        pl.BlockSpec((None, bt, A_HEADS, A_HEAD_DIM, A_HEAD_DIM), of_layer5),
    ]
    args = [x, nm, shift0, s_rwkv_all]
    if layer > 0:
        in_specs.append(pl.BlockSpec((bt, frames, DA), step))
        args.append(vfirst)
    in_specs += [
        _resident((None, D_MODEL, GROUP_COLS), of_layer3),
        pl.BlockSpec((1, GROUP_COLS), const2),
        pl.BlockSpec((8, DA), const2),
        pl.BlockSpec((LANES, DA), const2),
        pl.BlockSpec((LANES, DA), const2),
        pl.BlockSpec((LORA_G_PAD, DA), const2),
    ]
    args += [w_rw_all, mu_p, pv, w2_p, a2_p, g2_p]
    if layer > 0:
        in_specs += [pl.BlockSpec((DA, LANES), const2), pl.BlockSpec((LANES, DA), const2)]
        args += [vw1_p, vw2_p]
    in_specs += [
        _resident((None, D_MODEL, HGRN_COLS), of_layer3),
        pl.BlockSpec((DEPTH, BK), const2),
        pl.BlockSpec((1, DB), const2),
        pl.BlockSpec((None, bt, B_HEADS, B_KEY_DIM, B_VAL_DIM), of_layer5),
    ]
    args += [w_hg_all, lb_logits, hgrn_norm_w, s_hgrn_all]
    assert len(args) == 2 + N_RWKV_IN[min(layer, 1)] + N_HGRN_IN

    out_specs = [
        pl.BlockSpec((bt, frames, DA), step),
        pl.BlockSpec((bt, 1, GROUP_COLS), per_seq3),
        pl.BlockSpec((bt, A_HEADS, A_HEAD_DIM, A_HEAD_DIM), per_seq4),
    ]
    out_shape = [
        jax.ShapeDtypeStruct((bsz, t_len, DA), BF16),
        jax.ShapeDtypeStruct((bsz, 1, GROUP_COLS), F32),
        jax.ShapeDtypeStruct((bsz, A_HEADS, A_HEAD_DIM, A_HEAD_DIM), F32),
    ]
    if layer == 0:
        out_specs.append(pl.BlockSpec((bt, frames, DA), step))
        out_shape.append(jax.ShapeDtypeStruct((bsz, t_len, DA), F32))
    out_specs += [
        pl.BlockSpec((bt, frames, DB), step),
        pl.BlockSpec((bt, B_HEADS, B_KEY_DIM, B_VAL_DIM), per_seq4),
    ]
    out_shape += [
        jax.ShapeDtypeStruct((bsz, t_len, DB), BF16),
        jax.ShapeDtypeStruct((bsz, B_HEADS, B_KEY_DIM, B_VAL_DIM), F32),
    ]
    return pl.pallas_call(
        functools.partial(_mixer_kernel, layer, bt, nch, c_len),
        grid=(bsz // bt, t_len // frames),
        in_specs=in_specs,
        out_specs=out_specs,
        out_shape=out_shape,
        scratch_shapes=[pltpu.VMEM((bt, frames + 8, GROUP_COLS), F32),
                        pltpu.VMEM((bt, N_PAIRS, PAIR, PAIR), F32),
                        pltpu.VMEM((bt, B_HEADS, B_VAL_DIM, B_KEY_DIM), F32)],
        compiler_params=pltpu.CompilerParams(
            dimension_semantics=("arbitrary", "arbitrary"), vmem_limit_bytes=VMEM_LIMIT),
        name=f"mixers{layer}",
    )(*args)


FF_CHUNK = 1024


def _mix_ffn_kernel(final, *refs):
    if final:
        (x_ref, yg_ref, on_ref, nm_ref, wg_ref, wa_ref, wb_ref, wo_ref, nf_ref, up_ref, dn_ref, nfin_ref,
         o_ref) = refs
    else:
        (x_ref, yg_ref, on_ref, nm_ref, wg_ref, wa_ref, wb_ref, wo_ref, nf_ref, up_ref, dn_ref, o_ref) = refs
    x = x_ref[...]
    gt = _sigmoid(jnp.dot(_rms(x, nm_ref[...]).astype(BF16), wg_ref[...], preferred_element_type=F32))
    ya = jnp.dot(yg_ref[...], wa_ref[...], preferred_element_type=F32)
    yb = jnp.dot(on_ref[...], wb_ref[...], preferred_element_type=F32)
    merged = gt[:, 0:D_MODEL] * ya + gt[:, D_MODEL:] * yb
    x = x + jnp.dot(merged.astype(BF16), wo_ref[...], preferred_element_type=F32)
    h = _rms(x, nf_ref[...]).astype(BF16)
    acc = x
    for cf in range(D_FF // FF_CHUNK):
        u = jnp.dot(h, up_ref[:, cf * FF_CHUNK:(cf + 1) * FF_CHUNK], preferred_element_type=F32)
        u = jnp.maximum(u, 0.0)
        acc = acc + jnp.dot((u * u).astype(BF16), dn_ref[cf * FF_CHUNK:(cf + 1) * FF_CHUNK, :],
                            preferred_element_type=F32)
    if final:
        acc = _rms(acc, nfin_ref[...])
    o_ref[...] = acc


def _mix_ffn(layer, x, yg, on, nm, nf, nfin, w):
    final = layer == DEPTH - 1
    n = x.shape[0]
    tm = min(n, 512)
    small = lambda i: (0, 0)
    mine = lambda i: (layer, 0, 0)
    in_specs = [
        pl.BlockSpec((tm, D_MODEL), lambda i: (i, 0)),
        pl.BlockSpec((tm, DA), lambda i: (i, 0)),
        pl.BlockSpec((tm, DB), lambda i: (i, 0)),
        _resident((1, D_MODEL), small),
        _resident((None, D_MODEL, GATE_COLS), mine),
        _resident((None, DA, D_MODEL), mine),
        _resident((None, DB, D_MODEL), mine),
        _resident((None, D_MODEL, D_MODEL), mine),
        _resident((1, D_MODEL), small),
        _resident((None, D_MODEL, D_FF), mine),
        _resident((None, D_FF, D_MODEL), mine),
    ]
    args = [x, yg, on, nm, w["wg"], w["wa"], w["wb"], w["wo"], nf, w["up"], w["dn"]]
    if final:
        in_specs.append(_resident((1, D_MODEL), small))
        args.append(nfin)
    return pl.pallas_call(
        functools.partial(_mix_ffn_kernel, final),
        grid=(n // tm,),
        in_specs=in_specs,
        out_specs=pl.BlockSpec((tm, D_MODEL), lambda i: (i, 0)),
        out_shape=jax.ShapeDtypeStruct((n, D_MODEL), F32),
        compiler_params=pltpu.CompilerParams(dimension_semantics=("arbitrary",), vmem_limit_bytes=VMEM_LIMIT),
        name="mix_ffn_final" if final else "mix_ffn",
    )(*args)


def _pad_rows(w, rows, at=0):
    out = jnp.zeros((rows, w.shape[1]), w.dtype)
    return out.at[at:at + w.shape[0]].set(w)


def _prep_weights(p):
    w_in = p["w_in"]
    pad = ((0, 0), (0, 0), (0, GROUP_COLS - RWKV_COLS))
    return dict(
        w_rw=jnp.pad(w_in[:, :, :RWKV_COLS], pad).astype(BF16),
        w_hg=w_in[:, :, RWKV_COLS:RWKV_COLS + HGRN_COLS].astype(BF16),
        wg=w_in[:, :, RWKV_COLS + HGRN_COLS:].astype(BF16),
        wa=p["w_out_a"].astype(BF16), wb=p["w_out_b"].astype(BF16), wo=p["w_out"].astype(BF16),
        up=p["w_ffn_up"].astype(BF16), dn=p["w_ffn_down"].astype(BF16),
    )


def _prep_layer(l, p):
    mu_p = jnp.zeros((1, GROUP_COLS), F32).at[0, :RWKV_COLS].set(p["rwkv_mu"][l])
    v0 = p["rwkv_v0"][l - 1] if l > 0 else jnp.zeros((DA,), F32)
    pv = jnp.stack([p["rwkv_w0"][l], p["rwkv_a0"][l], p["rwkv_k_k"][l], p["rwkv_k_a"][l],
                    p["rwkv_r_k"][l].reshape(DA), p["rwkv_ln_w"][l], p["rwkv_ln_b"][l], v0])
    out = dict(
        mu_p=mu_p, pv=pv,
        w2_p=_pad_rows(p["rwkv_w2"][l], LANES, 0).astype(BF16),
        a2_p=_pad_rows(p["rwkv_a2"][l], LANES, W_LORA).astype(BF16),
        g2_p=_pad_rows(p["rwkv_g2"][l], LORA_G_PAD, 0).astype(BF16),
        vw1_p=None, vw2_p=None,
        norm_mix=p["norm_mix"][l].reshape(1, D_MODEL),
        hgrn_norm_w=p["hgrn_norm_w"][l].reshape(1, DB),
        nf=p["norm_ffn"][l].reshape(1, D_MODEL),
    )
    if l > 0:
        w1 = p["rwkv_vres_w1"][l - 1]
        out["vw1_p"] = jnp.zeros((DA, LANES), F32).at[:, :V_LORA].set(w1).astype(BF16)
        out["vw2_p"] = _pad_rows(p["rwkv_vres_w2"][l - 1], LANES, 0).astype(BF16)
    return out


def _trunk(x, state_shift, state_rwkv, state_hgrn, layers, weights, p, c_len, nch, bt):
    bsz, t_len, _ = x.shape
    nfin = p["norm_final"].reshape(1, D_MODEL)
    vfirst = None
    shifts, rwkv_states, hgrn_states = [], [], []
    for l in range(DEPTH):
        lp = layers[l]
        shift0 = jnp.zeros((bsz, 1, GROUP_COLS), F32).at[:, 0, :RWKV_COLS].set(state_shift[l])
        res = _mixers(l, x, shift0, state_rwkv, vfirst, lp["norm_mix"], weights["w_rw"], lp["mu_p"], lp["pv"],
                      lp["w2_p"], lp["a2_p"], lp["g2_p"], lp["vw1_p"], lp["vw2_p"], weights["w_hg"],
                      p["hgrn_lb_logits"], lp["hgrn_norm_w"], state_hgrn, c_len, nch, bt)
        if l == 0:
            yg, shift_out, s_rwkv, vfirst, on, s_hgrn = res
        else:
            yg, shift_out, s_rwkv, on, s_hgrn = res
        n = bsz * t_len
        x = _mix_ffn(l, x.reshape(n, D_MODEL), yg.reshape(n, DA), on.reshape(n, DB), lp["norm_mix"], lp["nf"],
                     nfin, weights).reshape(bsz, t_len, D_MODEL)
        shifts.append(shift_out[:, 0, :RWKV_COLS])
        rwkv_states.append(s_rwkv)
        hgrn_states.append(s_hgrn)
    return x, jnp.stack(shifts), jnp.stack(rwkv_states), jnp.stack(hgrn_states)


RWKV_CHUNK = 64
RWKV_CHUNKS = 2
STEP_ROWS = 256


def _run_trunk(x, state_shift, state_rwkv, state_hgrn, layers, weights, p):
    bsz, t_len, _ = x.shape
    c_len = min(t_len, RWKV_CHUNK)
    nch = min(t_len // c_len, RWKV_CHUNKS)
    bt = min(bsz, max(1, STEP_ROWS // (nch * c_len)))
    assert bsz % bt == 0 and t_len % (nch * c_len) == 0
    return _trunk(x, state_shift, state_rwkv, state_hgrn, layers, weights, p, c_len, nch, bt)


def kernel(x_prompt, x_sample, state_shift, state_rwkv, state_hgrn, norm_mix, w_in, rwkv_mu, rwkv_w0, rwkv_w2, rwkv_a0, rwkv_a2, rwkv_g2, rwkv_v0, rwkv_vres_w1, rwkv_vres_w2, rwkv_k_k, rwkv_k_a, rwkv_r_k, rwkv_ln_w, rwkv_ln_b, hgrn_lb_logits, hgrn_norm_w, w_out_a, w_out_b, w_out, norm_ffn, w_ffn_up, w_ffn_down, norm_final):
    p = dict(norm_mix=norm_mix, w_in=w_in, rwkv_mu=rwkv_mu, rwkv_w0=rwkv_w0, rwkv_w2=rwkv_w2, rwkv_a0=rwkv_a0,
             rwkv_a2=rwkv_a2, rwkv_g2=rwkv_g2, rwkv_v0=rwkv_v0, rwkv_vres_w1=rwkv_vres_w1,
             rwkv_vres_w2=rwkv_vres_w2, rwkv_k_k=rwkv_k_k, rwkv_k_a=rwkv_k_a, rwkv_r_k=rwkv_r_k,
             rwkv_ln_w=rwkv_ln_w, rwkv_ln_b=rwkv_ln_b, hgrn_lb_logits=hgrn_lb_logits, hgrn_norm_w=hgrn_norm_w,
             w_out_a=w_out_a, w_out_b=w_out_b, w_out=w_out, norm_ffn=norm_ffn, w_ffn_up=w_ffn_up,
             w_ffn_down=w_ffn_down, norm_final=norm_final)
    layers = [_prep_layer(l, p) for l in range(DEPTH)]
    weights = _prep_weights(p)
    bp = x_prompt.shape[0]
    dt = x_prompt.dtype
    zero_shift = jnp.zeros((DEPTH, bp, RWKV_COLS), dt)
    zero_rwkv = jnp.zeros((DEPTH, bp, A_HEADS, A_HEAD_DIM, A_HEAD_DIM), dt)
    zero_hgrn = jnp.zeros((DEPTH, bp, B_HEADS, B_KEY_DIM, B_VAL_DIM), dt)
    y_prompt, shift_p, rwkv_p, hgrn_p = _run_trunk(x_prompt, zero_shift, zero_rwkv, zero_hgrn, layers, weights, p)
    y_sample, shift_s, rwkv_s, hgrn_s = _run_trunk(x_sample, state_shift, state_rwkv, state_hgrn, layers, weights,
                                                   p)
    return (y_prompt, y_sample, shift_p, rwkv_p, hgrn_p, shift_s, rwkv_s, hgrn_s)
```

```python
import functools
import math

import jax
import jax.numpy as jnp
from jax import lax
from jax.experimental import pallas as pl
from jax.experimental.pallas import tpu as pltpu

F32 = jnp.float32
BF16 = jnp.bfloat16

D_MODEL = 1024
DEPTH = 2
A_HEADS = 8
A_HEAD_DIM = 64
DA = A_HEADS * A_HEAD_DIM
W_LORA = 64
A_LORA = 64
V_LORA = 32
G_LORA = 160
RWKV_COLS = 3 * DA + W_LORA + A_LORA + G_LORA
B_HEADS = 4
B_KEY_DIM = 128
B_VAL_DIM = 128
BK = B_HEADS * B_KEY_DIM
DB = B_HEADS * B_VAL_DIM
HGRN_COLS = 2 * BK + 2 * DB
GATE_COLS = 2 * D_MODEL
D_FF = 4 * D_MODEL
HGRN_BLOCK = 16
RMS_EPS = 1e-6
GN_EPS = 64e-5

LANES = 128
GROUP_COLS = 2048
LORA_WA = 3 * DA
LORA_G = LORA_WA + W_LORA + A_LORA
LORA_G_PAD = 256
PAIR = 2 * A_HEAD_DIM
N_PAIRS = A_HEADS // 2
PAIR_LEN = 2 * HGRN_BLOCK
VMEM_LIMIT = 56 * 1024 * 1024

RWKV_CHUNK = 64
RWKV_CHUNKS = 4
RWKV_STEP_ROWS = 512
HGRN_TILE = 256
HGRN_STEP_ROWS = 512


def _dot(a, b):
    return jnp.dot(a.astype(BF16), b.astype(BF16), preferred_element_type=F32)


def _dot_nt(a, b):
    return lax.dot_general(a.astype(BF16), b.astype(BF16), (((1,), (1,)), ((), ())), preferred_element_type=F32)


def _dot_tn(a, b):
    return lax.dot_general(a.astype(BF16), b.astype(BF16), (((0,), (0,)), ((), ())), preferred_element_type=F32)


def _split3(x):
    hi = x.astype(BF16)
    r1 = x - hi.astype(F32)
    mid = r1.astype(BF16)
    return hi, mid, (r1 - mid.astype(F32)).astype(BF16)


def _dot_01(m01, x):
    return functools.reduce(lambda a, b: a + b, [jnp.dot(m01, t, preferred_element_type=F32) for t in _split3(x)])


def _dot_x01(x, m01):
    return functools.reduce(lambda a, b: a + b, [jnp.dot(t, m01, preferred_element_type=F32) for t in _split3(x)])


def _rms(x, w):
    return x * lax.rsqrt(jnp.mean(x * x, axis=-1, keepdims=True) + RMS_EPS) * w


def _sigmoid(x):
    return 1.0 / (1.0 + jnp.exp(-x))


def _log2(n):
    assert n & (n - 1) == 0, n
    return n.bit_length() - 1


def _resident(shape, index_map):
    return pl.BlockSpec(shape, index_map, pipeline_mode=pl.Buffered(1))


_DONE = object()


def _round_robin(*gens):
    live = list(gens)
    while live:
        for gen in list(live):
            if next(gen, _DONE) is _DONE:
                live.remove(gen)
            else:
                yield


def _rwkv_part(layer, bt, nch, c_len, c, h, refs):
    if layer == 0:
        (shift_ref, s0_ref, w_ref, mu_ref, pv_ref, w2_ref, a2_ref, g2_ref,
         yg_ref, shift_out_ref, s_out_ref, vfirst_out_ref, xs_ref, s_ref) = refs
    else:
        (shift_ref, s0_ref, vfirst_ref, w_ref, mu_ref, pv_ref, w2_ref, a2_ref, g2_ref, vw1_ref, vw2_ref,
         yg_ref, shift_out_ref, s_out_ref, xs_ref, s_ref) = refs
    frames = nch * c_len
    rows = bt * frames
    stk = 2 * c_len

    hrow = lax.broadcasted_iota(jnp.int32, (A_HEAD_DIM, PAIR), 0)
    hlane = lax.broadcasted_iota(jnp.int32, (A_HEAD_DIM, PAIR), 1)
    to_lo = jnp.where(hlane == hrow, 1.0, 0.0).astype(BF16)
    to_hi = jnp.where(hlane == hrow + A_HEAD_DIM, 1.0, 0.0).astype(BF16)

    @pl.when(c == 0)
    def _():
        for b in range(bt):
            xs_ref[b, 7:8, :] = shift_ref[b]
            for p in range(N_PAIRS):
                s_ref[b, p] = jnp.concatenate(
                    [_dot_x01(s0_ref[b, 2 * p], to_lo), _dot_x01(s0_ref[b, 2 * p + 1], to_hi)], axis=0)

    def project(c0, c1):
        rw = jnp.dot(h, w_ref[:, c0:c1], preferred_element_type=F32)
        prevs = []
        for b in range(bt):
            rw_b = rw[b * frames:(b + 1) * frames]
            xs_ref[b, 8:8 + frames, c0:c1] = rw_b
            prevs.append(xs_ref[b, 7:7 + frames, c0:c1])
            last = rw_b[frames - 1:frames, :]
            xs_ref[b, 7:8, c0:c1] = last
            shift_out_ref[b, :, c0:c1] = last
        return rw + (jnp.concatenate(prevs, axis=0) - rw) * mu_ref[:, c0:c1]

    w0 = pv_ref[0:1, :]
    a0 = pv_ref[1:2, :]
    k_k = pv_ref[2:3, :]
    k_a = pv_ref[3:4, :]
    r_k = pv_ref[4:5, :]
    ln_w = pv_ref[5:6, :]
    ln_b = pv_ref[6:7, :]

    row = lax.broadcasted_iota(jnp.int32, (rows, rows), 0)
    col = lax.broadcasted_iota(jnp.int32, (rows, rows), 1)
    chunk_shift = _log2(c_len)
    causal = jnp.logical_and(
        lax.shift_right_logical(row, chunk_shift) == lax.shift_right_logical(col, chunk_shift), row >= col)

    lora = project(LORA_WA, GROUP_COLS)
    yield
    k = project(DA, 2 * DA)
    yield
    wa = lora[:, 0:LORA_G - LORA_WA]
    gl = lora[:, LORA_G - LORA_WA:LORA_G - LORA_WA + LORA_G_PAD]
    w_raw = w0 + _dot(jnp.tanh(wa), w2_ref[...])
    a_gate = _sigmoid(a0 + _dot(wa, a2_ref[...]))
    g = _dot(_sigmoid(gl), g2_ref[...])
    yield
    r = project(0, DA)
    yield
    lw = (-math.exp(-0.5)) * _sigmoid(w_raw)
    g_cum = _dot_01(jnp.where(causal, 1.0, 0.0).astype(BF16), lw)
    yield
    v = project(2 * DA, 3 * DA)
    yield
    if layer == 0:
        for b in range(bt):
            vfirst_out_ref[b] = v[b * frames:(b + 1) * frames]
    else:
        v0 = pv_ref[7:8, :]
        vg = _sigmoid(v0 + _dot(_dot(v, vw1_ref[...]), vw2_ref[...]))
        vfirst = jnp.concatenate([vfirst_ref[b] for b in range(bt)], axis=0)
        v = v + (vfirst - v) * vg

    lane = lax.broadcasted_iota(jnp.int32, (PAIR, PAIR), 1)
    sub = lax.broadcasted_iota(jnp.int32, (PAIR, PAIR), 0)
    head_ones = jnp.where((lane < A_HEAD_DIM) == (sub < A_HEAD_DIM), 1.0, 0.0).astype(BF16)

    def seg_sum(x):
        return jnp.concatenate(
            [jnp.dot(x[:, p * PAIR:(p + 1) * PAIR].astype(BF16), head_ones, preferred_element_type=F32)
             for p in range(N_PAIRS)], axis=1)

    kk = k * k_k
    kk = kk * lax.rsqrt(jnp.maximum(seg_sum(kk * kk), 1e-24))
    yield
    k_h = k * (1.0 + (a_gate - 1.0) * k_a)
    a_vec = -kk
    b_vec = kk * a_gate

    g_end = jnp.concatenate(
        [jnp.broadcast_to(g_cum[(q + 1) * c_len - 1:(q + 1) * c_len, :], (c_len, DA)) for q in range(bt * nch)],
        axis=0)
    e_pos = jnp.exp(g_cum)
    e_neg = jnp.exp(-g_cum)
    e_end = jnp.exp(g_end - g_cum)
    rq = r * e_pos
    aq = a_vec * jnp.exp(g_cum - lw)
    kn = k_h * e_neg
    bn = b_vec * e_neg
    k_end = k_h * e_end
    b_end = b_vec * e_end
    decay_end = jnp.exp(g_end)

    srow = lax.broadcasted_iota(jnp.int32, (stk, PAIR), 0)
    slane = lax.broadcasted_iota(jnp.int32, (stk, PAIR), 1)
    own_lanes = (srow < c_len) == (slane < A_HEAD_DIM)

    def stack(x):
        return jnp.where(own_lanes, jnp.concatenate([x, x], axis=0), 0.0).astype(BF16)

    mrow = lax.broadcasted_iota(jnp.int32, (stk, stk), 0)
    mcol = lax.broadcasted_iota(jnp.int32, (stk, stk), 1)
    strict = (mrow & (c_len - 1)) > (mcol & (c_len - 1))
    lower = (mrow & (c_len - 1)) >= (mcol & (c_len - 1))
    eye = jnp.where(mrow == mcol, 1.0, 0.0).astype(F32)

    def cut(x, u):
        b, ch, p = u
        r0 = (b * nch + ch) * c_len
        return x[r0:r0 + c_len, p * PAIR:(p + 1) * PAIR]

    lhs, v_st, a_ak_v, a_rk, a_rb, t_inv, y_parts = {}, {}, {}, {}, {}, {}, {}

    def scores_and_inverse(units):
        power = {}
        for u in units:
            lhs[u] = jnp.concatenate([stack(cut(aq, u)), stack(cut(rq, u))], axis=0)
            v_st[u] = stack(cut(v, u))
            sc_b = _dot_nt(lhs[u], stack(cut(bn, u)))
            sc_k = _dot_nt(lhs[u], stack(cut(kn, u)))
            power[u] = jnp.where(strict, sc_b[:stk], 0.0)
            a_rb[u] = jnp.where(lower, sc_b[stk:], 0.0)
            a_rk[u] = jnp.where(lower, sc_k[stk:], 0.0)
            a_ak_v[u] = jnp.where(strict, sc_k[:stk], 0.0)
        yield
        for u in units:
            t_inv[u] = eye + power[u]
            a_ak_v[u] = _dot(a_ak_v[u], v_st[u])
            power[u] = _dot(power[u], power[u])
        yield
        for lvl in range(1, _log2(c_len)):
            new_power = {}
            for u in units:
                t_inv[u] = t_inv[u] + _dot(power[u], t_inv[u])
                if lvl < _log2(c_len) - 1:
                    new_power[u] = _dot(power[u], power[u])
            power = new_power
            yield

    def apply_state(units):
        from_state = {u: _dot_nt(lhs[u], s_ref[u[0], u[2]]) for u in units}
        yield
        corr = {u: _dot(t_inv[u], from_state[u][:stk] + a_ak_v[u]) for u in units}
        yield
        for u in units:
            vu = jnp.concatenate([v_st[u], corr[u].astype(BF16)], axis=0)
            if stk % LANES == 0:
                y_st = from_state[u][stk:] + _dot(jnp.concatenate([a_rk[u], a_rb[u]], axis=1), vu)
            else:
                y_st = from_state[u][stk:] + _dot(a_rk[u], v_st[u]) + _dot(a_rb[u], corr[u])
            y_parts[u] = y_st[:c_len] + y_st[c_len:]
            ends = jnp.concatenate([stack(cut(k_end, u)), stack(cut(b_end, u))], axis=0)
            s_ref[u[0], u[2]] = s_ref[u[0], u[2]] * cut(decay_end, u)[0:1, :] + _dot_tn(vu, ends)
        yield

    chunk_units = [[(b, ch, p) for b in range(bt) for p in range(N_PAIRS)] for ch in range(nch)]
    yield from scores_and_inverse(chunk_units[0])
    for ch in range(nch):
        if ch + 1 < nch:
            yield from _round_robin(scores_and_inverse(chunk_units[ch + 1]), apply_state(chunk_units[ch]))
        else:
            yield from apply_state(chunk_units[ch])

    y = jnp.concatenate(
        [jnp.concatenate([y_parts[(b, ch, p)] for p in range(N_PAIRS)], axis=1)
         for b in range(bt) for ch in range(nch)], axis=0)
    inv_n = 1.0 / A_HEAD_DIM
    mean = seg_sum(y) * inv_n
    bonus = seg_sum(r * k_h * r_k)
    yield
    dev = y - mean
    var = seg_sum(dev * dev) * inv_n
    y = dev * lax.rsqrt(var + GN_EPS) * ln_w + ln_b
    y = ((y + bonus * v) * g).astype(BF16)
    for b in range(bt):
        yg_ref[b] = y[b * frames:(b + 1) * frames]

    @pl.when(c == pl.num_programs(1) - 1)
    def _():
        prow = lax.broadcasted_iota(jnp.int32, (PAIR, A_HEAD_DIM), 0)
        pcol = lax.broadcasted_iota(jnp.int32, (PAIR, A_HEAD_DIM), 1)
        from_lo = jnp.where(prow == pcol, 1.0, 0.0).astype(BF16)
        from_hi = jnp.where(prow == pcol + A_HEAD_DIM, 1.0, 0.0).astype(BF16)
        for b in range(bt):
            for p in range(N_PAIRS):
                s_pair = s_ref[b, p]
                s_out_ref[b, 2 * p] = _dot_x01(s_pair[:A_HEAD_DIM], from_lo)
                s_out_ref[b, 2 * p + 1] = _dot_x01(s_pair[A_HEAD_DIM:], from_hi)


def _hgrn_part(layer, bt, tb, c, h, refs):
    (w_ref, lg_ref, nw_ref, s0_ref, on_ref, s_out_ref, st_ref) = refs
    n_blk = tb // HGRN_BLOCK

    def project(c0):
        return [jnp.dot(h[b * tb:(b + 1) * tb], w_ref[:, c0:c0 + BK], preferred_element_type=F32)
                for b in range(bt)]

    fz_all = project(BK)
    yield
    q_all = project(0)
    yield

    @pl.when(c == 0)
    def _():
        for b in range(bt):
            for hd in range(B_HEADS):
                st_ref[b, hd] = s0_ref[b, hd].T

    lg = lg_ref[...]
    lrows = [lg[i:i + 1, :] for i in range(DEPTH)]
    mx = functools.reduce(jnp.maximum, lrows)
    ex = [jnp.exp(x - mx) for x in lrows]
    den = functools.reduce(lambda a, b: a + b, ex)
    sm = [e / den for e in ex]
    lb = functools.reduce(lambda a, b: a + b, sm[:layer + 1]) - sm[0]

    row = lax.broadcasted_iota(jnp.int32, (tb, tb), 0)
    col = lax.broadcasted_iota(jnp.int32, (tb, tb), 1)
    blk_shift = _log2(HGRN_BLOCK)
    causal = jnp.logical_and(lax.shift_right_logical(row, blk_shift) == lax.shift_right_logical(col, blk_shift),
                             row >= col)
    causal01 = jnp.where(causal, 1.0, 0.0).astype(BF16)

    f_all = [lb + (1.0 - lb) * _sigmoid(fz_all[b]) for b in range(bt)]
    g_cum_all = [_dot_01(causal01, jnp.log(f_all[b])) for b in range(bt)]
    yield
    iv = project(2 * BK)
    yield

    frow = lax.broadcasted_iota(jnp.int32, (tb, BK), 0)
    second = (lax.shift_right_logical(frow, blk_shift) & 1) == 1
    zeros_blk = jnp.zeros((HGRN_BLOCK, BK), F32)
    pair_mask = jnp.logical_and(
        lax.shift_right_logical(row, blk_shift + 1) == lax.shift_right_logical(col, blk_shift + 1),
        lax.shift_right_logical(row, blk_shift) > lax.shift_right_logical(col, blk_shift))

    qg, qg_pair, kg, k_end, k_end_pair, decay_pair = [], [], [], [], [], []
    for b in range(bt):
        q = q_all[b]
        kx = 1.0 - f_all[b]
        g_cum = g_cum_all[b]
        g_end = jnp.concatenate(
            [jnp.broadcast_to(g_cum[(j + 1) * HGRN_BLOCK - 1:(j + 1) * HGRN_BLOCK, :], (HGRN_BLOCK, BK))
             for j in range(n_blk)], axis=0)
        decay = jnp.exp(g_end)
        decay_prev = jnp.concatenate([zeros_blk, decay[:-HGRN_BLOCK]], axis=0)
        decay_next = jnp.concatenate([decay[HGRN_BLOCK:], zeros_blk], axis=0)
        q_dec = q * _sigmoid(q) * jnp.exp(g_cum)
        k_to_end = kx * jnp.exp(g_end - g_cum)
        qg.append(q_dec.astype(BF16))
        qg_pair.append((q_dec * jnp.where(second, decay_prev, 1.0)).astype(BF16))
        kg.append((kx * jnp.exp(-g_cum)).astype(BF16))
        k_end.append(k_to_end.astype(BF16))
        k_end_pair.append((k_to_end * jnp.where(second, 1.0, decay_next)).astype(BF16))
        decay_pair.append(decay * decay_next)

    units = [(b, hd) for b in range(bt) for hd in range(B_HEADS)]
    n_pair = tb // PAIR_LEN

    def cut(x, u):
        return x[u[0]][:, u[1] * B_KEY_DIM:(u[1] + 1) * B_KEY_DIM]

    def pair(j):
        return slice(j * PAIR_LEN, (j + 1) * PAIR_LEN)

    att = {u: jnp.where(causal, _dot_nt(cut(qg, u), cut(kg, u)), 0.0) for u in units}
    yield
    att = {u: att[u] + jnp.where(pair_mask, _dot_nt(cut(qg, u), cut(k_end, u)), 0.0) for u in units}
    yield
    upd = {}
    for j in range(n_pair):
        for u in units:
            upd[(u, j)] = _dot_tn(cut(iv, u)[pair(j)], cut(k_end_pair, u)[pair(j)])
        yield
    starts = {}
    for u in units:
        s = st_ref[u[0], u[1]]
        dec = cut(decay_pair, u)
        for j in range(n_pair):
            starts[(u, j)] = s.astype(BF16)
            s = s * dec[j * PAIR_LEN:j * PAIR_LEN + 1, :] + upd[(u, j)]
        st_ref[u[0], u[1]] = s
    og = project(2 * BK + DB)
    yield
    o_acc = {u: _dot(att[u], cut(iv, u)) for u in units}
    yield
    inter = {}
    for j in range(n_pair):
        for u in units:
            inter[(u, j)] = _dot_nt(cut(qg_pair, u)[pair(j)], starts[(u, j)])
        yield

    for b in range(bt):
        outs = []
        for hd in range(B_HEADS):
            u = (b, hd)
            o_h = o_acc[u] + jnp.concatenate([inter[(u, j)] for j in range(n_pair)], axis=0)
            outs.append(o_h * lax.rsqrt(jnp.mean(o_h * o_h, axis=-1, keepdims=True) + RMS_EPS))
        o = jnp.concatenate(outs, axis=1)
        on_ref[b] = (o * nw_ref[...] * (og[b] * _sigmoid(og[b]))).astype(BF16)

    @pl.when(c == pl.num_programs(1) - 1)
    def _():
        for b in range(bt):
            for hd in range(B_HEADS):
                s_out_ref[b, hd] = st_ref[b, hd].T


def _drive(gen):
    for _ in gen:
        pass


def _normalised(x_ref, nm_ref, bt):
    return jnp.concatenate([_rms(x_ref[b], nm_ref[...]).astype(BF16) for b in range(bt)], axis=0)


def _rwkv_kernel(layer, bt, nch, c_len, x_ref, nm_ref, *refs):
    _drive(_rwkv_part(layer, bt, nch, c_len, pl.program_id(1), _normalised(x_ref, nm_ref, bt), refs))


def _hgrn_kernel(layer, bt, tb, x_ref, nm_ref, *refs):
    _drive(_hgrn_part(layer, bt, tb, pl.program_id(1), _normalised(x_ref, nm_ref, bt), refs))


def _rwkv(layer, x, shift0, s_all, vfirst, nm, w_rw_all, mu_p, pv, w2_p, a2_p, g2_p, vw1_p, vw2_p, c_len, nch, bt):
    bsz, t_len, _ = x.shape
    frames = nch * c_len
    const2 = lambda b, c: (0, 0)
    step = lambda b, c: (b, c, 0)
    per_seq3 = lambda b, c: (b, 0, 0)
    per_seq4 = lambda b, c: (b, 0, 0, 0)
    in_specs = [
        pl.BlockSpec((bt, frames, D_MODEL), step),
        pl.BlockSpec((1, D_MODEL), const2),
        pl.BlockSpec((bt, 1, GROUP_COLS), per_seq3),
        pl.BlockSpec((None, bt, A_HEADS, A_HEAD_DIM, A_HEAD_DIM), lambda b, c: (layer, b, 0, 0, 0)),
    ]
    args = [x, nm, shift0, s_all]
    if layer > 0:
        in_specs.append(pl.BlockSpec((bt, frames, DA), step))
        args.append(vfirst)
    in_specs += [
        _resident((None, D_MODEL, GROUP_COLS), lambda b, c: (layer, 0, 0)),
        pl.BlockSpec((1, GROUP_COLS), const2),
        pl.BlockSpec((8, DA), const2),
        pl.BlockSpec((LANES, DA), const2),
        pl.BlockSpec((LANES, DA), const2),
        pl.BlockSpec((LORA_G_PAD, DA), const2),
    ]
    args += [w_rw_all, mu_p, pv, w2_p, a2_p, g2_p]
    if layer > 0:
        in_specs += [pl.BlockSpec((DA, LANES), const2), pl.BlockSpec((LANES, DA), const2)]
        args += [vw1_p, vw2_p]
    out_specs = [
        pl.BlockSpec((bt, frames, DA), step),
        pl.BlockSpec((bt, 1, GROUP_COLS), per_seq3),
        pl.BlockSpec((bt, A_HEADS, A_HEAD_DIM, A_HEAD_DIM), per_seq4),
    ]
    out_shape = [
        jax.ShapeDtypeStruct((bsz, t_len, DA), BF16),
        jax.ShapeDtypeStruct((bsz, 1, GROUP_COLS), F32),
        jax.ShapeDtypeStruct((bsz, A_HEADS, A_HEAD_DIM, A_HEAD_DIM), F32),
    ]
    if layer == 0:
        out_specs.append(pl.BlockSpec((bt, frames, DA), step))
        out_shape.append(jax.ShapeDtypeStruct((bsz, t_len, DA), F32))
    return pl.pallas_call(
        functools.partial(_rwkv_kernel, layer, bt, nch, c_len),
        grid=(bsz // bt, t_len // frames),
        in_specs=in_specs,
        out_specs=out_specs,
        out_shape=out_shape,
        scratch_shapes=[pltpu.VMEM((bt, frames + 8, GROUP_COLS), F32),
                        pltpu.VMEM((bt, N_PAIRS, PAIR, PAIR), F32)],
        compiler_params=pltpu.CompilerParams(
            dimension_semantics=("arbitrary", "arbitrary"), vmem_limit_bytes=VMEM_LIMIT),
        name=f"rwkv{layer}",
    )(*args)


def _hgrn(layer, x, nm, w_hg_all, lb_logits, norm_w, s_all, tb, bt):
    bsz, t_len, _ = x.shape
    const2 = lambda b, t: (0, 0)
    return pl.pallas_call(
        functools.partial(_hgrn_kernel, layer, bt, tb),
        grid=(bsz // bt, t_len // tb),
        in_specs=[
            pl.BlockSpec((bt, tb, D_MODEL), lambda b, t: (b, t, 0)),
            pl.BlockSpec((1, D_MODEL), const2),
            _resident((None, D_MODEL, HGRN_COLS), lambda b, t: (layer, 0, 0)),
            pl.BlockSpec((DEPTH, BK), const2),
            pl.BlockSpec((1, DB), const2),
            pl.BlockSpec((None, bt, B_HEADS, B_KEY_DIM, B_VAL_DIM), lambda b, t: (layer, b, 0, 0, 0)),
        ],
        out_specs=[
            pl.BlockSpec((bt, tb, DB), lambda b, t: (b, t, 0)),
            pl.BlockSpec((bt, B_HEADS, B_KEY_DIM, B_VAL_DIM), lambda b, t: (b, 0, 0, 0)),
        ],
        out_shape=[
            jax.ShapeDtypeStruct((bsz, t_len, DB), BF16),
            jax.ShapeDtypeStruct((bsz, B_HEADS, B_KEY_DIM, B_VAL_DIM), F32),
        ],
        scratch_shapes=[pltpu.VMEM((bt, B_HEADS, B_VAL_DIM, B_KEY_DIM), F32)],
        compiler_params=pltpu.CompilerParams(
            dimension_semantics=("arbitrary", "arbitrary"), vmem_limit_bytes=VMEM_LIMIT),
        name=f"hgrn{layer}",
    )(x, nm, w_hg_all, lb_logits, norm_w, s_all)


FF_CHUNK = 1024


def _mix_ffn_kernel(final, *refs):
    if final:
        (x_ref, yg_ref, on_ref, nm_ref, wg_ref, wa_ref, wb_ref, wo_ref, nf_ref, up_ref, dn_ref, nfin_ref,
         o_ref) = refs
    else:
        (x_ref, yg_ref, on_ref, nm_ref, wg_ref, wa_ref, wb_ref, wo_ref, nf_ref, up_ref, dn_ref, o_ref) = refs
    x = x_ref[...]
    gt = _sigmoid(jnp.dot(_rms(x, nm_ref[...]).astype(BF16), wg_ref[...], preferred_element_type=F32))
    ya = jnp.dot(yg_ref[...], wa_ref[...], preferred_element_type=F32)
    yb = jnp.dot(on_ref[...], wb_ref[...], preferred_element_type=F32)
    merged = gt[:, 0:D_MODEL] * ya + gt[:, D_MODEL:] * yb
    x = x + jnp.dot(merged.astype(BF16), wo_ref[...], preferred_element_type=F32)
    h = _rms(x, nf_ref[...]).astype(BF16)
    acc = x
    for cf in range(D_FF // FF_CHUNK):
        u = jnp.dot(h, up_ref[:, cf * FF_CHUNK:(cf + 1) * FF_CHUNK], preferred_element_type=F32)
        u = jnp.maximum(u, 0.0)
        acc = acc + jnp.dot((u * u).astype(BF16), dn_ref[cf * FF_CHUNK:(cf + 1) * FF_CHUNK, :],
                            preferred_element_type=F32)
    if final:
        acc = _rms(acc, nfin_ref[...])
    o_ref[...] = acc


def _mix_ffn(layer, x, yg, on, nm, nf, nfin, w):
    final = layer == DEPTH - 1
    n = x.shape[0]
    tm = min(n, 512)
    small = lambda i: (0, 0)
    mine = lambda i: (layer, 0, 0)
    in_specs = [
        pl.BlockSpec((tm, D_MODEL), lambda i: (i, 0)),
        pl.BlockSpec((tm, DA), lambda i: (i, 0)),
        pl.BlockSpec((tm, DB), lambda i: (i, 0)),
        _resident((1, D_MODEL), small),
        _resident((None, D_MODEL, GATE_COLS), mine),
        _resident((None, DA, D_MODEL), mine),
        _resident((None, DB, D_MODEL), mine),
        _resident((None, D_MODEL, D_MODEL), mine),
        _resident((1, D_MODEL), small),
        _resident((None, D_MODEL, D_FF), mine),
        _resident((None, D_FF, D_MODEL), mine),
    ]
    args = [x, yg, on, nm, w["wg"], w["wa"], w["wb"], w["wo"], nf, w["up"], w["dn"]]
    if final:
        in_specs.append(_resident((1, D_MODEL), small))
        args.append(nfin)
    return pl.pallas_call(
        functools.partial(_mix_ffn_kernel, final),
        grid=(n // tm,),
        in_specs=in_specs,
        out_specs=pl.BlockSpec((tm, D_MODEL), lambda i: (i, 0)),
        out_shape=jax.ShapeDtypeStruct((n, D_MODEL), F32),
        compiler_params=pltpu.CompilerParams(dimension_semantics=("arbitrary",), vmem_limit_bytes=VMEM_LIMIT),
        name="mix_ffn_final" if final else "mix_ffn",
    )(*args)


def _pad_rows(w, rows, at=0):
    out = jnp.zeros((rows, w.shape[1]), w.dtype)
    return out.at[at:at + w.shape[0]].set(w)


def _prep_weights(p):
    w_in = p["w_in"]
    pad = ((0, 0), (0, 0), (0, GROUP_COLS - RWKV_COLS))
    return dict(
        w_rw=jnp.pad(w_in[:, :, :RWKV_COLS], pad).astype(BF16),
        w_hg=w_in[:, :, RWKV_COLS:RWKV_COLS + HGRN_COLS].astype(BF16),
        wg=w_in[:, :, RWKV_COLS + HGRN_COLS:].astype(BF16),
        wa=p["w_out_a"].astype(BF16), wb=p["w_out_b"].astype(BF16), wo=p["w_out"].astype(BF16),
        up=p["w_ffn_up"].astype(BF16), dn=p["w_ffn_down"].astype(BF16),
    )


def _prep_layer(l, p):
    mu_p = jnp.zeros((1, GROUP_COLS), F32).at[0, :RWKV_COLS].set(p["rwkv_mu"][l])
    v0 = p["rwkv_v0"][l - 1] if l > 0 else jnp.zeros((DA,), F32)
    pv = jnp.stack([p["rwkv_w0"][l], p["rwkv_a0"][l], p["rwkv_k_k"][l], p["rwkv_k_a"][l],
                    p["rwkv_r_k"][l].reshape(DA), p["rwkv_ln_w"][l], p["rwkv_ln_b"][l], v0])
    out = dict(
        mu_p=mu_p, pv=pv,
        w2_p=_pad_rows(p["rwkv_w2"][l], LANES, 0).astype(BF16),
        a2_p=_pad_rows(p["rwkv_a2"][l], LANES, W_LORA).astype(BF16),
        g2_p=_pad_rows(p["rwkv_g2"][l], LORA_G_PAD, 0).astype(BF16),
        vw1_p=None, vw2_p=None,
        norm_mix=p["norm_mix"][l].reshape(1, D_MODEL),
        hgrn_norm_w=p["hgrn_norm_w"][l].reshape(1, DB),
        nf=p["norm_ffn"][l].reshape(1, D_MODEL),
    )
    if l > 0:
        w1 = p["rwkv_vres_w1"][l - 1]
        out["vw1_p"] = jnp.zeros((DA, LANES), F32).at[:, :V_LORA].set(w1).astype(BF16)
        out["vw2_p"] = _pad_rows(p["rwkv_vres_w2"][l - 1], LANES, 0).astype(BF16)
    return out


def _seq_tile(bsz, frames, step_rows):
    bt = min(bsz, max(1, step_rows // frames))
    assert bsz % bt == 0, (bsz, bt)
    return bt


def _trunk(x, state_shift, state_rwkv, state_hgrn, layers, weights, p):
    bsz, t_len, _ = x.shape
    c_len = min(t_len, RWKV_CHUNK)
    nch = min(t_len // c_len, RWKV_CHUNKS)
    tb = min(t_len, HGRN_TILE)
    assert t_len % (nch * c_len) == 0 and t_len % tb == 0 and tb % PAIR_LEN == 0
    nfin = p["norm_final"].reshape(1, D_MODEL)
    vfirst = None
    shifts, rwkv_states, hgrn_states = [], [], []
    for l in range(DEPTH):
        lp = layers[l]
        shift0 = jnp.zeros((bsz, 1, GROUP_COLS), F32).at[:, 0, :RWKV_COLS].set(state_shift[l])
        res = _rwkv(l, x, shift0, state_rwkv, vfirst, lp["norm_mix"], weights["w_rw"], lp["mu_p"], lp["pv"],
                    lp["w2_p"], lp["a2_p"], lp["g2_p"], lp["vw1_p"], lp["vw2_p"], c_len, nch,
                    _seq_tile(bsz, nch * c_len, RWKV_STEP_ROWS))
        if l == 0:
            yg, shift_out, s_rwkv, vfirst = res
        else:
            yg, shift_out, s_rwkv = res
        on, s_hgrn = _hgrn(l, x, lp["norm_mix"], weights["w_hg"], p["hgrn_lb_logits"], lp["hgrn_norm_w"],
                           state_hgrn, tb, _seq_tile(bsz, tb, HGRN_STEP_ROWS))
        n = bsz * t_len
        x = _mix_ffn(l, x.reshape(n, D_MODEL), yg.reshape(n, DA), on.reshape(n, DB), lp["norm_mix"], lp["nf"],
                     nfin, weights).reshape(bsz, t_len, D_MODEL)
        shifts.append(shift_out[:, 0, :RWKV_COLS])
        rwkv_states.append(s_rwkv)
        hgrn_states.append(s_hgrn)
    return x, jnp.stack(shifts), jnp.stack(rwkv_states), jnp.stack(hgrn_states)


def kernel(x_prompt, x_sample, state_shift, state_rwkv, state_hgrn, norm_mix, w_in, rwkv_mu, rwkv_w0, rwkv_w2, rwkv_a0, rwkv_a2, rwkv_g2, rwkv_v0, rwkv_vres_w1, rwkv_vres_w2, rwkv_k_k, rwkv_k_a, rwkv_r_k, rwkv_ln_w, rwkv_ln_b, hgrn_lb_logits, hgrn_norm_w, w_out_a, w_out_b, w_out, norm_ffn, w_ffn_up, w_ffn_down, norm_final):
    p = dict(norm_mix=norm_mix, w_in=w_in, rwkv_mu=rwkv_mu, rwkv_w0=rwkv_w0, rwkv_w2=rwkv_w2, rwkv_a0=rwkv_a0,
             rwkv_a2=rwkv_a2, rwkv_g2=rwkv_g2, rwkv_v0=rwkv_v0, rwkv_vres_w1=rwkv_vres_w1,
             rwkv_vres_w2=rwkv_vres_w2, rwkv_k_k=rwkv_k_k, rwkv_k_a=rwkv_k_a, rwkv_r_k=rwkv_r_k,
             rwkv_ln_w=rwkv_ln_w, rwkv_ln_b=rwkv_ln_b, hgrn_lb_logits=hgrn_lb_logits, hgrn_norm_w=hgrn_norm_w,
             w_out_a=w_out_a, w_out_b=w_out_b, w_out=w_out, norm_ffn=norm_ffn, w_ffn_up=w_ffn_up,
             w_ffn_down=w_ffn_down, norm_final=norm_final)
    layers = [_prep_layer(l, p) for l in range(DEPTH)]
    weights = _prep_weights(p)
    bp = x_prompt.shape[0]
    dt = x_prompt.dtype
    zero_shift = jnp.zeros((DEPTH, bp, RWKV_COLS), dt)
    zero_rwkv = jnp.zeros((DEPTH, bp, A_HEADS, A_HEAD_DIM, A_HEAD_DIM), dt)
    zero_hgrn = jnp.zeros((DEPTH, bp, B_HEADS, B_KEY_DIM, B_VAL_DIM), dt)
    y_prompt, shift_p, rwkv_p, hgrn_p = _trunk(x_prompt, zero_shift, zero_rwkv, zero_hgrn, layers, weights, p)
    y_sample, shift_s, rwkv_s, hgrn_s = _trunk(x_sample, state_shift, state_rwkv, state_hgrn, layers, weights, p)
    return (y_prompt, y_sample, shift_p, rwkv_p, hgrn_p, shift_s, rwkv_s, hgrn_s)
```

```python
import functools
import math

import jax
import jax.numpy as jnp
from jax import lax
from jax.experimental import pallas as pl
from jax.experimental.pallas import tpu as pltpu

F32 = jnp.float32
BF16 = jnp.bfloat16

D_MODEL = 1024
DEPTH = 2
A_HEADS = 8
A_HEAD_DIM = 64
DA = A_HEADS * A_HEAD_DIM
W_LORA = 64
A_LORA = 64
V_LORA = 32
G_LORA = 160
RWKV_COLS = 3 * DA + W_LORA + A_LORA + G_LORA
B_HEADS = 4
B_KEY_DIM = 128
B_VAL_DIM = 128
BK = B_HEADS * B_KEY_DIM
DB = B_HEADS * B_VAL_DIM
HGRN_COLS = 2 * BK + 2 * DB
GATE_COLS = 2 * D_MODEL
D_FF = 4 * D_MODEL
HGRN_BLOCK = 16
RMS_EPS = 1e-6
GN_EPS = 64e-5

LANES = 128
GROUP_COLS = 2048
LORA_WA = 3 * DA
LORA_G = LORA_WA + W_LORA + A_LORA
LORA_G_PAD = 256
PAIR = 2 * A_HEAD_DIM
N_PAIRS = A_HEADS // 2
PAIR_LEN = 2 * HGRN_BLOCK
VMEM_LIMIT = 56 * 1024 * 1024

RWKV_CHUNK = 64
RWKV_CHUNKS = 4
RWKV_STEP_ROWS = 512
CUMSUM_ROWS = 256
HGRN_TILE = 256
HGRN_STEP_ROWS = 512


def _dot(a, b):
    return jnp.dot(a.astype(BF16), b.astype(BF16), preferred_element_type=F32)


def _dot_nt(a, b):
    return lax.dot_general(a.astype(BF16), b.astype(BF16), (((1,), (1,)), ((), ())), preferred_element_type=F32)


def _dot_tn(a, b):
    return lax.dot_general(a.astype(BF16), b.astype(BF16), (((0,), (0,)), ((), ())), preferred_element_type=F32)


def _split3(x):
    hi = x.astype(BF16)
    r1 = x - hi.astype(F32)
    mid = r1.astype(BF16)
    return hi, mid, (r1 - mid.astype(F32)).astype(BF16)


def _dot_01(m01, x):
    return functools.reduce(lambda a, b: a + b, [jnp.dot(m01, t, preferred_element_type=F32) for t in _split3(x)])


def _dot_x01(x, m01):
    return functools.reduce(lambda a, b: a + b, [jnp.dot(t, m01, preferred_element_type=F32) for t in _split3(x)])


def _rms(x, w):
    return x * lax.rsqrt(jnp.mean(x * x, axis=-1, keepdims=True) + RMS_EPS) * w


def _sigmoid(x):
    return 1.0 / (1.0 + jnp.exp(-x))


def _log2(n):
    assert n & (n - 1) == 0, n
    return n.bit_length() - 1


def _resident(shape, index_map):
    return pl.BlockSpec(shape, index_map, pipeline_mode=pl.Buffered(1))


_DONE = object()


def _round_robin(*gens):
    live = list(gens)
    while live:
        for gen in list(live):
            if next(gen, _DONE) is _DONE:
                live.remove(gen)
            else:
                yield


def _rwkv_part(layer, bt, nch, c_len, c, h, refs):
    if layer == 0:
        (shift_ref, s0_ref, w_ref, mu_ref, pv_ref, w2_ref, a2_ref, g2_ref,
         yg_ref, shift_out_ref, s_out_ref, vfirst_out_ref, xs_ref, s_ref) = refs
    else:
        (shift_ref, s0_ref, vfirst_ref, w_ref, mu_ref, pv_ref, w2_ref, a2_ref, g2_ref, vw1_ref, vw2_ref,
         yg_ref, shift_out_ref, s_out_ref, xs_ref, s_ref) = refs
    frames = nch * c_len
    rows = bt * frames
    stk = 2 * c_len

    hrow = lax.broadcasted_iota(jnp.int32, (A_HEAD_DIM, PAIR), 0)
    hlane = lax.broadcasted_iota(jnp.int32, (A_HEAD_DIM, PAIR), 1)
    to_lo = jnp.where(hlane == hrow, 1.0, 0.0).astype(BF16)
    to_hi = jnp.where(hlane == hrow + A_HEAD_DIM, 1.0, 0.0).astype(BF16)

    @pl.when(c == 0)
    def _():
        for b in range(bt):
            xs_ref[b, 7:8, :] = shift_ref[b]
            for p in range(N_PAIRS):
                s_ref[b, p] = jnp.concatenate(
                    [_dot_x01(s0_ref[b, 2 * p], to_lo), _dot_x01(s0_ref[b, 2 * p + 1], to_hi)], axis=0)

    def project(c0, c1):
        rw = jnp.dot(h, w_ref[:, c0:c1], preferred_element_type=F32)
        prevs = []
        for b in range(bt):
            rw_b = rw[b * frames:(b + 1) * frames]
            xs_ref[b, 8:8 + frames, c0:c1] = rw_b
            prevs.append(xs_ref[b, 7:7 + frames, c0:c1])
            last = rw_b[frames - 1:frames, :]
            xs_ref[b, 7:8, c0:c1] = last
            shift_out_ref[b, :, c0:c1] = last
        return rw + (jnp.concatenate(prevs, axis=0) - rw) * mu_ref[:, c0:c1]

    w0 = pv_ref[0:1, :]
    a0 = pv_ref[1:2, :]
    k_k = pv_ref[2:3, :]
    k_a = pv_ref[3:4, :]
    r_k = pv_ref[4:5, :]
    ln_w = pv_ref[5:6, :]
    ln_b = pv_ref[6:7, :]

    cs_rows = min(rows, CUMSUM_ROWS)
    row = lax.broadcasted_iota(jnp.int32, (cs_rows, cs_rows), 0)
    col = lax.broadcasted_iota(jnp.int32, (cs_rows, cs_rows), 1)
    chunk_shift = _log2(c_len)
    causal01 = jnp.where(jnp.logical_and(
        lax.shift_right_logical(row, chunk_shift) == lax.shift_right_logical(col, chunk_shift), row >= col),
        1.0, 0.0).astype(BF16)

    lora = project(LORA_WA, GROUP_COLS)
    yield
    k = project(DA, 2 * DA)
    yield
    wa = lora[:, 0:LORA_G - LORA_WA]
    gl = lora[:, LORA_G - LORA_WA:LORA_G - LORA_WA + LORA_G_PAD]
    w_raw = w0 + _dot(jnp.tanh(wa), w2_ref[...])
    a_gate = _sigmoid(a0 + _dot(wa, a2_ref[...]))
    g = _dot(_sigmoid(gl), g2_ref[...])
    yield
    r = project(0, DA)
    yield
    lw = (-math.exp(-0.5)) * _sigmoid(w_raw)
    g_cum = jnp.concatenate([_dot_01(causal01, lw[i:i + cs_rows]) for i in range(0, rows, cs_rows)],
                            axis=0)
    yield
    v = project(2 * DA, 3 * DA)
    yield
    if layer == 0:
        for b in range(bt):
            vfirst_out_ref[b] = v[b * frames:(b + 1) * frames]
    else:
        v0 = pv_ref[7:8, :]
        vg = _sigmoid(v0 + _dot(_dot(v, vw1_ref[...]), vw2_ref[...]))
        vfirst = jnp.concatenate([vfirst_ref[b] for b in range(bt)], axis=0)
        v = v + (vfirst - v) * vg

    lane = lax.broadcasted_iota(jnp.int32, (PAIR, PAIR), 1)
    sub = lax.broadcasted_iota(jnp.int32, (PAIR, PAIR), 0)
    head_ones = jnp.where((lane < A_HEAD_DIM) == (sub < A_HEAD_DIM), 1.0, 0.0).astype(BF16)

    def seg_sum(x):
        return jnp.concatenate(
            [jnp.dot(x[:, p * PAIR:(p + 1) * PAIR].astype(BF16), head_ones, preferred_element_type=F32)
             for p in range(N_PAIRS)], axis=1)

    kk = k * k_k
    kk = kk * lax.rsqrt(jnp.maximum(seg_sum(kk * kk), 1e-24))
    yield
    k_h = k * (1.0 + (a_gate - 1.0) * k_a)
    a_vec = -kk
    b_vec = kk * a_gate

    g_end = jnp.concatenate(
        [jnp.broadcast_to(g_cum[(q + 1) * c_len - 1:(q + 1) * c_len, :], (c_len, DA)) for q in range(bt * nch)],
        axis=0)
    e_pos = jnp.exp(g_cum)
    e_neg = jnp.exp(-g_cum)
    e_end = jnp.exp(g_end - g_cum)
    rq = r * e_pos
    aq = a_vec * jnp.exp(g_cum - lw)
    kn = k_h * e_neg
    bn = b_vec * e_neg
    k_end = k_h * e_end
    b_end = b_vec * e_end
    decay_end = jnp.exp(g_end)

    srow = lax.broadcasted_iota(jnp.int32, (stk, PAIR), 0)
    slane = lax.broadcasted_iota(jnp.int32, (stk, PAIR), 1)
    own_lanes = (srow < c_len) == (slane < A_HEAD_DIM)

    def stack(x):
        return jnp.where(own_lanes, jnp.concatenate([x, x], axis=0), 0.0).astype(BF16)

    mrow = lax.broadcasted_iota(jnp.int32, (stk, stk), 0)
    mcol = lax.broadcasted_iota(jnp.int32, (stk, stk), 1)
    strict = (mrow & (c_len - 1)) > (mcol & (c_len - 1))
    lower = (mrow & (c_len - 1)) >= (mcol & (c_len - 1))
    eye = jnp.where(mrow == mcol, 1.0, 0.0).astype(F32)

    def cut(x, u):
        b, ch, p = u
        r0 = (b * nch + ch) * c_len
        return x[r0:r0 + c_len, p * PAIR:(p + 1) * PAIR]

    lhs, v_st, a_ak_v, a_rk, a_rb, t_inv, y_parts = {}, {}, {}, {}, {}, {}, {}

    def scores_and_inverse(units):
        power = {}
        for u in units:
            lhs[u] = jnp.concatenate([stack(cut(aq, u)), stack(cut(rq, u))], axis=0)
            v_st[u] = stack(cut(v, u))
            sc_b = _dot_nt(lhs[u], stack(cut(bn, u)))
            sc_k = _dot_nt(lhs[u], stack(cut(kn, u)))
            power[u] = jnp.where(strict, sc_b[:stk], 0.0)
            a_rb[u] = jnp.where(lower, sc_b[stk:], 0.0)
            a_rk[u] = jnp.where(lower, sc_k[stk:], 0.0)
            a_ak_v[u] = jnp.where(strict, sc_k[:stk], 0.0)
        yield
        for u in units:
            t_inv[u] = eye + power[u]
            a_ak_v[u] = _dot(a_ak_v[u], v_st[u])
            power[u] = _dot(power[u], power[u])
        yield
        for lvl in range(1, _log2(c_len)):
            new_power = {}
            for u in units:
                t_inv[u] = t_inv[u] + _dot(power[u], t_inv[u])
                if lvl < _log2(c_len) - 1:
                    new_power[u] = _dot(power[u], power[u])
            power = new_power
            yield

    def apply_state(units):
        from_state = {u: _dot_nt(lhs[u], s_ref[u[0], u[2]]) for u in units}
        yield
        corr = {u: _dot(t_inv[u], from_state[u][:stk] + a_ak_v[u]) for u in units}
        yield
        for u in units:
            vu = jnp.concatenate([v_st[u], corr[u].astype(BF16)], axis=0)
            if stk % LANES == 0:
                y_st = from_state[u][stk:] + _dot(jnp.concatenate([a_rk[u], a_rb[u]], axis=1), vu)
            else:
                y_st = from_state[u][stk:] + _dot(a_rk[u], v_st[u]) + _dot(a_rb[u], corr[u])
            y_parts[u] = y_st[:c_len] + y_st[c_len:]
            ends = jnp.concatenate([stack(cut(k_end, u)), stack(cut(b_end, u))], axis=0)
            s_ref[u[0], u[2]] = s_ref[u[0], u[2]] * cut(decay_end, u)[0:1, :] + _dot_tn(vu, ends)
        yield

    chunk_units = [[(b, ch, p) for b in range(bt) for p in range(N_PAIRS)] for ch in range(nch)]
    yield from scores_and_inverse(chunk_units[0])
    for ch in range(nch):
        if ch + 1 < nch:
            yield from _round_robin(scores_and_inverse(chunk_units[ch + 1]), apply_state(chunk_units[ch]))
        else:
            yield from apply_state(chunk_units[ch])

    y = jnp.concatenate(
        [jnp.concatenate([y_parts[(b, ch, p)] for p in range(N_PAIRS)], axis=1)
         for b in range(bt) for ch in range(nch)], axis=0)
    inv_n = 1.0 / A_HEAD_DIM
    mean = seg_sum(y) * inv_n
    bonus = seg_sum(r * k_h * r_k)
    yield
    dev = y - mean
    var = seg_sum(dev * dev) * inv_n
    y = dev * lax.rsqrt(var + GN_EPS) * ln_w + ln_b
    y = ((y + bonus * v) * g).astype(BF16)
    for b in range(bt):
        yg_ref[b] = y[b * frames:(b + 1) * frames]

    @pl.when(c == pl.num_programs(1) - 1)
    def _():
        prow = lax.broadcasted_iota(jnp.int32, (PAIR, A_HEAD_DIM), 0)
        pcol = lax.broadcasted_iota(jnp.int32, (PAIR, A_HEAD_DIM), 1)
        from_lo = jnp.where(prow == pcol, 1.0, 0.0).astype(BF16)
        from_hi = jnp.where(prow == pcol + A_HEAD_DIM, 1.0, 0.0).astype(BF16)
        for b in range(bt):
            for p in range(N_PAIRS):
                s_pair = s_ref[b, p]
                s_out_ref[b, 2 * p] = _dot_x01(s_pair[:A_HEAD_DIM], from_lo)
                s_out_ref[b, 2 * p + 1] = _dot_x01(s_pair[A_HEAD_DIM:], from_hi)


def _hgrn_part(layer, bt, tb, c, h, refs):
    (w_ref, lg_ref, nw_ref, s0_ref, on_ref, s_out_ref, st_ref) = refs
    n_blk = tb // HGRN_BLOCK

    def project(c0):
        return [jnp.dot(h[b * tb:(b + 1) * tb], w_ref[:, c0:c0 + BK], preferred_element_type=F32)
                for b in range(bt)]

    fz_all = project(BK)
    yield
    q_all = project(0)
    yield

    @pl.when(c == 0)
    def _():
        for b in range(bt):
            for hd in range(B_HEADS):
                st_ref[b, hd] = s0_ref[b, hd].T

    lg = lg_ref[...]
    lrows = [lg[i:i + 1, :] for i in range(DEPTH)]
    mx = functools.reduce(jnp.maximum, lrows)
    ex = [jnp.exp(x - mx) for x in lrows]
    den = functools.reduce(lambda a, b: a + b, ex)
    sm = [e / den for e in ex]
    lb = functools.reduce(lambda a, b: a + b, sm[:layer + 1]) - sm[0]

    row = lax.broadcasted_iota(jnp.int32, (tb, tb), 0)
    col = lax.broadcasted_iota(jnp.int32, (tb, tb), 1)
    blk_shift = _log2(HGRN_BLOCK)
    causal = jnp.logical_and(lax.shift_right_logical(row, blk_shift) == lax.shift_right_logical(col, blk_shift),
                             row >= col)
    causal01 = jnp.where(causal, 1.0, 0.0).astype(BF16)

    f_all = [lb + (1.0 - lb) * _sigmoid(fz_all[b]) for b in range(bt)]
    g_cum_all = [_dot_01(causal01, jnp.log(f_all[b])) for b in range(bt)]
    yield
    iv = project(2 * BK)
    yield

    frow = lax.broadcasted_iota(jnp.int32, (tb, BK), 0)
    second = (lax.shift_right_logical(frow, blk_shift) & 1) == 1
    zeros_blk = jnp.zeros((HGRN_BLOCK, BK), F32)
    pair_mask = jnp.logical_and(
        lax.shift_right_logical(row, blk_shift + 1) == lax.shift_right_logical(col, blk_shift + 1),
        lax.shift_right_logical(row, blk_shift) > lax.shift_right_logical(col, blk_shift))

    levels = [HGRN_BLOCK >> (i + 1) for i in range(blk_shift)]
    pos_in = {hl: frow & (2 * hl - 1) for hl in levels}
    level_mask = {hl: lax.shift_right_logical(row, _log2(2 * hl)) == lax.shift_right_logical(col, _log2(2 * hl))
                  for hl in levels}

    def anchor(g, hl):
        if 2 * hl >= 8:
            g3 = g.reshape(tb // (2 * hl), 2 * hl, BK)
            return jnp.broadcast_to(g3[:, hl - 1:hl, :], g3.shape).reshape(tb, BK)
        out = g
        for pos in range(2 * hl):
            if pos != hl - 1:
                out = jnp.where(pos_in[hl] == pos, pltpu.roll(g, (pos - (hl - 1)) % tb, 0), out)
        return out

    q_lvl = {hl: [] for hl in levels}
    k_lvl = {hl: [] for hl in levels}
    own = []
    qg, qg_pair, k_end, k_end_pair, decay_pair = [], [], [], [], []
    for b in range(bt):
        q = q_all[b]
        kx = 1.0 - f_all[b]
        g_cum = g_cum_all[b]
        qs = q * _sigmoid(q)
        own.append(qs * kx)
        for hl in levels:
            partial = jnp.exp(-jnp.abs(g_cum - anchor(g_cum, hl)))
            in_second = pos_in[hl] >= hl
            q_lvl[hl].append(jnp.where(in_second, qs * partial, 0.0).astype(BF16))
            k_lvl[hl].append(jnp.where(in_second, 0.0, kx * partial).astype(BF16))
        g_end = jnp.concatenate(
            [jnp.broadcast_to(g_cum[(j + 1) * HGRN_BLOCK - 1:(j + 1) * HGRN_BLOCK, :], (HGRN_BLOCK, BK))
             for j in range(n_blk)], axis=0)
        decay = jnp.exp(g_end)
        decay_prev = jnp.concatenate([zeros_blk, decay[:-HGRN_BLOCK]], axis=0)
        decay_next = jnp.concatenate([decay[HGRN_BLOCK:], zeros_blk], axis=0)
        q_dec = qs * jnp.exp(g_cum)
        k_to_end = kx * jnp.exp(g_end - g_cum)
        qg.append(q_dec.astype(BF16))
        qg_pair.append((q_dec * jnp.where(second, decay_prev, 1.0)).astype(BF16))
        k_end.append(k_to_end.astype(BF16))
        k_end_pair.append((k_to_end * jnp.where(second, 1.0, decay_next)).astype(BF16))
        decay_pair.append(decay * decay_next)

    units = [(b, hd) for b in range(bt) for hd in range(B_HEADS)]
    n_pair = tb // PAIR_LEN

    def cut(x, u):
        return x[u[0]][:, u[1] * B_KEY_DIM:(u[1] + 1) * B_KEY_DIM]

    def pair(j):
        return slice(j * PAIR_LEN, (j + 1) * PAIR_LEN)

    att = {u: jnp.where(pair_mask, _dot_nt(cut(qg, u), cut(k_end, u)), 0.0) for u in units}
    yield
    for hl in levels:
        att = {u: att[u] + jnp.where(level_mask[hl], _dot_nt(cut(q_lvl[hl], u), cut(k_lvl[hl], u)), 0.0)
               for u in units}
        yield
    upd = {}
    for j in range(n_pair):
        for u in units:
            upd[(u, j)] = _dot_tn(cut(iv, u)[pair(j)], cut(k_end_pair, u)[pair(j)])
        yield
    starts = {}
    for u in units:
        s = st_ref[u[0], u[1]]
        dec = cut(decay_pair, u)
        for j in range(n_pair):
            starts[(u, j)] = s.astype(BF16)
            s = s * dec[j * PAIR_LEN:j * PAIR_LEN + 1, :] + upd[(u, j)]
        st_ref[u[0], u[1]] = s
    og = project(2 * BK + DB)
    yield
    o_acc = {u: _dot(att[u], cut(iv, u)) for u in units}
    yield
    inter = {}
    for j in range(n_pair):
        for u in units:
            inter[(u, j)] = _dot_nt(cut(qg_pair, u)[pair(j)], starts[(u, j)])
        yield

    for b in range(bt):
        outs = []
        for hd in range(B_HEADS):
            u = (b, hd)
            o_h = o_acc[u] + jnp.concatenate([inter[(u, j)] for j in range(n_pair)], axis=0)
            o_h = o_h + jnp.sum(cut(own, u), axis=-1, keepdims=True) * cut(iv, u)
            outs.append(o_h * lax.rsqrt(jnp.mean(o_h * o_h, axis=-1, keepdims=True) + RMS_EPS))
        o = jnp.concatenate(outs, axis=1)
        on_ref[b] = (o * nw_ref[...] * (og[b] * _sigmoid(og[b]))).astype(BF16)

    @pl.when(c == pl.num_programs(1) - 1)
    def _():
        for b in range(bt):
            for hd in range(B_HEADS):
                s_out_ref[b, hd] = st_ref[b, hd].T


def _drive(gen):
    for _ in gen:
        pass


def _normalised(x_ref, nm_ref, bt):
    return jnp.concatenate([_rms(x_ref[b], nm_ref[...]).astype(BF16) for b in range(bt)], axis=0)


def _rwkv_kernel(layer, bt, nch, c_len, x_ref, nm_ref, *refs):
    _drive(_rwkv_part(layer, bt, nch, c_len, pl.program_id(1), _normalised(x_ref, nm_ref, bt), refs))


def _hgrn_kernel(layer, bt, tb, x_ref, nm_ref, *refs):
    _drive(_hgrn_part(layer, bt, tb, pl.program_id(1), _normalised(x_ref, nm_ref, bt), refs))


def _rwkv(layer, x, shift0, s_all, vfirst, nm, w_rw_all, mu_p, pv, w2_p, a2_p, g2_p, vw1_p, vw2_p, c_len, nch, bt):
    bsz, t_len, _ = x.shape
    frames = nch * c_len
    const2 = lambda b, c: (0, 0)
    step = lambda b, c: (b, c, 0)
    per_seq3 = lambda b, c: (b, 0, 0)
    per_seq4 = lambda b, c: (b, 0, 0, 0)
    in_specs = [
        pl.BlockSpec((bt, frames, D_MODEL), step),
        pl.BlockSpec((1, D_MODEL), const2),
        pl.BlockSpec((bt, 1, GROUP_COLS), per_seq3),
        pl.BlockSpec((None, bt, A_HEADS, A_HEAD_DIM, A_HEAD_DIM), lambda b, c: (layer, b, 0, 0, 0)),
    ]
    args = [x, nm, shift0, s_all]
    if layer > 0:
        in_specs.append(pl.BlockSpec((bt, frames, DA), step))
        args.append(vfirst)
    in_specs += [
        _resident((None, D_MODEL, GROUP_COLS), lambda b, c: (layer, 0, 0)),
        pl.BlockSpec((1, GROUP_COLS), const2),
        pl.BlockSpec((8, DA), const2),
        pl.BlockSpec((LANES, DA), const2),
        pl.BlockSpec((LANES, DA), const2),
        pl.BlockSpec((LORA_G_PAD, DA), const2),
    ]
    args += [w_rw_all, mu_p, pv, w2_p, a2_p, g2_p]
    if layer > 0:
        in_specs += [pl.BlockSpec((DA, LANES), const2), pl.BlockSpec((LANES, DA), const2)]
        args += [vw1_p, vw2_p]
    out_specs = [
        pl.BlockSpec((bt, frames, DA), step),
        pl.BlockSpec((bt, 1, GROUP_COLS), per_seq3),
        pl.BlockSpec((bt, A_HEADS, A_HEAD_DIM, A_HEAD_DIM), per_seq4),
    ]
    out_shape = [
        jax.ShapeDtypeStruct((bsz, t_len, DA), BF16),
        jax.ShapeDtypeStruct((bsz, 1, GROUP_COLS), F32),
        jax.ShapeDtypeStruct((bsz, A_HEADS, A_HEAD_DIM, A_HEAD_DIM), F32),
    ]
    if layer == 0:
        out_specs.append(pl.BlockSpec((bt, frames, DA), step))
        out_shape.append(jax.ShapeDtypeStruct((bsz, t_len, DA), F32))
    return pl.pallas_call(
        functools.partial(_rwkv_kernel, layer, bt, nch, c_len),
        grid=(bsz // bt, t_len // frames),
        in_specs=in_specs,
        out_specs=out_specs,
        out_shape=out_shape,
        scratch_shapes=[pltpu.VMEM((bt, frames + 8, GROUP_COLS), F32),
                        pltpu.VMEM((bt, N_PAIRS, PAIR, PAIR), F32)],
        compiler_params=pltpu.CompilerParams(
            dimension_semantics=("arbitrary", "arbitrary"), vmem_limit_bytes=VMEM_LIMIT),
        name=f"rwkv{layer}",
    )(*args)


def _hgrn(layer, x, nm, w_hg_all, lb_logits, norm_w, s_all, tb, bt):
    bsz, t_len, _ = x.shape
    const2 = lambda b, t: (0, 0)
    return pl.pallas_call(
        functools.partial(_hgrn_kernel, layer, bt, tb),
        grid=(bsz // bt, t_len // tb),
        in_specs=[
            pl.BlockSpec((bt, tb, D_MODEL), lambda b, t: (b, t, 0)),
            pl.BlockSpec((1, D_MODEL), const2),
            _resident((None, D_MODEL, HGRN_COLS), lambda b, t: (layer, 0, 0)),
            pl.BlockSpec((DEPTH, BK), const2),
            pl.BlockSpec((1, DB), const2),
            pl.BlockSpec((None, bt, B_HEADS, B_KEY_DIM, B_VAL_DIM), lambda b, t: (layer, b, 0, 0, 0)),
        ],
        out_specs=[
            pl.BlockSpec((bt, tb, DB), lambda b, t: (b, t, 0)),
            pl.BlockSpec((bt, B_HEADS, B_KEY_DIM, B_VAL_DIM), lambda b, t: (b, 0, 0, 0)),
        ],
        out_shape=[
            jax.ShapeDtypeStruct((bsz, t_len, DB), BF16),
            jax.ShapeDtypeStruct((bsz, B_HEADS, B_KEY_DIM, B_VAL_DIM), F32),
        ],
        scratch_shapes=[pltpu.VMEM((bt, B_HEADS, B_VAL_DIM, B_KEY_DIM), F32)],
        compiler_params=pltpu.CompilerParams(
            dimension_semantics=("arbitrary", "arbitrary"), vmem_limit_bytes=VMEM_LIMIT),
        name=f"hgrn{layer}",
    )(x, nm, w_hg_all, lb_logits, norm_w, s_all)


FF_CHUNK = 1024


def _mix_ffn_kernel(final, *refs):
    if final:
        (x_ref, yg_ref, on_ref, nm_ref, wg_ref, wa_ref, wb_ref, wo_ref, nf_ref, up_ref, dn_ref, nfin_ref,
         o_ref) = refs
    else:
        (x_ref, yg_ref, on_ref, nm_ref, wg_ref, wa_ref, wb_ref, wo_ref, nf_ref, up_ref, dn_ref, o_ref) = refs
    x = x_ref[...]
    gt = _sigmoid(jnp.dot(_rms(x, nm_ref[...]).astype(BF16), wg_ref[...], preferred_element_type=F32))
    ya = jnp.dot(yg_ref[...], wa_ref[...], preferred_element_type=F32)
    yb = jnp.dot(on_ref[...], wb_ref[...], preferred_element_type=F32)
    merged = gt[:, 0:D_MODEL] * ya + gt[:, D_MODEL:] * yb
    x = x + jnp.dot(merged.astype(BF16), wo_ref[...], preferred_element_type=F32)
    h = _rms(x, nf_ref[...]).astype(BF16)
    acc = x
    for cf in range(D_FF // FF_CHUNK):
        u = jnp.dot(h, up_ref[:, cf * FF_CHUNK:(cf + 1) * FF_CHUNK], preferred_element_type=F32)
        u = jnp.maximum(u, 0.0)
        acc = acc + jnp.dot((u * u).astype(BF16), dn_ref[cf * FF_CHUNK:(cf + 1) * FF_CHUNK, :],
                            preferred_element_type=F32)
    if final:
        acc = _rms(acc, nfin_ref[...])
    o_ref[...] = acc


def _mix_ffn(layer, x, yg, on, nm, nf, nfin, w):
    final = layer == DEPTH - 1
    n = x.shape[0]
    tm = min(n, 512)
    small = lambda i: (0, 0)
    mine = lambda i: (layer, 0, 0)
    in_specs = [
        pl.BlockSpec((tm, D_MODEL), lambda i: (i, 0)),
        pl.BlockSpec((tm, DA), lambda i: (i, 0)),
        pl.BlockSpec((tm, DB), lambda i: (i, 0)),
        _resident((1, D_MODEL), small),
        _resident((None, D_MODEL, GATE_COLS), mine),
        _resident((None, DA, D_MODEL), mine),
        _resident((None, DB, D_MODEL), mine),
        _resident((None, D_MODEL, D_MODEL), mine),
        _resident((1, D_MODEL), small),
        _resident((None, D_MODEL, D_FF), mine),
        _resident((None, D_FF, D_MODEL), mine),
    ]
    args = [x, yg, on, nm, w["wg"], w["wa"], w["wb"], w["wo"], nf, w["up"], w["dn"]]
    if final:
        in_specs.append(_resident((1, D_MODEL), small))
        args.append(nfin)
    return pl.pallas_call(
        functools.partial(_mix_ffn_kernel, final),
        grid=(n // tm,),
        in_specs=in_specs,
        out_specs=pl.BlockSpec((tm, D_MODEL), lambda i: (i, 0)),
        out_shape=jax.ShapeDtypeStruct((n, D_MODEL), F32),
        compiler_params=pltpu.CompilerParams(dimension_semantics=("arbitrary",), vmem_limit_bytes=VMEM_LIMIT),
        name="mix_ffn_final" if final else "mix_ffn",
    )(*args)


def _pad_rows(w, rows, at=0):
    out = jnp.zeros((rows, w.shape[1]), w.dtype)
    return out.at[at:at + w.shape[0]].set(w)


def _prep_weights(p):
    w_in = p["w_in"]
    pad = ((0, 0), (0, 0), (0, GROUP_COLS - RWKV_COLS))
    return dict(
        w_rw=jnp.pad(w_in[:, :, :RWKV_COLS], pad).astype(BF16),
        w_hg=w_in[:, :, RWKV_COLS:RWKV_COLS + HGRN_COLS].astype(BF16),
        wg=w_in[:, :, RWKV_COLS + HGRN_COLS:].astype(BF16),
        wa=p["w_out_a"].astype(BF16), wb=p["w_out_b"].astype(BF16), wo=p["w_out"].astype(BF16),
        up=p["w_ffn_up"].astype(BF16), dn=p["w_ffn_down"].astype(BF16),
    )


def _prep_layer(l, p):
    mu_p = jnp.zeros((1, GROUP_COLS), F32).at[0, :RWKV_COLS].set(p["rwkv_mu"][l])
    v0 = p["rwkv_v0"][l - 1] if l > 0 else jnp.zeros((DA,), F32)
    pv = jnp.stack([p["rwkv_w0"][l], p["rwkv_a0"][l], p["rwkv_k_k"][l], p["rwkv_k_a"][l],
                    p["rwkv_r_k"][l].reshape(DA), p["rwkv_ln_w"][l], p["rwkv_ln_b"][l], v0])
    out = dict(
        mu_p=mu_p, pv=pv,
        w2_p=_pad_rows(p["rwkv_w2"][l], LANES, 0).astype(BF16),
        a2_p=_pad_rows(p["rwkv_a2"][l], LANES, W_LORA).astype(BF16),
        g2_p=_pad_rows(p["rwkv_g2"][l], LORA_G_PAD, 0).astype(BF16),
        vw1_p=None, vw2_p=None,
        norm_mix=p["norm_mix"][l].reshape(1, D_MODEL),
        hgrn_norm_w=p["hgrn_norm_w"][l].reshape(1, DB),
        nf=p["norm_ffn"][l].reshape(1, D_MODEL),
    )
    if l > 0:
        w1 = p["rwkv_vres_w1"][l - 1]
        out["vw1_p"] = jnp.zeros((DA, LANES), F32).at[:, :V_LORA].set(w1).astype(BF16)
        out["vw2_p"] = _pad_rows(p["rwkv_vres_w2"][l - 1], LANES, 0).astype(BF16)
    return out


def _seq_tile(bsz, frames, step_rows):
    bt = min(bsz, max(1, step_rows // frames))
    assert bsz % bt == 0, (bsz, bt)
    return bt


def _trunk(x, state_shift, state_rwkv, state_hgrn, layers, weights, p):
    bsz, t_len, _ = x.shape
    c_len = min(t_len, RWKV_CHUNK)
    nch = min(t_len // c_len, RWKV_CHUNKS)
    tb = min(t_len, HGRN_TILE)
    assert t_len % (nch * c_len) == 0 and t_len % tb == 0 and tb % PAIR_LEN == 0
    nfin = p["norm_final"].reshape(1, D_MODEL)
    vfirst = None
    shifts, rwkv_states, hgrn_states = [], [], []
    for l in range(DEPTH):
        lp = layers[l]
        shift0 = jnp.zeros((bsz, 1, GROUP_COLS), F32).at[:, 0, :RWKV_COLS].set(state_shift[l])
        res = _rwkv(l, x, shift0, state_rwkv, vfirst, lp["norm_mix"], weights["w_rw"], lp["mu_p"], lp["pv"],
                    lp["w2_p"], lp["a2_p"], lp["g2_p"], lp["vw1_p"], lp["vw2_p"], c_len, nch,
                    _seq_tile(bsz, nch * c_len, RWKV_STEP_ROWS))
        if l == 0:
            yg, shift_out, s_rwkv, vfirst = res
        else:
            yg, shift_out, s_rwkv = res
        on, s_hgrn = _hgrn(l, x, lp["norm_mix"], weights["w_hg"], p["hgrn_lb_logits"], lp["hgrn_norm_w"],
                           state_hgrn, tb, _seq_tile(bsz, tb, HGRN_STEP_ROWS))
        n = bsz * t_len
        x = _mix_ffn(l, x.reshape(n, D_MODEL), yg.reshape(n, DA), on.reshape(n, DB), lp["norm_mix"], lp["nf"],
                     nfin, weights).reshape(bsz, t_len, D_MODEL)
        shifts.append(shift_out[:, 0, :RWKV_COLS])
        rwkv_states.append(s_rwkv)
        hgrn_states.append(s_hgrn)
    return x, jnp.stack(shifts), jnp.stack(rwkv_states), jnp.stack(hgrn_states)


def kernel(x_prompt, x_sample, state_shift, state_rwkv, state_hgrn, norm_mix, w_in, rwkv_mu, rwkv_w0, rwkv_w2, rwkv_a0, rwkv_a2, rwkv_g2, rwkv_v0, rwkv_vres_w1, rwkv_vres_w2, rwkv_k_k, rwkv_k_a, rwkv_r_k, rwkv_ln_w, rwkv_ln_b, hgrn_lb_logits, hgrn_norm_w, w_out_a, w_out_b, w_out, norm_ffn, w_ffn_up, w_ffn_down, norm_final):
    p = dict(norm_mix=norm_mix, w_in=w_in, rwkv_mu=rwkv_mu, rwkv_w0=rwkv_w0, rwkv_w2=rwkv_w2, rwkv_a0=rwkv_a0,
             rwkv_a2=rwkv_a2, rwkv_g2=rwkv_g2, rwkv_v0=rwkv_v0, rwkv_vres_w1=rwkv_vres_w1,
             rwkv_vres_w2=rwkv_vres_w2, rwkv_k_k=rwkv_k_k, rwkv_k_a=rwkv_k_a, rwkv_r_k=rwkv_r_k,
             rwkv_ln_w=rwkv_ln_w, rwkv_ln_b=rwkv_ln_b, hgrn_lb_logits=hgrn_lb_logits, hgrn_norm_w=hgrn_norm_w,
             w_out_a=w_out_a, w_out_b=w_out_b, w_out=w_out, norm_ffn=norm_ffn, w_ffn_up=w_ffn_up,
             w_ffn_down=w_ffn_down, norm_final=norm_final)
    layers = [_prep_layer(l, p) for l in range(DEPTH)]
    weights = _prep_weights(p)
    bp = x_prompt.shape[0]
    dt = x_prompt.dtype
    zero_shift = jnp.zeros((DEPTH, bp, RWKV_COLS), dt)
    zero_rwkv = jnp.zeros((DEPTH, bp, A_HEADS, A_HEAD_DIM, A_HEAD_DIM), dt)
    zero_hgrn = jnp.zeros((DEPTH, bp, B_HEADS, B_KEY_DIM, B_VAL_DIM), dt)
    y_prompt, shift_p, rwkv_p, hgrn_p = _trunk(x_prompt, zero_shift, zero_rwkv, zero_hgrn, layers, weights, p)
    y_sample, shift_s, rwkv_s, hgrn_s = _trunk(x_sample, state_shift, state_rwkv, state_hgrn, layers, weights, p)
    return (y_prompt, y_sample, shift_p, rwkv_p, hgrn_p, shift_s, rwkv_s, hgrn_s)
```

```python
import functools
import math

import jax
import jax.numpy as jnp
from jax import lax
from jax.experimental import pallas as pl
from jax.experimental.pallas import tpu as pltpu

F32 = jnp.float32
BF16 = jnp.bfloat16

D_MODEL = 1024
DEPTH = 2
A_HEADS = 8
A_HEAD_DIM = 64
DA = A_HEADS * A_HEAD_DIM
W_LORA = 64
A_LORA = 64
V_LORA = 32
G_LORA = 160
RWKV_COLS = 3 * DA + W_LORA + A_LORA + G_LORA
B_HEADS = 4
B_KEY_DIM = 128
B_VAL_DIM = 128
BK = B_HEADS * B_KEY_DIM
DB = B_HEADS * B_VAL_DIM
HGRN_COLS = 2 * BK + 2 * DB
GATE_COLS = 2 * D_MODEL
D_FF = 4 * D_MODEL
HGRN_BLOCK = 16
RMS_EPS = 1e-6
GN_EPS = 64e-5

LANES = 128
GROUP_COLS = 2048
LORA_WA = 3 * DA
LORA_G = LORA_WA + W_LORA + A_LORA
LORA_G_PAD = 256
PAIR = 2 * A_HEAD_DIM
N_PAIRS = A_HEADS // 2
PAIR_LEN = 2 * HGRN_BLOCK
VMEM_LIMIT = 56 * 1024 * 1024

RWKV_CHUNK = 64
RWKV_CHUNKS = 4
RWKV_STEP_ROWS = 512
CUMSUM_ROWS = 256
AHEAD_ROWS = 256
AHEAD_COLS = 256
HGRN_TILE = 256
HGRN_STEP_ROWS = 512
SCORE_ROWS = 128


def _dot(a, b):
    return jnp.dot(a.astype(BF16), b.astype(BF16), preferred_element_type=F32)


def _dot_nt(a, b):
    return lax.dot_general(a.astype(BF16), b.astype(BF16), (((1,), (1,)), ((), ())), preferred_element_type=F32)


def _dot_tn(a, b):
    return lax.dot_general(a.astype(BF16), b.astype(BF16), (((0,), (0,)), ((), ())), preferred_element_type=F32)


def _split3(x):
    hi = x.astype(BF16)
    r1 = x - hi.astype(F32)
    mid = r1.astype(BF16)
    return hi, mid, (r1 - mid.astype(F32)).astype(BF16)


def _dot_01(m01, x):
    return functools.reduce(lambda a, b: a + b, [jnp.dot(m01, t, preferred_element_type=F32) for t in _split3(x)])


def _dot_x01(x, m01):
    return functools.reduce(lambda a, b: a + b, [jnp.dot(t, m01, preferred_element_type=F32) for t in _split3(x)])


def _rms(x, w):
    return x * lax.rsqrt(jnp.mean(x * x, axis=-1, keepdims=True) + RMS_EPS) * w


def _sigmoid(x):
    return 1.0 / (1.0 + jnp.exp(-x))


def _log2(n):
    assert n & (n - 1) == 0, n
    return n.bit_length() - 1


def _resident(shape, index_map):
    return pl.BlockSpec(shape, index_map, pipeline_mode=pl.Buffered(1))


_DONE = object()


def _round_robin(*gens):
    live = list(gens)
    while live:
        for gen in list(live):
            if next(gen, _DONE) is _DONE:
                live.remove(gen)
            else:
                yield


def _rwkv_part(layer, bt, nch, c_len, c, h, refs, ahead=None):
    if ahead is None:
        yield from _rwkv_body(layer, bt, nch, c_len, c, h, refs, None)
        return
    h_next, rw_ref = ahead
    w_ref = refs[2 if layer == 0 else 3]
    rows = bt * nch * c_len
    cur = lax.rem(c, 2)

    def project_ahead():
        for c0 in range(0, GROUP_COLS, AHEAD_COLS):
            for r0 in range(0, rows, AHEAD_ROWS):
                rw_ref[1 - cur, r0:r0 + AHEAD_ROWS, c0:c0 + AHEAD_COLS] = jnp.dot(
                    h_next[r0:r0 + AHEAD_ROWS], w_ref[:, c0:c0 + AHEAD_COLS], preferred_element_type=F32)
                yield

    pieces = project_ahead()
    n_pieces = (GROUP_COLS // AHEAD_COLS) * (rows // AHEAD_ROWS)
    quota = {"prep": AHEAD_PER_PREP_GROUP, "dense": 0,
             "tail": -(-(n_pieces - AHEAD_PER_PREP_GROUP * N_PREP_GROUPS) // N_TAIL_GROUPS)}
    for phase in _rwkv_body(layer, bt, nch, c_len, c, None, refs, rw_ref.at[cur]):
        for _ in range(quota[phase]):
            next(pieces, None)
        yield
    for _ in pieces:
        yield


N_PREP_GROUPS = 7
N_TAIL_GROUPS = 5
AHEAD_PER_PREP_GROUP = 2


def _rwkv_body(layer, bt, nch, c_len, c, h, refs, rw_now):
    if layer == 0:
        (shift_ref, s0_ref, w_ref, mu_ref, pv_ref, w2_ref, a2_ref, g2_ref,
         yg_ref, shift_out_ref, s_out_ref, vfirst_out_ref, xs_ref, s_ref) = refs
    else:
        (shift_ref, s0_ref, vfirst_ref, w_ref, mu_ref, pv_ref, w2_ref, a2_ref, g2_ref, vw1_ref, vw2_ref,
         yg_ref, shift_out_ref, s_out_ref, xs_ref, s_ref) = refs
    frames = nch * c_len
    rows = bt * frames
    stk = 2 * c_len

    hrow = lax.broadcasted_iota(jnp.int32, (A_HEAD_DIM, PAIR), 0)
    hlane = lax.broadcasted_iota(jnp.int32, (A_HEAD_DIM, PAIR), 1)
    to_lo = jnp.where(hlane == hrow, 1.0, 0.0).astype(BF16)
    to_hi = jnp.where(hlane == hrow + A_HEAD_DIM, 1.0, 0.0).astype(BF16)

    @pl.when(c == 0)
    def _():
        for b in range(bt):
            xs_ref[b, 7:8, :] = shift_ref[b]
            for p in range(N_PAIRS):
                s_ref[b, p] = jnp.concatenate(
                    [_dot_x01(s0_ref[b, 2 * p], to_lo), _dot_x01(s0_ref[b, 2 * p + 1], to_hi)], axis=0)

    def project(c0, c1):
        if rw_now is None:
            rw = jnp.dot(h, w_ref[:, c0:c1], preferred_element_type=F32)
        else:
            rw = rw_now[:, c0:c1]
        prevs = []
        for b in range(bt):
            rw_b = rw[b * frames:(b + 1) * frames]
            xs_ref[b, 8:8 + frames, c0:c1] = rw_b
            prevs.append(xs_ref[b, 7:7 + frames, c0:c1])
            last = rw_b[frames - 1:frames, :]
            xs_ref[b, 7:8, c0:c1] = last
            shift_out_ref[b, :, c0:c1] = last
        return rw + (jnp.concatenate(prevs, axis=0) - rw) * mu_ref[:, c0:c1]

    w0 = pv_ref[0:1, :]
    a0 = pv_ref[1:2, :]
    k_k = pv_ref[2:3, :]
    k_a = pv_ref[3:4, :]
    r_k = pv_ref[4:5, :]
    ln_w = pv_ref[5:6, :]
    ln_b = pv_ref[6:7, :]

    cs_rows = min(rows, CUMSUM_ROWS)
    row = lax.broadcasted_iota(jnp.int32, (cs_rows, cs_rows), 0)
    col = lax.broadcasted_iota(jnp.int32, (cs_rows, cs_rows), 1)
    chunk_shift = _log2(c_len)
    causal01 = jnp.where(jnp.logical_and(
        lax.shift_right_logical(row, chunk_shift) == lax.shift_right_logical(col, chunk_shift), row >= col),
        1.0, 0.0).astype(BF16)

    lora = project(LORA_WA, GROUP_COLS)
    yield "prep"
    k = project(DA, 2 * DA)
    yield "prep"
    wa = lora[:, 0:LORA_G - LORA_WA]
    gl = lora[:, LORA_G - LORA_WA:LORA_G - LORA_WA + LORA_G_PAD]
    w_raw = w0 + _dot(jnp.tanh(wa), w2_ref[...])
    a_gate = _sigmoid(a0 + _dot(wa, a2_ref[...]))
    g = _dot(_sigmoid(gl), g2_ref[...])
    yield "prep"
    r = project(0, DA)
    yield "prep"
    lw = (-math.exp(-0.5)) * _sigmoid(w_raw)
    g_cum = jnp.concatenate([_dot_01(causal01, lw[i:i + cs_rows]) for i in range(0, rows, cs_rows)],
                            axis=0)
    yield "prep"
    v = project(2 * DA, 3 * DA)
    yield "prep"
    if layer == 0:
        for b in range(bt):
            vfirst_out_ref[b] = v[b * frames:(b + 1) * frames]
    else:
        v0 = pv_ref[7:8, :]
        vg = _sigmoid(v0 + _dot(_dot(v, vw1_ref[...]), vw2_ref[...]))
        vfirst = jnp.concatenate([vfirst_ref[b] for b in range(bt)], axis=0)
        v = v + (vfirst - v) * vg

    lane = lax.broadcasted_iota(jnp.int32, (PAIR, PAIR), 1)
    sub = lax.broadcasted_iota(jnp.int32, (PAIR, PAIR), 0)
    head_ones = jnp.where((lane < A_HEAD_DIM) == (sub < A_HEAD_DIM), 1.0, 0.0).astype(BF16)

    def seg_sum(x):
        return jnp.concatenate(
            [jnp.dot(x[:, p * PAIR:(p + 1) * PAIR].astype(BF16), head_ones, preferred_element_type=F32)
             for p in range(N_PAIRS)], axis=1)

    kk = k * k_k
    kk = kk * lax.rsqrt(jnp.maximum(seg_sum(kk * kk), 1e-24))
    yield "prep"
    k_h = k * (1.0 + (a_gate - 1.0) * k_a)
    a_vec = -kk
    b_vec = kk * a_gate

    g_end = jnp.concatenate(
        [jnp.broadcast_to(g_cum[(q + 1) * c_len - 1:(q + 1) * c_len, :], (c_len, DA)) for q in range(bt * nch)],
        axis=0)
    e_pos = jnp.exp(g_cum)
    e_neg = jnp.exp(-g_cum)
    e_end = jnp.exp(g_end - g_cum)
    rq = r * e_pos
    aq = a_vec * jnp.exp(g_cum - lw)
    kn = k_h * e_neg
    bn = b_vec * e_neg
    k_end = k_h * e_end
    b_end = b_vec * e_end
    decay_end = jnp.exp(g_end)

    srow = lax.broadcasted_iota(jnp.int32, (stk, PAIR), 0)
    slane = lax.broadcasted_iota(jnp.int32, (stk, PAIR), 1)
    own_lanes = (srow < c_len) == (slane < A_HEAD_DIM)

    def stack(x):
        return jnp.where(own_lanes, jnp.concatenate([x, x], axis=0), 0.0).astype(BF16)

    mrow = lax.broadcasted_iota(jnp.int32, (stk, stk), 0)
    mcol = lax.broadcasted_iota(jnp.int32, (stk, stk), 1)
    strict = (mrow & (c_len - 1)) > (mcol & (c_len - 1))
    lower = (mrow & (c_len - 1)) >= (mcol & (c_len - 1))
    eye = jnp.where(mrow == mcol, 1.0, 0.0).astype(F32)

    def cut(x, u):
        b, ch, p = u
        r0 = (b * nch + ch) * c_len
        return x[r0:r0 + c_len, p * PAIR:(p + 1) * PAIR]

    lhs, v_st, a_ak_v, a_rk, a_rb, t_inv, y_parts = {}, {}, {}, {}, {}, {}, {}

    def scores_and_inverse(units):
        power = {}
        for u in units:
            lhs[u] = jnp.concatenate([stack(cut(aq, u)), stack(cut(rq, u))], axis=0)
            v_st[u] = stack(cut(v, u))
            sc_b = _dot_nt(lhs[u], stack(cut(bn, u)))
            sc_k = _dot_nt(lhs[u], stack(cut(kn, u)))
            power[u] = jnp.where(strict, sc_b[:stk], 0.0)
            a_rb[u] = jnp.where(lower, sc_b[stk:], 0.0)
            a_rk[u] = jnp.where(lower, sc_k[stk:], 0.0)
            a_ak_v[u] = jnp.where(strict, sc_k[:stk], 0.0)
        yield
        for u in units:
            t_inv[u] = eye + power[u]
            a_ak_v[u] = _dot(a_ak_v[u], v_st[u])
            power[u] = _dot(power[u], power[u])
        yield
        for lvl in range(1, _log2(c_len)):
            new_power = {}
            for u in units:
                t_inv[u] = t_inv[u] + _dot(power[u], t_inv[u])
                if lvl < _log2(c_len) - 1:
                    new_power[u] = _dot(power[u], power[u])
            power = new_power
            yield

    def apply_state(units):
        from_state = {u: _dot_nt(lhs[u], s_ref[u[0], u[2]]) for u in units}
        yield
        corr = {u: _dot(t_inv[u], from_state[u][:stk] + a_ak_v[u]) for u in units}
        yield
        for u in units:
            vu = jnp.concatenate([v_st[u], corr[u].astype(BF16)], axis=0)
            if stk % LANES == 0:
                y_st = from_state[u][stk:] + _dot(jnp.concatenate([a_rk[u], a_rb[u]], axis=1), vu)
            else:
                y_st = from_state[u][stk:] + _dot(a_rk[u], v_st[u]) + _dot(a_rb[u], corr[u])
            y_parts[u] = y_st[:c_len] + y_st[c_len:]
            ends = jnp.concatenate([stack(cut(k_end, u)), stack(cut(b_end, u))], axis=0)
            s_ref[u[0], u[2]] = s_ref[u[0], u[2]] * cut(decay_end, u)[0:1, :] + _dot_tn(vu, ends)
        yield

    chunk_units = [[(b, ch, p) for b in range(bt) for p in range(N_PAIRS)] for ch in range(nch)]
    for _ in scores_and_inverse(chunk_units[0]):
        yield "dense"
    for ch in range(nch):
        if ch + 1 < nch:
            for _ in _round_robin(scores_and_inverse(chunk_units[ch + 1]), apply_state(chunk_units[ch])):
                yield "dense"
        else:
            for _ in apply_state(chunk_units[ch]):
                yield "tail"

    y = jnp.concatenate(
        [jnp.concatenate([y_parts[(b, ch, p)] for p in range(N_PAIRS)], axis=1)
         for b in range(bt) for ch in range(nch)], axis=0)
    inv_n = 1.0 / A_HEAD_DIM
    mean = seg_sum(y) * inv_n
    bonus = seg_sum(r * k_h * r_k)
    yield "tail"
    dev = y - mean
    var = seg_sum(dev * dev) * inv_n
    yield "tail"
    y = dev * lax.rsqrt(var + GN_EPS) * ln_w + ln_b
    y = ((y + bonus * v) * g).astype(BF16)
    for b in range(bt):
        yg_ref[b] = y[b * frames:(b + 1) * frames]

    @pl.when(c == pl.num_programs(1) - 1)
    def _():
        prow = lax.broadcasted_iota(jnp.int32, (PAIR, A_HEAD_DIM), 0)
        pcol = lax.broadcasted_iota(jnp.int32, (PAIR, A_HEAD_DIM), 1)
        from_lo = jnp.where(prow == pcol, 1.0, 0.0).astype(BF16)
        from_hi = jnp.where(prow == pcol + A_HEAD_DIM, 1.0, 0.0).astype(BF16)
        for b in range(bt):
            for p in range(N_PAIRS):
                s_pair = s_ref[b, p]
                s_out_ref[b, 2 * p] = _dot_x01(s_pair[:A_HEAD_DIM], from_lo)
                s_out_ref[b, 2 * p + 1] = _dot_x01(s_pair[A_HEAD_DIM:], from_hi)


def _hgrn_part(layer, bt, tb, c, h, refs):
    (w_ref, lg_ref, nw_ref, s0_ref, on_ref, s_out_ref, st_ref) = refs
    n_blk = tb // HGRN_BLOCK

    def project(c0):
        return [jnp.dot(h[b * tb:(b + 1) * tb], w_ref[:, c0:c0 + BK], preferred_element_type=F32)
                for b in range(bt)]

    fz_all = project(BK)
    yield
    q_all = project(0)
    yield

    @pl.when(c == 0)
    def _():
        for b in range(bt):
            for hd in range(B_HEADS):
                st_ref[b, hd] = s0_ref[b, hd].T

    lg = lg_ref[...]
    lrows = [lg[i:i + 1, :] for i in range(DEPTH)]
    mx = functools.reduce(jnp.maximum, lrows)
    ex = [jnp.exp(x - mx) for x in lrows]
    den = functools.reduce(lambda a, b: a + b, ex)
    sm = [e / den for e in ex]
    lb = functools.reduce(lambda a, b: a + b, sm[:layer + 1]) - sm[0]

    row = lax.broadcasted_iota(jnp.int32, (tb, tb), 0)
    col = lax.broadcasted_iota(jnp.int32, (tb, tb), 1)
    blk_shift = _log2(HGRN_BLOCK)
    causal = jnp.logical_and(lax.shift_right_logical(row, blk_shift) == lax.shift_right_logical(col, blk_shift),
                             row >= col)
    causal01 = jnp.where(causal, 1.0, 0.0).astype(BF16)

    f_all = [lb + (1.0 - lb) * _sigmoid(fz_all[b]) for b in range(bt)]
    g_cum_all = [_dot_01(causal01, jnp.log(f_all[b])) for b in range(bt)]
    yield
    iv = project(2 * BK)
    yield

    frow = lax.broadcasted_iota(jnp.int32, (tb, BK), 0)
    second = (lax.shift_right_logical(frow, blk_shift) & 1) == 1
    zeros_blk = jnp.zeros((HGRN_BLOCK, BK), F32)
    sr = min(tb, SCORE_ROWS)
    srow = lax.broadcasted_iota(jnp.int32, (sr, sr), 0)
    scol = lax.broadcasted_iota(jnp.int32, (sr, sr), 1)

    def split_mask(hl):
        grp = _log2(2 * hl)
        return jnp.logical_and(
            lax.shift_right_logical(srow, grp) == lax.shift_right_logical(scol, grp),
            jnp.logical_and((srow & (2 * hl - 1)) >= hl, (scol & (2 * hl - 1)) < hl))

    pair_mask = split_mask(HGRN_BLOCK)

    levels = [HGRN_BLOCK >> (i + 1) for i in range(blk_shift)]
    pos_in = {hl: frow & (2 * hl - 1) for hl in levels}
    level_mask = {hl: split_mask(hl) for hl in levels}
    toward_anchor = {hl: jnp.where(pos_in[hl] >= hl, 1.0, -1.0).astype(F32) for hl in levels}

    def anchor(g, hl):
        if 2 * hl >= 8:
            g3 = g.reshape(tb // (2 * hl), 2 * hl, BK)
            return jnp.broadcast_to(g3[:, hl - 1:hl, :], g3.shape).reshape(tb, BK)
        out = g
        for pos in range(2 * hl):
            if pos != hl - 1:
                out = jnp.where(pos_in[hl] == pos, pltpu.roll(g, (pos - (hl - 1)) % tb, 0), out)
        return out

    q_lvl = {hl: [] for hl in levels}
    k_lvl = {hl: [] for hl in levels}
    own = []
    qg, qg_pair, k_end, k_end_pair, decay_pair = [], [], [], [], []
    for b in range(bt):
        q = q_all[b]
        kx = 1.0 - f_all[b]
        g_cum = g_cum_all[b]
        qs = q * _sigmoid(q)
        own.append(qs * kx)
        for hl in levels:
            partial = jnp.exp((g_cum - anchor(g_cum, hl)) * toward_anchor[hl])
            q_lvl[hl].append((qs * partial).astype(BF16))
            k_lvl[hl].append((kx * partial).astype(BF16))
        g_end = jnp.concatenate(
            [jnp.broadcast_to(g_cum[(j + 1) * HGRN_BLOCK - 1:(j + 1) * HGRN_BLOCK, :], (HGRN_BLOCK, BK))
             for j in range(n_blk)], axis=0)
        decay = jnp.exp(g_end)
        decay_prev = jnp.concatenate([zeros_blk, decay[:-HGRN_BLOCK]], axis=0)
        decay_next = jnp.concatenate([decay[HGRN_BLOCK:], zeros_blk], axis=0)
        q_dec = qs * jnp.exp(g_cum)
        k_to_end = kx * jnp.exp(g_end - g_cum)
        qg.append(q_dec.astype(BF16))
        qg_pair.append((q_dec * jnp.where(second, decay_prev, 1.0)).astype(BF16))
        k_end.append(k_to_end.astype(BF16))
        k_end_pair.append((k_to_end * jnp.where(second, 1.0, decay_next)).astype(BF16))
        decay_pair.append(decay * decay_next)

    units = [(b, hd) for b in range(bt) for hd in range(B_HEADS)]
    n_pair = tb // PAIR_LEN

    def cut(x, u):
        return x[u[0]][:, u[1] * B_KEY_DIM:(u[1] + 1) * B_KEY_DIM]

    def pair(j):
        return slice(j * PAIR_LEN, (j + 1) * PAIR_LEN)

    def tile(i):
        return slice(i * sr, (i + 1) * sr)

    def scores(mask, qx, kx_):
        return {(u, i): jnp.where(mask, _dot_nt(cut(qx, u)[tile(i)], cut(kx_, u)[tile(i)]), 0.0)
                for u in units for i in range(tb // sr)}

    att = scores(pair_mask, qg, k_end)
    yield
    for hl in levels:
        level = scores(level_mask[hl], q_lvl[hl], k_lvl[hl])
        att = {key: att[key] + level[key] for key in att}
        yield
    upd = {}
    for j in range(n_pair):
        for u in units:
            upd[(u, j)] = _dot_tn(cut(iv, u)[pair(j)], cut(k_end_pair, u)[pair(j)])
        yield
    starts = {}
    for u in units:
        s = st_ref[u[0], u[1]]
        dec = cut(decay_pair, u)
        for j in range(n_pair):
            starts[(u, j)] = s.astype(BF16)
            s = s * dec[j * PAIR_LEN:j * PAIR_LEN + 1, :] + upd[(u, j)]
        st_ref[u[0], u[1]] = s
    og = project(2 * BK + DB)
    yield
    o_acc = {u: jnp.concatenate([_dot(att[(u, i)], cut(iv, u)[tile(i)]) for i in range(tb // sr)], axis=0)
             for u in units}
    yield
    inter = {}
    for j in range(n_pair):
        for u in units:
            inter[(u, j)] = _dot_nt(cut(qg_pair, u)[pair(j)], starts[(u, j)])
        yield

    for b in range(bt):
        outs = []
        for hd in range(B_HEADS):
            u = (b, hd)
            o_h = o_acc[u] + jnp.concatenate([inter[(u, j)] for j in range(n_pair)], axis=0)
            o_h = o_h + jnp.sum(cut(own, u), axis=-1, keepdims=True) * cut(iv, u)
            outs.append(o_h * lax.rsqrt(jnp.mean(o_h * o_h, axis=-1, keepdims=True) + RMS_EPS))
        o = jnp.concatenate(outs, axis=1)
        on_ref[b] = (o * nw_ref[...] * (og[b] * _sigmoid(og[b]))).astype(BF16)

    @pl.when(c == pl.num_programs(1) - 1)
    def _():
        for b in range(bt):
            for hd in range(B_HEADS):
                s_out_ref[b, hd] = st_ref[b, hd].T


def _drive(gen):
    for _ in gen:
        pass


def _normalised(x_ref, nm_ref, bt):
    return jnp.concatenate([_rms(x_ref[b], nm_ref[...]).astype(BF16) for b in range(bt)], axis=0)


def _rwkv_kernel(layer, bt, nch, c_len, x_ref, nm_ref, *refs):
    _drive(_rwkv_part(layer, bt, nch, c_len, pl.program_id(1), _normalised(x_ref, nm_ref, bt), refs))


def _rwkv_ahead_kernel(layer, bt, nch, c_len, x_ref, xn_ref, nm_ref, *refs):
    *refs, rw_ref = refs
    w_ref = refs[2 if layer == 0 else 3]
    c = pl.program_id(1)

    @pl.when(c == 0)
    def _():
        h = _normalised(x_ref, nm_ref, bt)
        for c0 in range(0, GROUP_COLS, AHEAD_COLS):
            rw_ref[0, :, c0:c0 + AHEAD_COLS] = jnp.dot(h, w_ref[:, c0:c0 + AHEAD_COLS], preferred_element_type=F32)

    _drive(_rwkv_part(layer, bt, nch, c_len, c, None, refs, ahead=(_normalised(xn_ref, nm_ref, bt), rw_ref)))


def _hgrn_kernel(layer, bt, tb, x_ref, nm_ref, *refs):
    _drive(_hgrn_part(layer, bt, tb, pl.program_id(1), _normalised(x_ref, nm_ref, bt), refs))


def _rwkv(layer, x, shift0, s_all, vfirst, nm, w_rw_all, mu_p, pv, w2_p, a2_p, g2_p, vw1_p, vw2_p, c_len, nch, bt):
    bsz, t_len, _ = x.shape
    frames = nch * c_len
    const2 = lambda b, c: (0, 0)
    step = lambda b, c: (b, c, 0)
    per_seq3 = lambda b, c: (b, 0, 0)
    per_seq4 = lambda b, c: (b, 0, 0, 0)
    n_steps = t_len // frames
    ahead = n_steps > 1
    in_specs = [pl.BlockSpec((bt, frames, D_MODEL), step)]
    args = [x]
    if ahead:
        in_specs.append(pl.BlockSpec((bt, frames, D_MODEL), lambda b, c: (b, jnp.minimum(c + 1, n_steps - 1), 0)))
        args.append(x)
    in_specs += [
        pl.BlockSpec((1, D_MODEL), const2),
        pl.BlockSpec((bt, 1, GROUP_COLS), per_seq3),
        pl.BlockSpec((None, bt, A_HEADS, A_HEAD_DIM, A_HEAD_DIM), lambda b, c: (layer, b, 0, 0, 0)),
    ]
    args += [nm, shift0, s_all]
    if layer > 0:
        in_specs.append(pl.BlockSpec((bt, frames, DA), step))
        args.append(vfirst)
    in_specs += [
        _resident((None, D_MODEL, GROUP_COLS), lambda b, c: (layer, 0, 0)),
        pl.BlockSpec((1, GROUP_COLS), const2),
        pl.BlockSpec((8, DA), const2),
        pl.BlockSpec((LANES, DA), const2),
        pl.BlockSpec((LANES, DA), const2),
        pl.BlockSpec((LORA_G_PAD, DA), const2),
    ]
    args += [w_rw_all, mu_p, pv, w2_p, a2_p, g2_p]
    if layer > 0:
        in_specs += [pl.BlockSpec((DA, LANES), const2), pl.BlockSpec((LANES, DA), const2)]
        args += [vw1_p, vw2_p]
    out_specs = [
        pl.BlockSpec((bt, frames, DA), step),
        pl.BlockSpec((bt, 1, GROUP_COLS), per_seq3),
        pl.BlockSpec((bt, A_HEADS, A_HEAD_DIM, A_HEAD_DIM), per_seq4),
    ]
    out_shape = [
        jax.ShapeDtypeStruct((bsz, t_len, DA), BF16),
        jax.ShapeDtypeStruct((bsz, 1, GROUP_COLS), F32),
        jax.ShapeDtypeStruct((bsz, A_HEADS, A_HEAD_DIM, A_HEAD_DIM), F32),
    ]
    if layer == 0:
        out_specs.append(pl.BlockSpec((bt, frames, DA), step))
        out_shape.append(jax.ShapeDtypeStruct((bsz, t_len, DA), F32))
    scratch = [pltpu.VMEM((bt, frames + 8, GROUP_COLS), F32), pltpu.VMEM((bt, N_PAIRS, PAIR, PAIR), F32)]
    if ahead:
        scratch.append(pltpu.VMEM((2, bt * frames, GROUP_COLS), F32))
    return pl.pallas_call(
        functools.partial(_rwkv_ahead_kernel if ahead else _rwkv_kernel, layer, bt, nch, c_len),
        grid=(bsz // bt, n_steps),
        in_specs=in_specs,
        out_specs=out_specs,
        out_shape=out_shape,
        scratch_shapes=scratch,
        compiler_params=pltpu.CompilerParams(
            dimension_semantics=("arbitrary", "arbitrary"), vmem_limit_bytes=VMEM_LIMIT),
        name=f"rwkv{layer}",
    )(*args)


def _hgrn(layer, x, nm, w_hg_all, lb_logits, norm_w, s_all, tb, bt):
    bsz, t_len, _ = x.shape
    const2 = lambda b, t: (0, 0)
    return pl.pallas_call(
        functools.partial(_hgrn_kernel, layer, bt, tb),
        grid=(bsz // bt, t_len // tb),
        in_specs=[
            pl.BlockSpec((bt, tb, D_MODEL), lambda b, t: (b, t, 0)),
            pl.BlockSpec((1, D_MODEL), const2),
            _resident((None, D_MODEL, HGRN_COLS), lambda b, t: (layer, 0, 0)),
            pl.BlockSpec((DEPTH, BK), const2),
            pl.BlockSpec((1, DB), const2),
            pl.BlockSpec((None, bt, B_HEADS, B_KEY_DIM, B_VAL_DIM), lambda b, t: (layer, b, 0, 0, 0)),
        ],
        out_specs=[
            pl.BlockSpec((bt, tb, DB), lambda b, t: (b, t, 0)),
            pl.BlockSpec((bt, B_HEADS, B_KEY_DIM, B_VAL_DIM), lambda b, t: (b, 0, 0, 0)),
        ],
        out_shape=[
            jax.ShapeDtypeStruct((bsz, t_len, DB), BF16),
            jax.ShapeDtypeStruct((bsz, B_HEADS, B_KEY_DIM, B_VAL_DIM), F32),
        ],
        scratch_shapes=[pltpu.VMEM((bt, B_HEADS, B_VAL_DIM, B_KEY_DIM), F32)],
        compiler_params=pltpu.CompilerParams(
            dimension_semantics=("arbitrary", "arbitrary"), vmem_limit_bytes=VMEM_LIMIT),
        name=f"hgrn{layer}",
    )(x, nm, w_hg_all, lb_logits, norm_w, s_all)


FF_CHUNK = 1024


def _mix_ffn_kernel(final, *refs):
    if final:
        (x_ref, yg_ref, on_ref, nm_ref, wg_ref, wa_ref, wb_ref, wo_ref, nf_ref, up_ref, dn_ref, nfin_ref,
         o_ref) = refs
    else:
        (x_ref, yg_ref, on_ref, nm_ref, wg_ref, wa_ref, wb_ref, wo_ref, nf_ref, up_ref, dn_ref, o_ref) = refs
    x = x_ref[...]
    gt = _sigmoid(jnp.dot(_rms(x, nm_ref[...]).astype(BF16), wg_ref[...], preferred_element_type=F32))
    ya = jnp.dot(yg_ref[...], wa_ref[...], preferred_element_type=F32)
    yb = jnp.dot(on_ref[...], wb_ref[...], preferred_element_type=F32)
    merged = gt[:, 0:D_MODEL] * ya + gt[:, D_MODEL:] * yb
    x = x + jnp.dot(merged.astype(BF16), wo_ref[...], preferred_element_type=F32)
    h = _rms(x, nf_ref[...]).astype(BF16)
    acc = x
    for cf in range(D_FF // FF_CHUNK):
        u = jnp.dot(h, up_ref[:, cf * FF_CHUNK:(cf + 1) * FF_CHUNK], preferred_element_type=F32)
        u = jnp.maximum(u, 0.0)
        acc = acc + jnp.dot((u * u).astype(BF16), dn_ref[cf * FF_CHUNK:(cf + 1) * FF_CHUNK, :],
                            preferred_element_type=F32)
    if final:
        acc = _rms(acc, nfin_ref[...])
    o_ref[...] = acc


def _mix_ffn(layer, x, yg, on, nm, nf, nfin, w):
    final = layer == DEPTH - 1
    n = x.shape[0]
    tm = min(n, 512)
    small = lambda i: (0, 0)
    mine = lambda i: (layer, 0, 0)
    in_specs = [
        pl.BlockSpec((tm, D_MODEL), lambda i: (i, 0)),
        pl.BlockSpec((tm, DA), lambda i: (i, 0)),
        pl.BlockSpec((tm, DB), lambda i: (i, 0)),
        _resident((1, D_MODEL), small),
        _resident((None, D_MODEL, GATE_COLS), mine),
        _resident((None, DA, D_MODEL), mine),
        _resident((None, DB, D_MODEL), mine),
        _resident((None, D_MODEL, D_MODEL), mine),
        _resident((1, D_MODEL), small),
        _resident((None, D_MODEL, D_FF), mine),
        _resident((None, D_FF, D_MODEL), mine),
    ]
    args = [x, yg, on, nm, w["wg"], w["wa"], w["wb"], w["wo"], nf, w["up"], w["dn"]]
    if final:
        in_specs.append(_resident((1, D_MODEL), small))
        args.append(nfin)
    return pl.pallas_call(
        functools.partial(_mix_ffn_kernel, final),
        grid=(n // tm,),
        in_specs=in_specs,
        out_specs=pl.BlockSpec((tm, D_MODEL), lambda i: (i, 0)),
        out_shape=jax.ShapeDtypeStruct((n, D_MODEL), F32),
        compiler_params=pltpu.CompilerParams(dimension_semantics=("arbitrary",), vmem_limit_bytes=VMEM_LIMIT),
        name="mix_ffn_final" if final else "mix_ffn",
    )(*args)


def _pad_rows(w, rows, at=0):
    out = jnp.zeros((rows, w.shape[1]), w.dtype)
    return out.at[at:at + w.shape[0]].set(w)


def _prep_weights(p):
    w_in = p["w_in"]
    pad = ((0, 0), (0, 0), (0, GROUP_COLS - RWKV_COLS))
    return dict(
        w_rw=jnp.pad(w_in[:, :, :RWKV_COLS], pad).astype(BF16),
        w_hg=w_in[:, :, RWKV_COLS:RWKV_COLS + HGRN_COLS].astype(BF16),
        wg=w_in[:, :, RWKV_COLS + HGRN_COLS:].astype(BF16),
        wa=p["w_out_a"].astype(BF16), wb=p["w_out_b"].astype(BF16), wo=p["w_out"].astype(BF16),
        up=p["w_ffn_up"].astype(BF16), dn=p["w_ffn_down"].astype(BF16),
    )


def _prep_layer(l, p):
    mu_p = jnp.zeros((1, GROUP_COLS), F32).at[0, :RWKV_COLS].set(p["rwkv_mu"][l])
    v0 = p["rwkv_v0"][l - 1] if l > 0 else jnp.zeros((DA,), F32)
    pv = jnp.stack([p["rwkv_w0"][l], p["rwkv_a0"][l], p["rwkv_k_k"][l], p["rwkv_k_a"][l],
                    p["rwkv_r_k"][l].reshape(DA), p["rwkv_ln_w"][l], p["rwkv_ln_b"][l], v0])
    out = dict(
        mu_p=mu_p, pv=pv,
        w2_p=_pad_rows(p["rwkv_w2"][l], LANES, 0).astype(BF16),
        a2_p=_pad_rows(p["rwkv_a2"][l], LANES, W_LORA).astype(BF16),
        g2_p=_pad_rows(p["rwkv_g2"][l], LORA_G_PAD, 0).astype(BF16),
        vw1_p=None, vw2_p=None,
        norm_mix=p["norm_mix"][l].reshape(1, D_MODEL),
        hgrn_norm_w=p["hgrn_norm_w"][l].reshape(1, DB),
        nf=p["norm_ffn"][l].reshape(1, D_MODEL),
    )
    if l > 0:
        w1 = p["rwkv_vres_w1"][l - 1]
        out["vw1_p"] = jnp.zeros((DA, LANES), F32).at[:, :V_LORA].set(w1).astype(BF16)
        out["vw2_p"] = _pad_rows(p["rwkv_vres_w2"][l - 1], LANES, 0).astype(BF16)
    return out


def _seq_tile(bsz, frames, step_rows):
    bt = min(bsz, max(1, step_rows // frames))
    assert bsz % bt == 0, (bsz, bt)
    return bt


def _trunk(x, state_shift, state_rwkv, state_hgrn, layers, weights, p):
    bsz, t_len, _ = x.shape
    c_len = min(t_len, RWKV_CHUNK)
    nch = min(t_len // c_len, RWKV_CHUNKS)
    tb = min(t_len, HGRN_TILE)
    assert t_len % (nch * c_len) == 0 and t_len % tb == 0 and tb % PAIR_LEN == 0
    nfin = p["norm_final"].reshape(1, D_MODEL)
    vfirst = None
    shifts, rwkv_states, hgrn_states = [], [], []
    for l in range(DEPTH):
        lp = layers[l]
        shift0 = jnp.zeros((bsz, 1, GROUP_COLS), F32).at[:, 0, :RWKV_COLS].set(state_shift[l])
        res = _rwkv(l, x, shift0, state_rwkv, vfirst, lp["norm_mix"], weights["w_rw"], lp["mu_p"], lp["pv"],
                    lp["w2_p"], lp["a2_p"], lp["g2_p"], lp["vw1_p"], lp["vw2_p"], c_len, nch,
                    _seq_tile(bsz, nch * c_len, RWKV_STEP_ROWS))
        if l == 0:
            yg, shift_out, s_rwkv, vfirst = res
        else:
            yg, shift_out, s_rwkv = res
        on, s_hgrn = _hgrn(l, x, lp["norm_mix"], weights["w_hg"], p["hgrn_lb_logits"], lp["hgrn_norm_w"],
                           state_hgrn, tb, _seq_tile(bsz, tb, HGRN_STEP_ROWS))
        n = bsz * t_len
        x = _mix_ffn(l, x.reshape(n, D_MODEL), yg.reshape(n, DA), on.reshape(n, DB), lp["norm_mix"], lp["nf"],
                     nfin, weights).reshape(bsz, t_len, D_MODEL)
        shifts.append(shift_out[:, 0, :RWKV_COLS])
        rwkv_states.append(s_rwkv)
        hgrn_states.append(s_hgrn)
    return x, jnp.stack(shifts), jnp.stack(rwkv_states), jnp.stack(hgrn_states)


def kernel(x_prompt, x_sample, state_shift, state_rwkv, state_hgrn, norm_mix, w_in, rwkv_mu, rwkv_w0, rwkv_w2, rwkv_a0, rwkv_a2, rwkv_g2, rwkv_v0, rwkv_vres_w1, rwkv_vres_w2, rwkv_k_k, rwkv_k_a, rwkv_r_k, rwkv_ln_w, rwkv_ln_b, hgrn_lb_logits, hgrn_norm_w, w_out_a, w_out_b, w_out, norm_ffn, w_ffn_up, w_ffn_down, norm_final):
    p = dict(norm_mix=norm_mix, w_in=w_in, rwkv_mu=rwkv_mu, rwkv_w0=rwkv_w0, rwkv_w2=rwkv_w2, rwkv_a0=rwkv_a0,
             rwkv_a2=rwkv_a2, rwkv_g2=rwkv_g2, rwkv_v0=rwkv_v0, rwkv_vres_w1=rwkv_vres_w1,
             rwkv_vres_w2=rwkv_vres_w2, rwkv_k_k=rwkv_k_k, rwkv_k_a=rwkv_k_a, rwkv_r_k=rwkv_r_k,
             rwkv_ln_w=rwkv_ln_w, rwkv_ln_b=rwkv_ln_b, hgrn_lb_logits=hgrn_lb_logits, hgrn_norm_w=hgrn_norm_w,
             w_out_a=w_out_a, w_out_b=w_out_b, w_out=w_out, norm_ffn=norm_ffn, w_ffn_up=w_ffn_up,
             w_ffn_down=w_ffn_down, norm_final=norm_final)
    layers = [_prep_layer(l, p) for l in range(DEPTH)]
    weights = _prep_weights(p)
    bp = x_prompt.shape[0]
    dt = x_prompt.dtype
    zero_shift = jnp.zeros((DEPTH, bp, RWKV_COLS), dt)
    zero_rwkv = jnp.zeros((DEPTH, bp, A_HEADS, A_HEAD_DIM, A_HEAD_DIM), dt)
    zero_hgrn = jnp.zeros((DEPTH, bp, B_HEADS, B_KEY_DIM, B_VAL_DIM), dt)
    y_prompt, shift_p, rwkv_p, hgrn_p = _trunk(x_prompt, zero_shift, zero_rwkv, zero_hgrn, layers, weights, p)
    y_sample, shift_s, rwkv_s, hgrn_s = _trunk(x_sample, state_shift, state_rwkv, state_hgrn, layers, weights, p)
    return (y_prompt, y_sample, shift_p, rwkv_p, hgrn_p, shift_s, rwkv_s, hgrn_s)
```

```python
import functools
import math

import jax
import jax.numpy as jnp
from jax import lax
from jax.experimental import pallas as pl
from jax.experimental.pallas import tpu as pltpu

F32 = jnp.float32
BF16 = jnp.bfloat16

D_MODEL = 1024
DEPTH = 2
A_HEADS = 8
A_HEAD_DIM = 64
DA = A_HEADS * A_HEAD_DIM
W_LORA = 64
A_LORA = 64
V_LORA = 32
G_LORA = 160
RWKV_COLS = 3 * DA + W_LORA + A_LORA + G_LORA
B_HEADS = 4
B_KEY_DIM = 128
B_VAL_DIM = 128
BK = B_HEADS * B_KEY_DIM
DB = B_HEADS * B_VAL_DIM
HGRN_COLS = 2 * BK + 2 * DB
GATE_COLS = 2 * D_MODEL
D_FF = 4 * D_MODEL
HGRN_BLOCK = 16
RMS_EPS = 1e-6
GN_EPS = 64e-5

LANES = 128
GROUP_COLS = 2048
LORA_WA = 3 * DA
LORA_G = LORA_WA + W_LORA + A_LORA
LORA_G_PAD = 256
PAIR = 2 * A_HEAD_DIM
N_PAIRS = A_HEADS // 2
PAIR_LEN = 2 * HGRN_BLOCK
VMEM_LIMIT = 56 * 1024 * 1024

RWKV_CHUNK = 64
RWKV_CHUNKS = 4
RWKV_STEP_ROWS = 512
RWKV_STEP_UNITS = 32
CUMSUM_ROWS = 256
HGRN_TILE = 256
HGRN_STEP_ROWS = 512
SCORE_ROWS = 128


def _dot(a, b):
    return jnp.dot(a.astype(BF16), b.astype(BF16), preferred_element_type=F32)


def _dot_nt(a, b):
    return lax.dot_general(a.astype(BF16), b.astype(BF16), (((1,), (1,)), ((), ())), preferred_element_type=F32)


def _dot_tn(a, b):
    return lax.dot_general(a.astype(BF16), b.astype(BF16), (((0,), (0,)), ((), ())), preferred_element_type=F32)


def _split3(x):
    hi = x.astype(BF16)
    r1 = x - hi.astype(F32)
    mid = r1.astype(BF16)
    return hi, mid, (r1 - mid.astype(F32)).astype(BF16)


def _dot_01(m01, x):
    return functools.reduce(lambda a, b: a + b, [jnp.dot(m01, t, preferred_element_type=F32) for t in _split3(x)])


def _dot_x01(x, m01):
    return functools.reduce(lambda a, b: a + b, [jnp.dot(t, m01, preferred_element_type=F32) for t in _split3(x)])


def _rms(x, w):
    return x * lax.rsqrt(jnp.mean(x * x, axis=-1, keepdims=True) + RMS_EPS) * w


def _sigmoid(x):
    return 1.0 / (1.0 + jnp.exp(-x))


def _log2(n):
    assert n & (n - 1) == 0, n
    return n.bit_length() - 1


def _resident(shape, index_map):
    return pl.BlockSpec(shape, index_map, pipeline_mode=pl.Buffered(1))


_DONE = object()


def _round_robin(*gens):
    live = list(gens)
    while live:
        for gen in list(live):
            if next(gen, _DONE) is _DONE:
                live.remove(gen)
            else:
                yield


def _rwkv_part(layer, bt, nch, c_len, c, h, refs):
    if layer == 0:
        (shift_ref, s0_ref, w_ref, mu_ref, pv_ref, w2_ref, a2_ref, g2_ref,
         yg_ref, shift_out_ref, s_out_ref, vfirst_out_ref, xs_ref, s_ref) = refs
    else:
        (shift_ref, s0_ref, vfirst_ref, w_ref, mu_ref, pv_ref, w2_ref, a2_ref, g2_ref, vw1_ref, vw2_ref,
         yg_ref, shift_out_ref, s_out_ref, xs_ref, s_ref) = refs
    frames = nch * c_len
    rows = bt * frames
    stk = 2 * c_len

    hrow = lax.broadcasted_iota(jnp.int32, (A_HEAD_DIM, PAIR), 0)
    hlane = lax.broadcasted_iota(jnp.int32, (A_HEAD_DIM, PAIR), 1)
    to_lo = jnp.where(hlane == hrow, 1.0, 0.0).astype(BF16)
    to_hi = jnp.where(hlane == hrow + A_HEAD_DIM, 1.0, 0.0).astype(BF16)

    @pl.when(c == 0)
    def _():
        for b in range(bt):
            xs_ref[b, 7:8, :] = shift_ref[b]
            for p in range(N_PAIRS):
                s_ref[b, p] = jnp.concatenate(
                    [_dot_x01(s0_ref[b, 2 * p], to_lo), _dot_x01(s0_ref[b, 2 * p + 1], to_hi)], axis=0)

    def project(c0, c1):
        rw = jnp.dot(h, w_ref[:, c0:c1], preferred_element_type=F32)
        prevs = []
        for b in range(bt):
            rw_b = rw[b * frames:(b + 1) * frames]
            xs_ref[b, 8:8 + frames, c0:c1] = rw_b
            prevs.append(xs_ref[b, 7:7 + frames, c0:c1])
            last = rw_b[frames - 1:frames, :]
            xs_ref[b, 7:8, c0:c1] = last
            shift_out_ref[b, :, c0:c1] = last
        return rw + (jnp.concatenate(prevs, axis=0) - rw) * mu_ref[:, c0:c1]

    w0 = pv_ref[0:1, :]
    a0 = pv_ref[1:2, :]
    k_k = pv_ref[2:3, :]
    k_a = pv_ref[3:4, :]
    r_k = pv_ref[4:5, :]
    ln_w = pv_ref[5:6, :]
    ln_b = pv_ref[6:7, :]

    cs_rows = min(rows, CUMSUM_ROWS)
    row = lax.broadcasted_iota(jnp.int32, (cs_rows, cs_rows), 0)
    col = lax.broadcasted_iota(jnp.int32, (cs_rows, cs_rows), 1)
    chunk_shift = _log2(c_len)
    causal01 = jnp.where(jnp.logical_and(
        lax.shift_right_logical(row, chunk_shift) == lax.shift_right_logical(col, chunk_shift), row >= col),
        1.0, 0.0).astype(BF16)

    lora = project(LORA_WA, GROUP_COLS)
    yield "prep"
    k = project(DA, 2 * DA)
    yield "prep"
    wa = lora[:, 0:LORA_G - LORA_WA]
    gl = lora[:, LORA_G - LORA_WA:LORA_G - LORA_WA + LORA_G_PAD]
    w_raw = w0 + _dot(jnp.tanh(wa), w2_ref[...])
    a_gate = _sigmoid(a0 + _dot(wa, a2_ref[...]))
    g = _dot(_sigmoid(gl), g2_ref[...])
    yield "prep"
    r = project(0, DA)
    yield "prep"
    lw = (-math.exp(-0.5)) * _sigmoid(w_raw)
    g_cum = jnp.concatenate([_dot_01(causal01, lw[i:i + cs_rows]) for i in range(0, rows, cs_rows)],
                            axis=0)
    yield "prep"
    v = project(2 * DA, 3 * DA)
    yield "prep"
    if layer == 0:
        for b in range(bt):
            vfirst_out_ref[b] = v[b * frames:(b + 1) * frames]
    else:
        v0 = pv_ref[7:8, :]
        vg = _sigmoid(v0 + _dot(_dot(v, vw1_ref[...]), vw2_ref[...]))
        vfirst = jnp.concatenate([vfirst_ref[b] for b in range(bt)], axis=0)
        v = v + (vfirst - v) * vg

    lane = lax.broadcasted_iota(jnp.int32, (PAIR, PAIR), 1)
    sub = lax.broadcasted_iota(jnp.int32, (PAIR, PAIR), 0)
    head_ones = jnp.where((lane < A_HEAD_DIM) == (sub < A_HEAD_DIM), 1.0, 0.0).astype(BF16)

    def seg_sum(x):
        return jnp.concatenate(
            [jnp.dot(x[:, p * PAIR:(p + 1) * PAIR].astype(BF16), head_ones, preferred_element_type=F32)
             for p in range(N_PAIRS)], axis=1)

    kk = k * k_k
    kk = kk * lax.rsqrt(jnp.maximum(seg_sum(kk * kk), 1e-24))
    yield "prep"
    k_h = k * (1.0 + (a_gate - 1.0) * k_a)
    a_vec = -kk
    b_vec = kk * a_gate

    slabs = {}

    def slab(b, ch):
        if (b, ch) not in slabs:
            r0 = (b * nch + ch) * c_len
            rs = slice(r0, r0 + c_len)
            gc = g_cum[rs]
            ge = jnp.broadcast_to(gc[c_len - 1:c_len, :], (c_len, DA))
            e_neg = jnp.exp(-gc)
            e_end = jnp.exp(ge - gc)
            slabs[(b, ch)] = dict(
                rq=r[rs] * jnp.exp(gc), aq=a_vec[rs] * jnp.exp(gc - lw[rs]), kn=k_h[rs] * e_neg,
                bn=b_vec[rs] * e_neg, k_end=k_h[rs] * e_end, b_end=b_vec[rs] * e_end,
                decay_end=jnp.exp(ge[0:1, :]), v=v[rs])
        return slabs[(b, ch)]

    srow = lax.broadcasted_iota(jnp.int32, (stk, PAIR), 0)
    slane = lax.broadcasted_iota(jnp.int32, (stk, PAIR), 1)
    own_lanes = (srow < c_len) == (slane < A_HEAD_DIM)

    def stack(x):
        return jnp.where(own_lanes, jnp.concatenate([x, x], axis=0), 0.0).astype(BF16)

    mrow = lax.broadcasted_iota(jnp.int32, (stk, stk), 0)
    mcol = lax.broadcasted_iota(jnp.int32, (stk, stk), 1)
    strict = (mrow & (c_len - 1)) > (mcol & (c_len - 1))
    lower = (mrow & (c_len - 1)) >= (mcol & (c_len - 1))
    eye = jnp.where(mrow == mcol, 1.0, 0.0).astype(F32)

    def cut(name, u):
        b, ch, p = u
        return slab(b, ch)[name][:, p * PAIR:(p + 1) * PAIR]

    lhs, v_st, a_ak_v, a_rk, a_rb, t_inv, y_parts = {}, {}, {}, {}, {}, {}, {}

    def scores_and_inverse(units):
        power = {}
        for u in units:
            lhs[u] = jnp.concatenate([stack(cut("aq", u)), stack(cut("rq", u))], axis=0)
            v_st[u] = stack(cut("v", u))
            sc_b = _dot_nt(lhs[u], stack(cut("bn", u)))
            sc_k = _dot_nt(lhs[u], stack(cut("kn", u)))
            power[u] = jnp.where(strict, sc_b[:stk], 0.0)
            a_rb[u] = jnp.where(lower, sc_b[stk:], 0.0)
            a_rk[u] = jnp.where(lower, sc_k[stk:], 0.0)
            a_ak_v[u] = jnp.where(strict, sc_k[:stk], 0.0)
        yield
        for u in units:
            t_inv[u] = eye + power[u]
            a_ak_v[u] = _dot(a_ak_v[u], v_st[u])
            power[u] = _dot(power[u], power[u])
        yield
        for lvl in range(1, _log2(c_len)):
            new_power = {}
            for u in units:
                t_inv[u] = t_inv[u] + _dot(power[u], t_inv[u])
                if lvl < _log2(c_len) - 1:
                    new_power[u] = _dot(power[u], power[u])
            power = new_power
            yield

    def apply_state(units):
        from_state = {u: _dot_nt(lhs[u], s_ref[u[0], u[2]]) for u in units}
        yield
        corr = {u: _dot(t_inv[u], from_state[u][:stk] + a_ak_v[u]) for u in units}
        yield
        for u in units:
            vu = jnp.concatenate([v_st[u], corr[u].astype(BF16)], axis=0)
            if stk % LANES == 0:
                y_st = from_state[u][stk:] + _dot(jnp.concatenate([a_rk[u], a_rb[u]], axis=1), vu)
            else:
                y_st = from_state[u][stk:] + _dot(a_rk[u], v_st[u]) + _dot(a_rb[u], corr[u])
            y_parts[u] = y_st[:c_len] + y_st[c_len:]
            ends = jnp.concatenate([stack(cut("k_end", u)), stack(cut("b_end", u))], axis=0)
            s_ref[u[0], u[2]] = s_ref[u[0], u[2]] * cut("decay_end", u) + _dot_tn(vu, ends)
        yield

    chunk_units = [[(b, ch, p) for b in range(bt) for p in range(N_PAIRS)] for ch in range(nch)]
    for _ in scores_and_inverse(chunk_units[0]):
        yield "dense"
    for ch in range(nch):
        if ch + 1 < nch:
            for _ in _round_robin(scores_and_inverse(chunk_units[ch + 1]), apply_state(chunk_units[ch])):
                yield "dense"
        else:
            for _ in apply_state(chunk_units[ch]):
                yield "tail"

    y = jnp.concatenate(
        [jnp.concatenate([y_parts[(b, ch, p)] for p in range(N_PAIRS)], axis=1)
         for b in range(bt) for ch in range(nch)], axis=0)
    inv_n = 1.0 / A_HEAD_DIM
    mean = seg_sum(y) * inv_n
    bonus = seg_sum(r * k_h * r_k)
    yield "tail"
    dev = y - mean
    var = seg_sum(dev * dev) * inv_n
    yield "tail"
    y = dev * lax.rsqrt(var + GN_EPS) * ln_w + ln_b
    y = ((y + bonus * v) * g).astype(BF16)
    for b in range(bt):
        yg_ref[b] = y[b * frames:(b + 1) * frames]

    @pl.when(c == pl.num_programs(1) - 1)
    def _():
        prow = lax.broadcasted_iota(jnp.int32, (PAIR, A_HEAD_DIM), 0)
        pcol = lax.broadcasted_iota(jnp.int32, (PAIR, A_HEAD_DIM), 1)
        from_lo = jnp.where(prow == pcol, 1.0, 0.0).astype(BF16)
        from_hi = jnp.where(prow == pcol + A_HEAD_DIM, 1.0, 0.0).astype(BF16)
        for b in range(bt):
            for p in range(N_PAIRS):
                s_pair = s_ref[b, p]
                s_out_ref[b, 2 * p] = _dot_x01(s_pair[:A_HEAD_DIM], from_lo)
                s_out_ref[b, 2 * p + 1] = _dot_x01(s_pair[A_HEAD_DIM:], from_hi)


def _hgrn_part(layer, bt, tb, c, h, refs):
    (w_ref, lg_ref, nw_ref, s0_ref, on_ref, s_out_ref, st_ref) = refs
    n_blk = tb // HGRN_BLOCK

    def project(c0):
        return [jnp.dot(h[b * tb:(b + 1) * tb], w_ref[:, c0:c0 + BK], preferred_element_type=F32)
                for b in range(bt)]

    fz_all = project(BK)
    yield
    q_all = project(0)
    yield

    @pl.when(c == 0)
    def _():
        for b in range(bt):
            for hd in range(B_HEADS):
                st_ref[b, hd] = s0_ref[b, hd].T

    lg = lg_ref[...]
    lrows = [lg[i:i + 1, :] for i in range(DEPTH)]
    mx = functools.reduce(jnp.maximum, lrows)
    ex = [jnp.exp(x - mx) for x in lrows]
    den = functools.reduce(lambda a, b: a + b, ex)
    sm = [e / den for e in ex]
    lb = functools.reduce(lambda a, b: a + b, sm[:layer + 1]) - sm[0]

    row = lax.broadcasted_iota(jnp.int32, (tb, tb), 0)
    col = lax.broadcasted_iota(jnp.int32, (tb, tb), 1)
    blk_shift = _log2(HGRN_BLOCK)
    causal = jnp.logical_and(lax.shift_right_logical(row, blk_shift) == lax.shift_right_logical(col, blk_shift),
                             row >= col)
    causal01 = jnp.where(causal, 1.0, 0.0).astype(BF16)

    f_all = [lb + (1.0 - lb) * _sigmoid(fz_all[b]) for b in range(bt)]
    g_cum_all = [_dot_01(causal01, jnp.log(f_all[b])) for b in range(bt)]
    yield
    iv = project(2 * BK)
    yield

    frow = lax.broadcasted_iota(jnp.int32, (tb, BK), 0)
    second = (lax.shift_right_logical(frow, blk_shift) & 1) == 1
    zeros_blk = jnp.zeros((HGRN_BLOCK, BK), F32)
    sr = min(tb, SCORE_ROWS)
    srow = lax.broadcasted_iota(jnp.int32, (sr, sr), 0)
    scol = lax.broadcasted_iota(jnp.int32, (sr, sr), 1)

    def split_mask(hl):
        grp = _log2(2 * hl)
        return jnp.logical_and(
            lax.shift_right_logical(srow, grp) == lax.shift_right_logical(scol, grp),
            jnp.logical_and((srow & (2 * hl - 1)) >= hl, (scol & (2 * hl - 1)) < hl))

    pair_mask = split_mask(HGRN_BLOCK)

    levels = [HGRN_BLOCK >> (i + 1) for i in range(blk_shift)]
    pos_in = {hl: frow & (2 * hl - 1) for hl in levels}
    level_mask = {hl: split_mask(hl) for hl in levels}
    toward_anchor = {hl: jnp.where(pos_in[hl] >= hl, 1.0, -1.0).astype(F32) for hl in levels}

    def anchor(g, hl):
        if 2 * hl >= 8:
            g3 = g.reshape(tb // (2 * hl), 2 * hl, BK)
            return jnp.broadcast_to(g3[:, hl - 1:hl, :], g3.shape).reshape(tb, BK)
        out = g
        for pos in range(2 * hl):
            if pos != hl - 1:
                out = jnp.where(pos_in[hl] == pos, pltpu.roll(g, (pos - (hl - 1)) % tb, 0), out)
        return out

    q_lvl = {hl: [] for hl in levels}
    k_lvl = {hl: [] for hl in levels}
    own = []
    qg, qg_pair, k_end, k_end_pair, decay_pair = [], [], [], [], []
    for b in range(bt):
        q = q_all[b]
        kx = 1.0 - f_all[b]
        g_cum = g_cum_all[b]
        qs = q * _sigmoid(q)
        own.append(qs * kx)
        for hl in levels:
            partial = jnp.exp((g_cum - anchor(g_cum, hl)) * toward_anchor[hl])
            q_lvl[hl].append((qs * partial).astype(BF16))
            k_lvl[hl].append((kx * partial).astype(BF16))
        g_end = jnp.concatenate(
            [jnp.broadcast_to(g_cum[(j + 1) * HGRN_BLOCK - 1:(j + 1) * HGRN_BLOCK, :], (HGRN_BLOCK, BK))
             for j in range(n_blk)], axis=0)
        decay = jnp.exp(g_end)
        decay_prev = jnp.concatenate([zeros_blk, decay[:-HGRN_BLOCK]], axis=0)
        decay_next = jnp.concatenate([decay[HGRN_BLOCK:], zeros_blk], axis=0)
        q_dec = qs * jnp.exp(g_cum)
        k_to_end = kx * jnp.exp(g_end - g_cum)
        qg.append(q_dec.astype(BF16))
        qg_pair.append((q_dec * jnp.where(second, decay_prev, 1.0)).astype(BF16))
        k_end.append(k_to_end.astype(BF16))
        k_end_pair.append((k_to_end * jnp.where(second, 1.0, decay_next)).astype(BF16))
        decay_pair.append(decay * decay_next)

    units = [(b, hd) for b in range(bt) for hd in range(B_HEADS)]
    n_pair = tb // PAIR_LEN

    def cut(x, u):
        return x[u[0]][:, u[1] * B_KEY_DIM:(u[1] + 1) * B_KEY_DIM]

    def pair(j):
        return slice(j * PAIR_LEN, (j + 1) * PAIR_LEN)

    def tile(i):
        return slice(i * sr, (i + 1) * sr)

    def scores(mask, qx, kx_):
        return {(u, i): jnp.where(mask, _dot_nt(cut(qx, u)[tile(i)], cut(kx_, u)[tile(i)]), 0.0)
                for u in units for i in range(tb // sr)}

    att = scores(pair_mask, qg, k_end)
    yield
    for hl in levels:
        level = scores(level_mask[hl], q_lvl[hl], k_lvl[hl])
        att = {key: att[key] + level[key] for key in att}
        yield
    upd = {}
    for j in range(n_pair):
        for u in units:
            upd[(u, j)] = _dot_tn(cut(iv, u)[pair(j)], cut(k_end_pair, u)[pair(j)])
        yield
    starts = {}
    for u in units:
        s = st_ref[u[0], u[1]]
        dec = cut(decay_pair, u)
        for j in range(n_pair):
            starts[(u, j)] = s.astype(BF16)
            s = s * dec[j * PAIR_LEN:j * PAIR_LEN + 1, :] + upd[(u, j)]
        st_ref[u[0], u[1]] = s
    og = project(2 * BK + DB)
    yield
    o_acc = {u: jnp.concatenate([_dot(att[(u, i)], cut(iv, u)[tile(i)]) for i in range(tb // sr)], axis=0)
             for u in units}
    yield
    inter = {}
    for j in range(n_pair):
        for u in units:
            inter[(u, j)] = _dot_nt(cut(qg_pair, u)[pair(j)], starts[(u, j)])
        yield

    for b in range(bt):
        outs = []
        for hd in range(B_HEADS):
            u = (b, hd)
            o_h = o_acc[u] + jnp.concatenate([inter[(u, j)] for j in range(n_pair)], axis=0)
            o_h = o_h + jnp.sum(cut(own, u), axis=-1, keepdims=True) * cut(iv, u)
            outs.append(o_h * lax.rsqrt(jnp.mean(o_h * o_h, axis=-1, keepdims=True) + RMS_EPS))
        o = jnp.concatenate(outs, axis=1)
        on_ref[b] = (o * nw_ref[...] * (og[b] * _sigmoid(og[b]))).astype(BF16)

    @pl.when(c == pl.num_programs(1) - 1)
    def _():
        for b in range(bt):
            for hd in range(B_HEADS):
                s_out_ref[b, hd] = st_ref[b, hd].T


def _drive(gen):
    for _ in gen:
        pass


def _normalised(x_ref, nm_ref, bt):
    return jnp.concatenate([_rms(x_ref[b], nm_ref[...]).astype(BF16) for b in range(bt)], axis=0)


def _rwkv_kernel(layer, bt, nch, c_len, x_ref, nm_ref, *refs):
    _drive(_rwkv_part(layer, bt, nch, c_len, pl.program_id(1), _normalised(x_ref, nm_ref, bt), refs))


def _hgrn_kernel(layer, bt, tb, x_ref, nm_ref, *refs):
    _drive(_hgrn_part(layer, bt, tb, pl.program_id(1), _normalised(x_ref, nm_ref, bt), refs))


def _rwkv(layer, x, shift0, s_all, vfirst, nm, w_rw_all, mu_p, pv, w2_p, a2_p, g2_p, vw1_p, vw2_p, c_len, nch, bt):
    bsz, t_len, _ = x.shape
    frames = nch * c_len
    const2 = lambda b, c: (0, 0)
    step = lambda b, c: (b, c, 0)
    per_seq3 = lambda b, c: (b, 0, 0)
    per_seq4 = lambda b, c: (b, 0, 0, 0)
    in_specs = [
        pl.BlockSpec((bt, frames, D_MODEL), step),
        pl.BlockSpec((1, D_MODEL), const2),
        pl.BlockSpec((bt, 1, GROUP_COLS), per_seq3),
        pl.BlockSpec((None, bt, A_HEADS, A_HEAD_DIM, A_HEAD_DIM), lambda b, c: (layer, b, 0, 0, 0)),
    ]
    args = [x, nm, shift0, s_all]
    if layer > 0:
        in_specs.append(pl.BlockSpec((bt, frames, DA), step))
        args.append(vfirst)
    in_specs += [
        _resident((None, D_MODEL, GROUP_COLS), lambda b, c: (layer, 0, 0)),
        pl.BlockSpec((1, GROUP_COLS), const2),
        pl.BlockSpec((8, DA), const2),
        pl.BlockSpec((LANES, DA), const2),
        pl.BlockSpec((LANES, DA), const2),
        pl.BlockSpec((LORA_G_PAD, DA), const2),
    ]
    args += [w_rw_all, mu_p, pv, w2_p, a2_p, g2_p]
    if layer > 0:
        in_specs += [pl.BlockSpec((DA, LANES), const2), pl.BlockSpec((LANES, DA), const2)]
        args += [vw1_p, vw2_p]
    out_specs = [
        pl.BlockSpec((bt, frames, DA), step),
        pl.BlockSpec((bt, 1, GROUP_COLS), per_seq3),
        pl.BlockSpec((bt, A_HEADS, A_HEAD_DIM, A_HEAD_DIM), per_seq4),
    ]
    out_shape = [
        jax.ShapeDtypeStruct((bsz, t_len, DA), BF16),
        jax.ShapeDtypeStruct((bsz, 1, GROUP_COLS), F32),
        jax.ShapeDtypeStruct((bsz, A_HEADS, A_HEAD_DIM, A_HEAD_DIM), F32),
    ]
    if layer == 0:
        out_specs.append(pl.BlockSpec((bt, frames, DA), step))
        out_shape.append(jax.ShapeDtypeStruct((bsz, t_len, DA), F32))
    return pl.pallas_call(
        functools.partial(_rwkv_kernel, layer, bt, nch, c_len),
        grid=(bsz // bt, t_len // frames),
        in_specs=in_specs,
        out_specs=out_specs,
        out_shape=out_shape,
        scratch_shapes=[pltpu.VMEM((bt, frames + 8, GROUP_COLS), F32),
                        pltpu.VMEM((bt, N_PAIRS, PAIR, PAIR), F32)],
        compiler_params=pltpu.CompilerParams(
            dimension_semantics=("arbitrary", "arbitrary"), vmem_limit_bytes=VMEM_LIMIT),
        name=f"rwkv{layer}",
    )(*args)


def _hgrn(layer, x, nm, w_hg_all, lb_logits, norm_w, s_all, tb, bt):
    bsz, t_len, _ = x.shape
    const2 = lambda b, t: (0, 0)
    return pl.pallas_call(
        functools.partial(_hgrn_kernel, layer, bt, tb),
        grid=(bsz // bt, t_len // tb),
        in_specs=[
            pl.BlockSpec((bt, tb, D_MODEL), lambda b, t: (b, t, 0)),
            pl.BlockSpec((1, D_MODEL), const2),
            _resident((None, D_MODEL, HGRN_COLS), lambda b, t: (layer, 0, 0)),
            pl.BlockSpec((DEPTH, BK), const2),
            pl.BlockSpec((1, DB), const2),
            pl.BlockSpec((None, bt, B_HEADS, B_KEY_DIM, B_VAL_DIM), lambda b, t: (layer, b, 0, 0, 0)),
        ],
        out_specs=[
            pl.BlockSpec((bt, tb, DB), lambda b, t: (b, t, 0)),
            pl.BlockSpec((bt, B_HEADS, B_KEY_DIM, B_VAL_DIM), lambda b, t: (b, 0, 0, 0)),
        ],
        out_shape=[
            jax.ShapeDtypeStruct((bsz, t_len, DB), BF16),
            jax.ShapeDtypeStruct((bsz, B_HEADS, B_KEY_DIM, B_VAL_DIM), F32),
        ],
        scratch_shapes=[pltpu.VMEM((bt, B_HEADS, B_VAL_DIM, B_KEY_DIM), F32)],
        compiler_params=pltpu.CompilerParams(
            dimension_semantics=("arbitrary", "arbitrary"), vmem_limit_bytes=VMEM_LIMIT),
        name=f"hgrn{layer}",
    )(x, nm, w_hg_all, lb_logits, norm_w, s_all)


FF_CHUNK = 1024


def _mix_ffn_kernel(final, *refs):
    if final:
        (x_ref, yg_ref, on_ref, nm_ref, wg_ref, wa_ref, wb_ref, wo_ref, nf_ref, up_ref, dn_ref, nfin_ref,
         o_ref) = refs
    else:
        (x_ref, yg_ref, on_ref, nm_ref, wg_ref, wa_ref, wb_ref, wo_ref, nf_ref, up_ref, dn_ref, o_ref) = refs
    x = x_ref[...]
    gt = _sigmoid(jnp.dot(_rms(x, nm_ref[...]).astype(BF16), wg_ref[...], preferred_element_type=F32))
    ya = jnp.dot(yg_ref[...], wa_ref[...], preferred_element_type=F32)
    yb = jnp.dot(on_ref[...], wb_ref[...], preferred_element_type=F32)
    merged = gt[:, 0:D_MODEL] * ya + gt[:, D_MODEL:] * yb
    x = x + jnp.dot(merged.astype(BF16), wo_ref[...], preferred_element_type=F32)
    h = _rms(x, nf_ref[...]).astype(BF16)
    acc = x
    for cf in range(D_FF // FF_CHUNK):
        u = jnp.dot(h, up_ref[:, cf * FF_CHUNK:(cf + 1) * FF_CHUNK], preferred_element_type=F32)
        u = jnp.maximum(u, 0.0)
        acc = acc + jnp.dot((u * u).astype(BF16), dn_ref[cf * FF_CHUNK:(cf + 1) * FF_CHUNK, :],
                            preferred_element_type=F32)
    if final:
        acc = _rms(acc, nfin_ref[...])
    o_ref[...] = acc


def _mix_ffn(layer, x, yg, on, nm, nf, nfin, w):
    final = layer == DEPTH - 1
    n = x.shape[0]
    tm = min(n, 512)
    small = lambda i: (0, 0)
    mine = lambda i: (layer, 0, 0)
    in_specs = [
        pl.BlockSpec((tm, D_MODEL), lambda i: (i, 0)),
        pl.BlockSpec((tm, DA), lambda i: (i, 0)),
        pl.BlockSpec((tm, DB), lambda i: (i, 0)),
        _resident((1, D_MODEL), small),
        _resident((None, D_MODEL, GATE_COLS), mine),
        _resident((None, DA, D_MODEL), mine),
        _resident((None, DB, D_MODEL), mine),
        _resident((None, D_MODEL, D_MODEL), mine),
        _resident((1, D_MODEL), small),
        _resident((None, D_MODEL, D_FF), mine),
        _resident((None, D_FF, D_MODEL), mine),
    ]
    args = [x, yg, on, nm, w["wg"], w["wa"], w["wb"], w["wo"], nf, w["up"], w["dn"]]
    if final:
        in_specs.append(_resident((1, D_MODEL), small))
        args.append(nfin)
    return pl.pallas_call(
        functools.partial(_mix_ffn_kernel, final),
        grid=(n // tm,),
        in_specs=in_specs,
        out_specs=pl.BlockSpec((tm, D_MODEL), lambda i: (i, 0)),
        out_shape=jax.ShapeDtypeStruct((n, D_MODEL), F32),
        compiler_params=pltpu.CompilerParams(dimension_semantics=("arbitrary",), vmem_limit_bytes=VMEM_LIMIT),
        name="mix_ffn_final" if final else "mix_ffn",
    )(*args)


def _pad_rows(w, rows, at=0):
    out = jnp.zeros((rows, w.shape[1]), w.dtype)
    return out.at[at:at + w.shape[0]].set(w)


def _prep_weights(p):
    w_in = p["w_in"]
    pad = ((0, 0), (0, 0), (0, GROUP_COLS - RWKV_COLS))
    return dict(
        w_rw=jnp.pad(w_in[:, :, :RWKV_COLS], pad).astype(BF16),
        w_hg=w_in[:, :, RWKV_COLS:RWKV_COLS + HGRN_COLS].astype(BF16),
        wg=w_in[:, :, RWKV_COLS + HGRN_COLS:].astype(BF16),
        wa=p["w_out_a"].astype(BF16), wb=p["w_out_b"].astype(BF16), wo=p["w_out"].astype(BF16),
        up=p["w_ffn_up"].astype(BF16), dn=p["w_ffn_down"].astype(BF16),
    )


def _prep_layer(l, p):
    mu_p = jnp.zeros((1, GROUP_COLS), F32).at[0, :RWKV_COLS].set(p["rwkv_mu"][l])
    v0 = p["rwkv_v0"][l - 1] if l > 0 else jnp.zeros((DA,), F32)
    pv = jnp.stack([p["rwkv_w0"][l], p["rwkv_a0"][l], p["rwkv_k_k"][l], p["rwkv_k_a"][l],
                    p["rwkv_r_k"][l].reshape(DA), p["rwkv_ln_w"][l], p["rwkv_ln_b"][l], v0])
    out = dict(
        mu_p=mu_p, pv=pv,
        w2_p=_pad_rows(p["rwkv_w2"][l], LANES, 0).astype(BF16),
        a2_p=_pad_rows(p["rwkv_a2"][l], LANES, W_LORA).astype(BF16),
        g2_p=_pad_rows(p["rwkv_g2"][l], LORA_G_PAD, 0).astype(BF16),
        vw1_p=None, vw2_p=None,
        norm_mix=p["norm_mix"][l].reshape(1, D_MODEL),
        hgrn_norm_w=p["hgrn_norm_w"][l].reshape(1, DB),
        nf=p["norm_ffn"][l].reshape(1, D_MODEL),
    )
    if l > 0:
        w1 = p["rwkv_vres_w1"][l - 1]
        out["vw1_p"] = jnp.zeros((DA, LANES), F32).at[:, :V_LORA].set(w1).astype(BF16)
        out["vw2_p"] = _pad_rows(p["rwkv_vres_w2"][l - 1], LANES, 0).astype(BF16)
    return out


def _seq_tile(bsz, frames, step_rows):
    bt = min(bsz, max(1, step_rows // frames))
    assert bsz % bt == 0, (bsz, bt)
    return bt


def _trunk(x, state_shift, state_rwkv, state_hgrn, layers, weights, p):
    bsz, t_len, _ = x.shape
    c_len = min(t_len, RWKV_CHUNK)
    nch = min(t_len // c_len, RWKV_CHUNKS)
    tb = min(t_len, HGRN_TILE)
    assert t_len % (nch * c_len) == 0 and t_len % tb == 0 and tb % PAIR_LEN == 0
    nfin = p["norm_final"].reshape(1, D_MODEL)
    vfirst = None
    shifts, rwkv_states, hgrn_states = [], [], []
    for l in range(DEPTH):
        lp = layers[l]
        shift0 = jnp.zeros((bsz, 1, GROUP_COLS), F32).at[:, 0, :RWKV_COLS].set(state_shift[l])
        res = _rwkv(l, x, shift0, state_rwkv, vfirst, lp["norm_mix"], weights["w_rw"], lp["mu_p"], lp["pv"],
                    lp["w2_p"], lp["a2_p"], lp["g2_p"], lp["vw1_p"], lp["vw2_p"], c_len, nch,
                    min(_seq_tile(bsz, nch * c_len, RWKV_STEP_ROWS), RWKV_STEP_UNITS // (nch * N_PAIRS)))
        if l == 0:
            yg, shift_out, s_rwkv, vfirst = res
        else:
            yg, shift_out, s_rwkv = res
        on, s_hgrn = _hgrn(l, x, lp["norm_mix"], weights["w_hg"], p["hgrn_lb_logits"], lp["hgrn_norm_w"],
                           state_hgrn, tb, _seq_tile(bsz, tb, HGRN_STEP_ROWS))
        n = bsz * t_len
        x = _mix_ffn(l, x.reshape(n, D_MODEL), yg.reshape(n, DA), on.reshape(n, DB), lp["norm_mix"], lp["nf"],
                     nfin, weights).reshape(bsz, t_len, D_MODEL)
        shifts.append(shift_out[:, 0, :RWKV_COLS])
        rwkv_states.append(s_rwkv)
        hgrn_states.append(s_hgrn)
    return x, jnp.stack(shifts), jnp.stack(rwkv_states), jnp.stack(hgrn_states)


def kernel(x_prompt, x_sample, state_shift, state_rwkv, state_hgrn, norm_mix, w_in, rwkv_mu, rwkv_w0, rwkv_w2, rwkv_a0, rwkv_a2, rwkv_g2, rwkv_v0, rwkv_vres_w1, rwkv_vres_w2, rwkv_k_k, rwkv_k_a, rwkv_r_k, rwkv_ln_w, rwkv_ln_b, hgrn_lb_logits, hgrn_norm_w, w_out_a, w_out_b, w_out, norm_ffn, w_ffn_up, w_ffn_down, norm_final):
    p = dict(norm_mix=norm_mix, w_in=w_in, rwkv_mu=rwkv_mu, rwkv_w0=rwkv_w0, rwkv_w2=rwkv_w2, rwkv_a0=rwkv_a0,
             rwkv_a2=rwkv_a2, rwkv_g2=rwkv_g2, rwkv_v0=rwkv_v0, rwkv_vres_w1=rwkv_vres_w1,
             rwkv_vres_w2=rwkv_vres_w2, rwkv_k_k=rwkv_k_k, rwkv_k_a=rwkv_k_a, rwkv_r_k=rwkv_r_k,
             rwkv_ln_w=rwkv_ln_w, rwkv_ln_b=rwkv_ln_b, hgrn_lb_logits=hgrn_lb_logits, hgrn_norm_w=hgrn_norm_w,
             w_out_a=w_out_a, w_out_b=w_out_b, w_out=w_out, norm_ffn=norm_ffn, w_ffn_up=w_ffn_up,
             w_ffn_down=w_ffn_down, norm_final=norm_final)
    layers = [_prep_layer(l, p) for l in range(DEPTH)]
    weights = _prep_weights(p)
    bp = x_prompt.shape[0]
    dt = x_prompt.dtype
    zero_shift = jnp.zeros((DEPTH, bp, RWKV_COLS), dt)
    zero_rwkv = jnp.zeros((DEPTH, bp, A_HEADS, A_HEAD_DIM, A_HEAD_DIM), dt)
    zero_hgrn = jnp.zeros((DEPTH, bp, B_HEADS, B_KEY_DIM, B_VAL_DIM), dt)
    y_prompt, shift_p, rwkv_p, hgrn_p = _trunk(x_prompt, zero_shift, zero_rwkv, zero_hgrn, layers, weights, p)
    y_sample, shift_s, rwkv_s, hgrn_s = _trunk(x_sample, state_shift, state_rwkv, state_hgrn, layers, weights, p)
    return (y_prompt, y_sample, shift_p, rwkv_p, hgrn_p, shift_s, rwkv_s, hgrn_s)
```

```python
import functools
import math

import jax
import jax.numpy as jnp
from jax import lax
from jax.experimental import pallas as pl
from jax.experimental.pallas import tpu as pltpu

F32 = jnp.float32
BF16 = jnp.bfloat16

D_MODEL = 1024
DEPTH = 2
A_HEADS = 8
A_HEAD_DIM = 64
DA = A_HEADS * A_HEAD_DIM
W_LORA = 64
A_LORA = 64
V_LORA = 32
G_LORA = 160
RWKV_COLS = 3 * DA + W_LORA + A_LORA + G_LORA
B_HEADS = 4
B_KEY_DIM = 128
B_VAL_DIM = 128
BK = B_HEADS * B_KEY_DIM
DB = B_HEADS * B_VAL_DIM
HGRN_COLS = 2 * BK + 2 * DB
GATE_COLS = 2 * D_MODEL
D_FF = 4 * D_MODEL
HGRN_BLOCK = 16
RMS_EPS = 1e-6
GN_EPS = 64e-5

LANES = 128
GROUP_COLS = 2048
LORA_WA = 3 * DA
LORA_G = LORA_WA + W_LORA + A_LORA
LORA_G_PAD = 256
PAIR = 2 * A_HEAD_DIM
N_PAIRS = A_HEADS // 2
PAIR_LEN = 2 * HGRN_BLOCK
VMEM_LIMIT = 56 * 1024 * 1024

RWKV_CHUNK = 64
RWKV_CHUNKS = 4
RWKV_STEP_ROWS = 512
RWKV_STEP_UNITS = 32
CUMSUM_ROWS = 256
HGRN_TILE = 512
HGRN_STEP_ROWS = 1024
HGRN_STEP_UNITS = 64
SCORE_ROWS = 128


def _dot(a, b):
    return jnp.dot(a.astype(BF16), b.astype(BF16), preferred_element_type=F32)


def _dot_nt(a, b):
    return lax.dot_general(a.astype(BF16), b.astype(BF16), (((1,), (1,)), ((), ())), preferred_element_type=F32)


def _dot_tn(a, b):
    return lax.dot_general(a.astype(BF16), b.astype(BF16), (((0,), (0,)), ((), ())), preferred_element_type=F32)


def _split3(x):
    hi = x.astype(BF16)
    r1 = x - hi.astype(F32)
    mid = r1.astype(BF16)
    return hi, mid, (r1 - mid.astype(F32)).astype(BF16)


def _dot_01(m01, x):
    return functools.reduce(lambda a, b: a + b, [jnp.dot(m01, t, preferred_element_type=F32) for t in _split3(x)])


def _dot_x01(x, m01):
    return functools.reduce(lambda a, b: a + b, [jnp.dot(t, m01, preferred_element_type=F32) for t in _split3(x)])


def _rms(x, w):
    return x * lax.rsqrt(jnp.mean(x * x, axis=-1, keepdims=True) + RMS_EPS) * w


def _sigmoid(x):
    return 1.0 / (1.0 + jnp.exp(-x))


def _log2(n):
    assert n & (n - 1) == 0, n
    return n.bit_length() - 1


def _resident(shape, index_map):
    return pl.BlockSpec(shape, index_map, pipeline_mode=pl.Buffered(1))


_DONE = object()


def _round_robin(*gens):
    live = list(gens)
    while live:
        for gen in list(live):
            if next(gen, _DONE) is _DONE:
                live.remove(gen)
            else:
                yield


def _rwkv_part(layer, bt, nch, c_len, c, h, refs):
    if layer == 0:
        (shift_ref, s0_ref, w_ref, mu_ref, pv_ref, w2_ref, a2_ref, g2_ref,
         yg_ref, shift_out_ref, s_out_ref, vfirst_out_ref, xs_ref, s_ref) = refs
    else:
        (shift_ref, s0_ref, vfirst_ref, w_ref, mu_ref, pv_ref, w2_ref, a2_ref, g2_ref, vw1_ref, vw2_ref,
         yg_ref, shift_out_ref, s_out_ref, xs_ref, s_ref) = refs
    frames = nch * c_len
    rows = bt * frames
    stk = 2 * c_len

    hrow = lax.broadcasted_iota(jnp.int32, (A_HEAD_DIM, PAIR), 0)
    hlane = lax.broadcasted_iota(jnp.int32, (A_HEAD_DIM, PAIR), 1)
    to_lo = jnp.where(hlane == hrow, 1.0, 0.0).astype(BF16)
    to_hi = jnp.where(hlane == hrow + A_HEAD_DIM, 1.0, 0.0).astype(BF16)

    @pl.when(c == 0)
    def _():
        for b in range(bt):
            xs_ref[b, 7:8, :] = shift_ref[b]
            for p in range(N_PAIRS):
                s_ref[b, p] = jnp.concatenate(
                    [_dot_x01(s0_ref[b, 2 * p], to_lo), _dot_x01(s0_ref[b, 2 * p + 1], to_hi)], axis=0)

    def project(c0, c1):
        rw = jnp.dot(h, w_ref[:, c0:c1], preferred_element_type=F32)
        prevs = []
        for b in range(bt):
            rw_b = rw[b * frames:(b + 1) * frames]
            xs_ref[b, 8:8 + frames, c0:c1] = rw_b
            prevs.append(xs_ref[b, 7:7 + frames, c0:c1])
            last = rw_b[frames - 1:frames, :]
            xs_ref[b, 7:8, c0:c1] = last
            shift_out_ref[b, :, c0:c1] = last
        return rw + (jnp.concatenate(prevs, axis=0) - rw) * mu_ref[:, c0:c1]

    w0 = pv_ref[0:1, :]
    a0 = pv_ref[1:2, :]
    k_k = pv_ref[2:3, :]
    k_a = pv_ref[3:4, :]
    r_k = pv_ref[4:5, :]
    ln_w = pv_ref[5:6, :]
    ln_b = pv_ref[6:7, :]

    cs_rows = min(rows, CUMSUM_ROWS)
    row = lax.broadcasted_iota(jnp.int32, (cs_rows, cs_rows), 0)
    col = lax.broadcasted_iota(jnp.int32, (cs_rows, cs_rows), 1)
    chunk_shift = _log2(c_len)
    causal01 = jnp.where(jnp.logical_and(
        lax.shift_right_logical(row, chunk_shift) == lax.shift_right_logical(col, chunk_shift), row >= col),
        1.0, 0.0).astype(BF16)

    lora = project(LORA_WA, GROUP_COLS)
    yield "prep"
    k = project(DA, 2 * DA)
    yield "prep"
    wa = lora[:, 0:LORA_G - LORA_WA]
    gl = lora[:, LORA_G - LORA_WA:LORA_G - LORA_WA + LORA_G_PAD]
    w_raw = w0 + _dot(jnp.tanh(wa), w2_ref[...])
    a_gate = _sigmoid(a0 + _dot(wa, a2_ref[...]))
    g = _dot(_sigmoid(gl), g2_ref[...])
    yield "prep"
    r = project(0, DA)
    yield "prep"
    lw = (-math.exp(-0.5)) * _sigmoid(w_raw)
    g_cum = jnp.concatenate([_dot_01(causal01, lw[i:i + cs_rows]) for i in range(0, rows, cs_rows)],
                            axis=0)
    yield "prep"
    v = project(2 * DA, 3 * DA)
    yield "prep"
    if layer == 0:
        for b in range(bt):
            vfirst_out_ref[b] = v[b * frames:(b + 1) * frames]
    else:
        v0 = pv_ref[7:8, :]
        vg = _sigmoid(v0 + _dot(_dot(v, vw1_ref[...]), vw2_ref[...]))
        vfirst = jnp.concatenate([vfirst_ref[b] for b in range(bt)], axis=0)
        v = v + (vfirst - v) * vg

    lane = lax.broadcasted_iota(jnp.int32, (PAIR, PAIR), 1)
    sub = lax.broadcasted_iota(jnp.int32, (PAIR, PAIR), 0)
    head_ones = jnp.where((lane < A_HEAD_DIM) == (sub < A_HEAD_DIM), 1.0, 0.0).astype(BF16)

    def seg_sum(x):
        return jnp.concatenate(
            [jnp.dot(x[:, p * PAIR:(p + 1) * PAIR].astype(BF16), head_ones, preferred_element_type=F32)
             for p in range(N_PAIRS)], axis=1)

    kk = k * k_k
    kk = kk * lax.rsqrt(jnp.maximum(seg_sum(kk * kk), 1e-24))
    yield "prep"
    k_h = k * (1.0 + (a_gate - 1.0) * k_a)
    a_vec = -kk
    b_vec = kk * a_gate

    slabs = {}

    def slab(b, ch):
        if (b, ch) not in slabs:
            r0 = (b * nch + ch) * c_len
            rs = slice(r0, r0 + c_len)
            gc = g_cum[rs]
            ge = jnp.broadcast_to(gc[c_len - 1:c_len, :], (c_len, DA))
            e_neg = jnp.exp(-gc)
            e_end = jnp.exp(ge - gc)
            slabs[(b, ch)] = dict(
                rq=r[rs] * jnp.exp(gc), aq=a_vec[rs] * jnp.exp(gc - lw[rs]), kn=k_h[rs] * e_neg,
                bn=b_vec[rs] * e_neg, k_end=k_h[rs] * e_end, b_end=b_vec[rs] * e_end,
                decay_end=jnp.exp(ge[0:1, :]), v=v[rs])
        return slabs[(b, ch)]

    srow = lax.broadcasted_iota(jnp.int32, (stk, PAIR), 0)
    slane = lax.broadcasted_iota(jnp.int32, (stk, PAIR), 1)
    own_lanes = (srow < c_len) == (slane < A_HEAD_DIM)

    def stack(x):
        return jnp.where(own_lanes, jnp.concatenate([x, x], axis=0), 0.0).astype(BF16)

    mrow = lax.broadcasted_iota(jnp.int32, (stk, stk), 0)
    mcol = lax.broadcasted_iota(jnp.int32, (stk, stk), 1)
    strict = (mrow & (c_len - 1)) > (mcol & (c_len - 1))
    lower = (mrow & (c_len - 1)) >= (mcol & (c_len - 1))
    eye = jnp.where(mrow == mcol, 1.0, 0.0).astype(F32)

    def cut(name, u):
        b, ch, p = u
        return slab(b, ch)[name][:, p * PAIR:(p + 1) * PAIR]

    lhs, v_st, a_ak_v, a_rk, a_rb, t_inv, y_parts = {}, {}, {}, {}, {}, {}, {}

    def scores_and_inverse(units):
        power = {}
        for u in units:
            lhs[u] = jnp.concatenate([stack(cut("aq", u)), stack(cut("rq", u))], axis=0)
            v_st[u] = stack(cut("v", u))
            sc_b = _dot_nt(lhs[u], stack(cut("bn", u)))
            sc_k = _dot_nt(lhs[u], stack(cut("kn", u)))
            power[u] = jnp.where(strict, sc_b[:stk], 0.0)
            a_rb[u] = jnp.where(lower, sc_b[stk:], 0.0)
            a_rk[u] = jnp.where(lower, sc_k[stk:], 0.0)
            a_ak_v[u] = jnp.where(strict, sc_k[:stk], 0.0)
        yield
        for u in units:
            t_inv[u] = eye + power[u]
            a_ak_v[u] = _dot(a_ak_v[u], v_st[u])
            power[u] = _dot(power[u], power[u])
        yield
        for lvl in range(1, _log2(c_len)):
            new_power = {}
            for u in units:
                t_inv[u] = t_inv[u] + _dot(power[u], t_inv[u])
                if lvl < _log2(c_len) - 1:
                    new_power[u] = _dot(power[u], power[u])
            power = new_power
            yield

    def apply_state(units):
        from_state = {u: _dot_nt(lhs[u], s_ref[u[0], u[2]]) for u in units}
        yield
        corr = {u: _dot(t_inv[u], from_state[u][:stk] + a_ak_v[u]) for u in units}
        yield
        for u in units:
            vu = jnp.concatenate([v_st[u], corr[u].astype(BF16)], axis=0)
            if stk % LANES == 0:
                y_st = from_state[u][stk:] + _dot(jnp.concatenate([a_rk[u], a_rb[u]], axis=1), vu)
            else:
                y_st = from_state[u][stk:] + _dot(a_rk[u], v_st[u]) + _dot(a_rb[u], corr[u])
            y_parts[u] = y_st[:c_len] + y_st[c_len:]
            ends = jnp.concatenate([stack(cut("k_end", u)), stack(cut("b_end", u))], axis=0)
            s_ref[u[0], u[2]] = s_ref[u[0], u[2]] * cut("decay_end", u) + _dot_tn(vu, ends)
        yield

    chunk_units = [[(b, ch, p) for b in range(bt) for p in range(N_PAIRS)] for ch in range(nch)]
    for _ in scores_and_inverse(chunk_units[0]):
        yield "dense"
    for ch in range(nch):
        if ch + 1 < nch:
            for _ in _round_robin(scores_and_inverse(chunk_units[ch + 1]), apply_state(chunk_units[ch])):
                yield "dense"
        else:
            for _ in apply_state(chunk_units[ch]):
                yield "tail"

    y = jnp.concatenate(
        [jnp.concatenate([y_parts[(b, ch, p)] for p in range(N_PAIRS)], axis=1)
         for b in range(bt) for ch in range(nch)], axis=0)
    inv_n = 1.0 / A_HEAD_DIM
    mean = seg_sum(y) * inv_n
    bonus = seg_sum(r * k_h * r_k)
    yield "tail"
    dev = y - mean
    var = seg_sum(dev * dev) * inv_n
    yield "tail"
    y = dev * lax.rsqrt(var + GN_EPS) * ln_w + ln_b
    y = ((y + bonus * v) * g).astype(BF16)
    for b in range(bt):
        yg_ref[b] = y[b * frames:(b + 1) * frames]

    @pl.when(c == pl.num_programs(1) - 1)
    def _():
        prow = lax.broadcasted_iota(jnp.int32, (PAIR, A_HEAD_DIM), 0)
        pcol = lax.broadcasted_iota(jnp.int32, (PAIR, A_HEAD_DIM), 1)
        from_lo = jnp.where(prow == pcol, 1.0, 0.0).astype(BF16)
        from_hi = jnp.where(prow == pcol + A_HEAD_DIM, 1.0, 0.0).astype(BF16)
        for b in range(bt):
            for p in range(N_PAIRS):
                s_pair = s_ref[b, p]
                s_out_ref[b, 2 * p] = _dot_x01(s_pair[:A_HEAD_DIM], from_lo)
                s_out_ref[b, 2 * p + 1] = _dot_x01(s_pair[A_HEAD_DIM:], from_hi)


def _hgrn_part(layer, bt, tb, c, h, refs):
    (w_ref, lg_ref, nw_ref, s0_ref, on_ref, s_out_ref, st_ref) = refs
    n_blk = tb // HGRN_BLOCK

    def project(c0):
        return [jnp.dot(h[b * tb:(b + 1) * tb], w_ref[:, c0:c0 + BK], preferred_element_type=F32)
                for b in range(bt)]

    fz_all = project(BK)
    yield
    q_all = project(0)
    yield

    @pl.when(c == 0)
    def _():
        for b in range(bt):
            for hd in range(B_HEADS):
                st_ref[b, hd] = s0_ref[b, hd].T

    lg = lg_ref[...]
    lrows = [lg[i:i + 1, :] for i in range(DEPTH)]
    mx = functools.reduce(jnp.maximum, lrows)
    ex = [jnp.exp(x - mx) for x in lrows]
    den = functools.reduce(lambda a, b: a + b, ex)
    sm = [e / den for e in ex]
    lb = functools.reduce(lambda a, b: a + b, sm[:layer + 1]) - sm[0]

    cs_rows = min(tb, CUMSUM_ROWS)
    row = lax.broadcasted_iota(jnp.int32, (cs_rows, cs_rows), 0)
    col = lax.broadcasted_iota(jnp.int32, (cs_rows, cs_rows), 1)
    blk_shift = _log2(HGRN_BLOCK)
    causal = jnp.logical_and(lax.shift_right_logical(row, blk_shift) == lax.shift_right_logical(col, blk_shift),
                             row >= col)
    causal01 = jnp.where(causal, 1.0, 0.0).astype(BF16)

    f_all = [lb + (1.0 - lb) * _sigmoid(fz_all[b]) for b in range(bt)]
    g_cum_all = [jnp.concatenate([_dot_01(causal01, jnp.log(f_all[b][i:i + cs_rows]))
                                  for i in range(0, tb, cs_rows)], axis=0) for b in range(bt)]
    yield
    iv = project(2 * BK)
    yield

    frow = lax.broadcasted_iota(jnp.int32, (tb, BK), 0)
    second = (lax.shift_right_logical(frow, blk_shift) & 1) == 1
    zeros_blk = jnp.zeros((HGRN_BLOCK, BK), F32)
    sr = min(tb, SCORE_ROWS)
    srow = lax.broadcasted_iota(jnp.int32, (sr, sr), 0)
    scol = lax.broadcasted_iota(jnp.int32, (sr, sr), 1)

    def split_mask(hl):
        grp = _log2(2 * hl)
        return jnp.logical_and(
            lax.shift_right_logical(srow, grp) == lax.shift_right_logical(scol, grp),
            jnp.logical_and((srow & (2 * hl - 1)) >= hl, (scol & (2 * hl - 1)) < hl))

    pair_mask = split_mask(HGRN_BLOCK)

    levels = [HGRN_BLOCK >> (i + 1) for i in range(blk_shift)]
    pos_in = {hl: frow & (2 * hl - 1) for hl in levels}
    level_mask = {hl: split_mask(hl) for hl in levels}
    toward_anchor = {hl: jnp.where(pos_in[hl] >= hl, 1.0, -1.0).astype(F32) for hl in levels}

    def anchor(g, hl):
        if 2 * hl >= 8:
            g3 = g.reshape(tb // (2 * hl), 2 * hl, BK)
            return jnp.broadcast_to(g3[:, hl - 1:hl, :], g3.shape).reshape(tb, BK)
        out = g
        for pos in range(2 * hl):
            if pos != hl - 1:
                out = jnp.where(pos_in[hl] == pos, pltpu.roll(g, (pos - (hl - 1)) % tb, 0), out)
        return out

    q_lvl = {hl: [] for hl in levels}
    k_lvl = {hl: [] for hl in levels}
    own = []
    qg, qg_pair, k_end, k_end_pair, decay_pair = [], [], [], [], []
    for b in range(bt):
        q = q_all[b]
        kx = 1.0 - f_all[b]
        g_cum = g_cum_all[b]
        qs = q * _sigmoid(q)
        own.append(qs * kx)
        for hl in levels:
            partial = jnp.exp((g_cum - anchor(g_cum, hl)) * toward_anchor[hl])
            q_lvl[hl].append((qs * partial).astype(BF16))
            k_lvl[hl].append((kx * partial).astype(BF16))
        g_end = jnp.concatenate(
            [jnp.broadcast_to(g_cum[(j + 1) * HGRN_BLOCK - 1:(j + 1) * HGRN_BLOCK, :], (HGRN_BLOCK, BK))
             for j in range(n_blk)], axis=0)
        decay = jnp.exp(g_end)
        decay_prev = jnp.concatenate([zeros_blk, decay[:-HGRN_BLOCK]], axis=0)
        decay_next = jnp.concatenate([decay[HGRN_BLOCK:], zeros_blk], axis=0)
        q_dec = qs * jnp.exp(g_cum)
        k_to_end = kx * jnp.exp(g_end - g_cum)
        qg.append(q_dec.astype(BF16))
        qg_pair.append((q_dec * jnp.where(second, decay_prev, 1.0)).astype(BF16))
        k_end.append(k_to_end.astype(BF16))
        k_end_pair.append((k_to_end * jnp.where(second, 1.0, decay_next)).astype(BF16))
        decay_pair.append(decay * decay_next)

    units = [(b, hd) for b in range(bt) for hd in range(B_HEADS)]
    n_pair = tb // PAIR_LEN

    def cut(x, u):
        return x[u[0]][:, u[1] * B_KEY_DIM:(u[1] + 1) * B_KEY_DIM]

    def pair(j):
        return slice(j * PAIR_LEN, (j + 1) * PAIR_LEN)

    def tile(i):
        return slice(i * sr, (i + 1) * sr)

    def scores(mask, qx, kx_):
        return {(u, i): jnp.where(mask, _dot_nt(cut(qx, u)[tile(i)], cut(kx_, u)[tile(i)]), 0.0)
                for u in units for i in range(tb // sr)}

    att = scores(pair_mask, qg, k_end)
    yield
    for hl in levels:
        level = scores(level_mask[hl], q_lvl[hl], k_lvl[hl])
        att = {key: att[key] + level[key] for key in att}
        yield
    upd = {}
    for j in range(n_pair):
        for u in units:
            upd[(u, j)] = _dot_tn(cut(iv, u)[pair(j)], cut(k_end_pair, u)[pair(j)])
        yield
    starts = {}
    for u in units:
        s = st_ref[u[0], u[1]]
        dec = cut(decay_pair, u)
        for j in range(n_pair):
            starts[(u, j)] = s.astype(BF16)
            s = s * dec[j * PAIR_LEN:j * PAIR_LEN + 1, :] + upd[(u, j)]
        st_ref[u[0], u[1]] = s
    og = project(2 * BK + DB)
    yield
    o_acc = {u: jnp.concatenate([_dot(att[(u, i)], cut(iv, u)[tile(i)]) for i in range(tb // sr)], axis=0)
             for u in units}
    yield
    inter = {}
    for j in range(n_pair):
        for u in units:
            inter[(u, j)] = _dot_nt(cut(qg_pair, u)[pair(j)], starts[(u, j)])
        yield

    for b in range(bt):
        outs = []
        for hd in range(B_HEADS):
            u = (b, hd)
            o_h = o_acc[u] + jnp.concatenate([inter[(u, j)] for j in range(n_pair)], axis=0)
            o_h = o_h + jnp.sum(cut(own, u), axis=-1, keepdims=True) * cut(iv, u)
            outs.append(o_h * lax.rsqrt(jnp.mean(o_h * o_h, axis=-1, keepdims=True) + RMS_EPS))
        o = jnp.concatenate(outs, axis=1)
        on_ref[b] = (o * nw_ref[...] * (og[b] * _sigmoid(og[b]))).astype(BF16)

    @pl.when(c == pl.num_programs(1) - 1)
    def _():
        for b in range(bt):
            for hd in range(B_HEADS):
                s_out_ref[b, hd] = st_ref[b, hd].T


def _drive(gen):
    for _ in gen:
        pass


def _normalised(x_ref, nm_ref, bt):
    return jnp.concatenate([_rms(x_ref[b], nm_ref[...]).astype(BF16) for b in range(bt)], axis=0)


def _rwkv_kernel(layer, bt, nch, c_len, x_ref, nm_ref, *refs):
    _drive(_rwkv_part(layer, bt, nch, c_len, pl.program_id(1), _normalised(x_ref, nm_ref, bt), refs))


def _hgrn_kernel(layer, bt, tb, x_ref, nm_ref, *refs):
    _drive(_hgrn_part(layer, bt, tb, pl.program_id(1), _normalised(x_ref, nm_ref, bt), refs))


def _rwkv(layer, x, shift0, s_all, vfirst, nm, w_rw_all, mu_p, pv, w2_p, a2_p, g2_p, vw1_p, vw2_p, c_len, nch, bt):
    bsz, t_len, _ = x.shape
    frames = nch * c_len
    const2 = lambda b, c: (0, 0)
    step = lambda b, c: (b, c, 0)
    per_seq3 = lambda b, c: (b, 0, 0)
    per_seq4 = lambda b, c: (b, 0, 0, 0)
    in_specs = [
        pl.BlockSpec((bt, frames, D_MODEL), step),
        pl.BlockSpec((1, D_MODEL), const2),
        pl.BlockSpec((bt, 1, GROUP_COLS), per_seq3),
        pl.BlockSpec((None, bt, A_HEADS, A_HEAD_DIM, A_HEAD_DIM), lambda b, c: (layer, b, 0, 0, 0)),
    ]
    args = [x, nm, shift0, s_all]
    if layer > 0:
        in_specs.append(pl.BlockSpec((bt, frames, DA), step))
        args.append(vfirst)
    in_specs += [
        _resident((None, D_MODEL, GROUP_COLS), lambda b, c: (layer, 0, 0)),
        pl.BlockSpec((1, GROUP_COLS), const2),
        pl.BlockSpec((8, DA), const2),
        pl.BlockSpec((LANES, DA), const2),
        pl.BlockSpec((LANES, DA), const2),
        pl.BlockSpec((LORA_G_PAD, DA), const2),
    ]
    args += [w_rw_all, mu_p, pv, w2_p, a2_p, g2_p]
    if layer > 0:
        in_specs += [pl.BlockSpec((DA, LANES), const2), pl.BlockSpec((LANES, DA), const2)]
        args += [vw1_p, vw2_p]
    out_specs = [
        pl.BlockSpec((bt, frames, DA), step),
        pl.BlockSpec((bt, 1, GROUP_COLS), per_seq3),
        pl.BlockSpec((bt, A_HEADS, A_HEAD_DIM, A_HEAD_DIM), per_seq4),
    ]
    out_shape = [
        jax.ShapeDtypeStruct((bsz, t_len, DA), BF16),
        jax.ShapeDtypeStruct((bsz, 1, GROUP_COLS), F32),
        jax.ShapeDtypeStruct((bsz, A_HEADS, A_HEAD_DIM, A_HEAD_DIM), F32),
    ]
    if layer == 0:
        out_specs.append(pl.BlockSpec((bt, frames, DA), step))
        out_shape.append(jax.ShapeDtypeStruct((bsz, t_len, DA), F32))
    return pl.pallas_call(
        functools.partial(_rwkv_kernel, layer, bt, nch, c_len),
        grid=(bsz // bt, t_len // frames),
        in_specs=in_specs,
        out_specs=out_specs,
        out_shape=out_shape,
        scratch_shapes=[pltpu.VMEM((bt, frames + 8, GROUP_COLS), F32),
                        pltpu.VMEM((bt, N_PAIRS, PAIR, PAIR), F32)],
        compiler_params=pltpu.CompilerParams(
            dimension_semantics=("arbitrary", "arbitrary"), vmem_limit_bytes=VMEM_LIMIT),
        name=f"rwkv{layer}",
    )(*args)


def _hgrn(layer, x, nm, w_hg_all, lb_logits, norm_w, s_all, tb, bt):
    bsz, t_len, _ = x.shape
    const2 = lambda b, t: (0, 0)
    return pl.pallas_call(
        functools.partial(_hgrn_kernel, layer, bt, tb),
        grid=(bsz // bt, t_len // tb),
        in_specs=[
            pl.BlockSpec((bt, tb, D_MODEL), lambda b, t: (b, t, 0)),
            pl.BlockSpec((1, D_MODEL), const2),
            _resident((None, D_MODEL, HGRN_COLS), lambda b, t: (layer, 0, 0)),
            pl.BlockSpec((DEPTH, BK), const2),
            pl.BlockSpec((1, DB), const2),
            pl.BlockSpec((None, bt, B_HEADS, B_KEY_DIM, B_VAL_DIM), lambda b, t: (layer, b, 0, 0, 0)),
        ],
        out_specs=[
            pl.BlockSpec((bt, tb, DB), lambda b, t: (b, t, 0)),
            pl.BlockSpec((bt, B_HEADS, B_KEY_DIM, B_VAL_DIM), lambda b, t: (b, 0, 0, 0)),
        ],
        out_shape=[
            jax.ShapeDtypeStruct((bsz, t_len, DB), BF16),
            jax.ShapeDtypeStruct((bsz, B_HEADS, B_KEY_DIM, B_VAL_DIM), F32),
        ],
        scratch_shapes=[pltpu.VMEM((bt, B_HEADS, B_VAL_DIM, B_KEY_DIM), F32)],
        compiler_params=pltpu.CompilerParams(
            dimension_semantics=("arbitrary", "arbitrary"), vmem_limit_bytes=VMEM_LIMIT),
        name=f"hgrn{layer}",
    )(x, nm, w_hg_all, lb_logits, norm_w, s_all)


FF_CHUNK = 1024


def _mix_ffn_kernel(final, *refs):
    if final:
        (x_ref, yg_ref, on_ref, nm_ref, wg_ref, wa_ref, wb_ref, wo_ref, nf_ref, up_ref, dn_ref, nfin_ref,
         o_ref) = refs
    else:
        (x_ref, yg_ref, on_ref, nm_ref, wg_ref, wa_ref, wb_ref, wo_ref, nf_ref, up_ref, dn_ref, o_ref) = refs
    x = x_ref[...]
    gt = _sigmoid(jnp.dot(_rms(x, nm_ref[...]).astype(BF16), wg_ref[...], preferred_element_type=F32))
    ya = jnp.dot(yg_ref[...], wa_ref[...], preferred_element_type=F32)
    yb = jnp.dot(on_ref[...], wb_ref[...], preferred_element_type=F32)
    merged = gt[:, 0:D_MODEL] * ya + gt[:, D_MODEL:] * yb
    x = x + jnp.dot(merged.astype(BF16), wo_ref[...], preferred_element_type=F32)
    h = _rms(x, nf_ref[...]).astype(BF16)
    acc = x
    for cf in range(D_FF // FF_CHUNK):
        u = jnp.dot(h, up_ref[:, cf * FF_CHUNK:(cf + 1) * FF_CHUNK], preferred_element_type=F32)
        u = jnp.maximum(u, 0.0)
        acc = acc + jnp.dot((u * u).astype(BF16), dn_ref[cf * FF_CHUNK:(cf + 1) * FF_CHUNK, :],
                            preferred_element_type=F32)
    if final:
        acc = _rms(acc, nfin_ref[...])
    o_ref[...] = acc


def _mix_ffn(layer, x, yg, on, nm, nf, nfin, w):
    final = layer == DEPTH - 1
    n = x.shape[0]
    tm = min(n, 512)
    small = lambda i: (0, 0)
    mine = lambda i: (layer, 0, 0)
    in_specs = [
        pl.BlockSpec((tm, D_MODEL), lambda i: (i, 0)),
        pl.BlockSpec((tm, DA), lambda i: (i, 0)),
        pl.BlockSpec((tm, DB), lambda i: (i, 0)),
        _resident((1, D_MODEL), small),
        _resident((None, D_MODEL, GATE_COLS), mine),
        _resident((None, DA, D_MODEL), mine),
        _resident((None, DB, D_MODEL), mine),
        _resident((None, D_MODEL, D_MODEL), mine),
        _resident((1, D_MODEL), small),
        _resident((None, D_MODEL, D_FF), mine),
        _resident((None, D_FF, D_MODEL), mine),
    ]
    args = [x, yg, on, nm, w["wg"], w["wa"], w["wb"], w["wo"], nf, w["up"], w["dn"]]
    if final:
        in_specs.append(_resident((1, D_MODEL), small))
        args.append(nfin)
    return pl.pallas_call(
        functools.partial(_mix_ffn_kernel, final),
        grid=(n // tm,),
        in_specs=in_specs,
        out_specs=pl.BlockSpec((tm, D_MODEL), lambda i: (i, 0)),
        out_shape=jax.ShapeDtypeStruct((n, D_MODEL), F32),
        compiler_params=pltpu.CompilerParams(dimension_semantics=("arbitrary",), vmem_limit_bytes=VMEM_LIMIT),
        name="mix_ffn_final" if final else "mix_ffn",
    )(*args)


def _pad_rows(w, rows, at=0):
    out = jnp.zeros((rows, w.shape[1]), w.dtype)
    return out.at[at:at + w.shape[0]].set(w)


def _prep_weights(p):
    w_in = p["w_in"]
    pad = ((0, 0), (0, 0), (0, GROUP_COLS - RWKV_COLS))
    return dict(
        w_rw=jnp.pad(w_in[:, :, :RWKV_COLS], pad).astype(BF16),
        w_hg=w_in[:, :, RWKV_COLS:RWKV_COLS + HGRN_COLS].astype(BF16),
        wg=w_in[:, :, RWKV_COLS + HGRN_COLS:].astype(BF16),
        wa=p["w_out_a"].astype(BF16), wb=p["w_out_b"].astype(BF16), wo=p["w_out"].astype(BF16),
        up=p["w_ffn_up"].astype(BF16), dn=p["w_ffn_down"].astype(BF16),
    )


def _prep_layer(l, p):
    mu_p = jnp.zeros((1, GROUP_COLS), F32).at[0, :RWKV_COLS].set(p["rwkv_mu"][l])
    v0 = p["rwkv_v0"][l - 1] if l > 0 else jnp.zeros((DA,), F32)
    pv = jnp.stack([p["rwkv_w0"][l], p["rwkv_a0"][l], p["rwkv_k_k"][l], p["rwkv_k_a"][l],
                    p["rwkv_r_k"][l].reshape(DA), p["rwkv_ln_w"][l], p["rwkv_ln_b"][l], v0])
    out = dict(
        mu_p=mu_p, pv=pv,
        w2_p=_pad_rows(p["rwkv_w2"][l], LANES, 0).astype(BF16),
        a2_p=_pad_rows(p["rwkv_a2"][l], LANES, W_LORA).astype(BF16),
        g2_p=_pad_rows(p["rwkv_g2"][l], LORA_G_PAD, 0).astype(BF16),
        vw1_p=None, vw2_p=None,
        norm_mix=p["norm_mix"][l].reshape(1, D_MODEL),
        hgrn_norm_w=p["hgrn_norm_w"][l].reshape(1, DB),
        nf=p["norm_ffn"][l].reshape(1, D_MODEL),
    )
    if l > 0:
        w1 = p["rwkv_vres_w1"][l - 1]
        out["vw1_p"] = jnp.zeros((DA, LANES), F32).at[:, :V_LORA].set(w1).astype(BF16)
        out["vw2_p"] = _pad_rows(p["rwkv_vres_w2"][l - 1], LANES, 0).astype(BF16)
    return out


def _seq_tile(bsz, frames, step_rows):
    bt = min(bsz, max(1, step_rows // frames))
    assert bsz % bt == 0, (bsz, bt)
    return bt


def _trunk(x, state_shift, state_rwkv, state_hgrn, layers, weights, p):
    bsz, t_len, _ = x.shape
    c_len = min(t_len, RWKV_CHUNK)
    nch = min(t_len // c_len, RWKV_CHUNKS)
    tb = min(t_len, HGRN_TILE)
    assert t_len % (nch * c_len) == 0 and t_len % tb == 0 and tb % PAIR_LEN == 0
    nfin = p["norm_final"].reshape(1, D_MODEL)
    vfirst = None
    shifts, rwkv_states, hgrn_states = [], [], []
    for l in range(DEPTH):
        lp = layers[l]
        shift0 = jnp.zeros((bsz, 1, GROUP_COLS), F32).at[:, 0, :RWKV_COLS].set(state_shift[l])
        res = _rwkv(l, x, shift0, state_rwkv, vfirst, lp["norm_mix"], weights["w_rw"], lp["mu_p"], lp["pv"],
                    lp["w2_p"], lp["a2_p"], lp["g2_p"], lp["vw1_p"], lp["vw2_p"], c_len, nch,
                    min(_seq_tile(bsz, nch * c_len, RWKV_STEP_ROWS), RWKV_STEP_UNITS // (nch * N_PAIRS)))
        if l == 0:
            yg, shift_out, s_rwkv, vfirst = res
        else:
            yg, shift_out, s_rwkv = res
        on, s_hgrn = _hgrn(l, x, lp["norm_mix"], weights["w_hg"], p["hgrn_lb_logits"], lp["hgrn_norm_w"],
                           state_hgrn, tb, min(_seq_tile(bsz, tb, HGRN_STEP_ROWS), HGRN_STEP_UNITS // B_HEADS))
        n = bsz * t_len
        x = _mix_ffn(l, x.reshape(n, D_MODEL), yg.reshape(n, DA), on.reshape(n, DB), lp["norm_mix"], lp["nf"],
                     nfin, weights).reshape(bsz, t_len, D_MODEL)
        shifts.append(shift_out[:, 0, :RWKV_COLS])
        rwkv_states.append(s_rwkv)
        hgrn_states.append(s_hgrn)
    return x, jnp.stack(shifts), jnp.stack(rwkv_states), jnp.stack(hgrn_states)


def kernel(x_prompt, x_sample, state_shift, state_rwkv, state_hgrn, norm_mix, w_in, rwkv_mu, rwkv_w0, rwkv_w2, rwkv_a0, rwkv_a2, rwkv_g2, rwkv_v0, rwkv_vres_w1, rwkv_vres_w2, rwkv_k_k, rwkv_k_a, rwkv_r_k, rwkv_ln_w, rwkv_ln_b, hgrn_lb_logits, hgrn_norm_w, w_out_a, w_out_b, w_out, norm_ffn, w_ffn_up, w_ffn_down, norm_final):
    p = dict(norm_mix=norm_mix, w_in=w_in, rwkv_mu=rwkv_mu, rwkv_w0=rwkv_w0, rwkv_w2=rwkv_w2, rwkv_a0=rwkv_a0,
             rwkv_a2=rwkv_a2, rwkv_g2=rwkv_g2, rwkv_v0=rwkv_v0, rwkv_vres_w1=rwkv_vres_w1,
             rwkv_vres_w2=rwkv_vres_w2, rwkv_k_k=rwkv_k_k, rwkv_k_a=rwkv_k_a, rwkv_r_k=rwkv_r_k,
             rwkv_ln_w=rwkv_ln_w, rwkv_ln_b=rwkv_ln_b, hgrn_lb_logits=hgrn_lb_logits, hgrn_norm_w=hgrn_norm_w,
             w_out_a=w_out_a, w_out_b=w_out_b, w_out=w_out, norm_ffn=norm_ffn, w_ffn_up=w_ffn_up,
             w_ffn_down=w_ffn_down, norm_final=norm_final)
    layers = [_prep_layer(l, p) for l in range(DEPTH)]
    weights = _prep_weights(p)
    bp = x_prompt.shape[0]
    dt = x_prompt.dtype
    zero_shift = jnp.zeros((DEPTH, bp, RWKV_COLS), dt)
    zero_rwkv = jnp.zeros((DEPTH, bp, A_HEADS, A_HEAD_DIM, A_HEAD_DIM), dt)
    zero_hgrn = jnp.zeros((DEPTH, bp, B_HEADS, B_KEY_DIM, B_VAL_DIM), dt)
    y_prompt, shift_p, rwkv_p, hgrn_p = _trunk(x_prompt, zero_shift, zero_rwkv, zero_hgrn, layers, weights, p)
    y_sample, shift_s, rwkv_s, hgrn_s = _trunk(x_sample, state_shift, state_rwkv, state_hgrn, layers, weights, p)
    return (y_prompt, y_sample, shift_p, rwkv_p, hgrn_p, shift_s, rwkv_s, hgrn_s)
```

```python
import functools
import math

import jax
import jax.numpy as jnp
from jax import lax
from jax.experimental import pallas as pl
from jax.experimental.pallas import tpu as pltpu

F32 = jnp.float32
BF16 = jnp.bfloat16

D_MODEL = 1024
DEPTH = 2
A_HEADS = 8
A_HEAD_DIM = 64
DA = A_HEADS * A_HEAD_DIM
W_LORA = 64
A_LORA = 64
V_LORA = 32
G_LORA = 160
RWKV_COLS = 3 * DA + W_LORA + A_LORA + G_LORA
B_HEADS = 4
B_KEY_DIM = 128
B_VAL_DIM = 128
BK = B_HEADS * B_KEY_DIM
DB = B_HEADS * B_VAL_DIM
HGRN_COLS = 2 * BK + 2 * DB
GATE_COLS = 2 * D_MODEL
D_FF = 4 * D_MODEL
HGRN_BLOCK = 16
RMS_EPS = 1e-6
GN_EPS = 64e-5

LANES = 128
GROUP_COLS = 2048
LORA_WA = 3 * DA
LORA_G = LORA_WA + W_LORA + A_LORA
LORA_G_PAD = 256
PAIR = 2 * A_HEAD_DIM
N_PAIRS = A_HEADS // 2
PAIR_LEN = 2 * HGRN_BLOCK
STATE_LEN = 2 * PAIR_LEN
VMEM_LIMIT = 56 * 1024 * 1024

RWKV_CHUNK = 64
RWKV_CHUNKS = 4
RWKV_STEP_ROWS = 512
RWKV_STEP_UNITS = 32
CUMSUM_ROWS = 256
HGRN_TILE = 512
HGRN_STEP_ROWS = 1024
HGRN_STEP_UNITS = 64
SCORE_ROWS = 128


def _dot(a, b):
    return jnp.dot(a.astype(BF16), b.astype(BF16), preferred_element_type=F32)


def _dot_nt(a, b):
    return lax.dot_general(a.astype(BF16), b.astype(BF16), (((1,), (1,)), ((), ())), preferred_element_type=F32)


def _dot_tn(a, b):
    return lax.dot_general(a.astype(BF16), b.astype(BF16), (((0,), (0,)), ((), ())), preferred_element_type=F32)


def _split3(x):
    hi = x.astype(BF16)
    r1 = x - hi.astype(F32)
    mid = r1.astype(BF16)
    return hi, mid, (r1 - mid.astype(F32)).astype(BF16)


def _dot_01(m01, x):
    return functools.reduce(lambda a, b: a + b, [jnp.dot(m01, t, preferred_element_type=F32) for t in _split3(x)])


def _dot_x01(x, m01):
    return functools.reduce(lambda a, b: a + b, [jnp.dot(t, m01, preferred_element_type=F32) for t in _split3(x)])


def _rms(x, w):
    return x * lax.rsqrt(jnp.mean(x * x, axis=-1, keepdims=True) + RMS_EPS) * w


def _sigmoid(x):
    return 1.0 / (1.0 + jnp.exp(-x))


def _log2(n):
    assert n & (n - 1) == 0, n
    return n.bit_length() - 1


def _resident(shape, index_map):
    return pl.BlockSpec(shape, index_map, pipeline_mode=pl.Buffered(1))


_DONE = object()


def _round_robin(*gens):
    live = list(gens)
    while live:
        for gen in list(live):
            if next(gen, _DONE) is _DONE:
                live.remove(gen)
            else:
                yield


def _rwkv_part(layer, bt, nch, c_len, c, h, refs):
    if layer == 0:
        (shift_ref, s0_ref, w_ref, mu_ref, pv_ref, w2_ref, a2_ref, g2_ref,
         yg_ref, shift_out_ref, s_out_ref, vfirst_out_ref, xs_ref, s_ref) = refs
    else:
        (shift_ref, s0_ref, vfirst_ref, w_ref, mu_ref, pv_ref, w2_ref, a2_ref, g2_ref, vw1_ref, vw2_ref,
         yg_ref, shift_out_ref, s_out_ref, xs_ref, s_ref) = refs
    frames = nch * c_len
    rows = bt * frames
    stk = 2 * c_len

    hrow = lax.broadcasted_iota(jnp.int32, (A_HEAD_DIM, PAIR), 0)
    hlane = lax.broadcasted_iota(jnp.int32, (A_HEAD_DIM, PAIR), 1)
    to_lo = jnp.where(hlane == hrow, 1.0, 0.0).astype(BF16)
    to_hi = jnp.where(hlane == hrow + A_HEAD_DIM, 1.0, 0.0).astype(BF16)

    @pl.when(c == 0)
    def _():
        for b in range(bt):
            xs_ref[b, 7:8, :] = shift_ref[b]
            for p in range(N_PAIRS):
                s_ref[b, p] = jnp.concatenate(
                    [_dot_x01(s0_ref[b, 2 * p], to_lo), _dot_x01(s0_ref[b, 2 * p + 1], to_hi)], axis=0)

    def project(c0, c1):
        rw = jnp.dot(h, w_ref[:, c0:c1], preferred_element_type=F32)
        prevs = []
        for b in range(bt):
            rw_b = rw[b * frames:(b + 1) * frames]
            xs_ref[b, 8:8 + frames, c0:c1] = rw_b
            prevs.append(xs_ref[b, 7:7 + frames, c0:c1])
            last = rw_b[frames - 1:frames, :]
            xs_ref[b, 7:8, c0:c1] = last
            shift_out_ref[b, :, c0:c1] = last
        return rw + (jnp.concatenate(prevs, axis=0) - rw) * mu_ref[:, c0:c1]

    w0 = pv_ref[0:1, :]
    a0 = pv_ref[1:2, :]
    k_k = pv_ref[2:3, :]
    k_a = pv_ref[3:4, :]
    r_k = pv_ref[4:5, :]
    ln_w = pv_ref[5:6, :]
    ln_b = pv_ref[6:7, :]

    cs_rows = min(rows, CUMSUM_ROWS)
    row = lax.broadcasted_iota(jnp.int32, (cs_rows, cs_rows), 0)
    col = lax.broadcasted_iota(jnp.int32, (cs_rows, cs_rows), 1)
    chunk_shift = _log2(c_len)
    causal01 = jnp.where(jnp.logical_and(
        lax.shift_right_logical(row, chunk_shift) == lax.shift_right_logical(col, chunk_shift), row >= col),
        1.0, 0.0).astype(BF16)

    lora = project(LORA_WA, GROUP_COLS)
    yield "prep"
    k = project(DA, 2 * DA)
    yield "prep"
    wa = lora[:, 0:LORA_G - LORA_WA]
    gl = lora[:, LORA_G - LORA_WA:LORA_G - LORA_WA + LORA_G_PAD]
    w_raw = w0 + _dot(jnp.tanh(wa), w2_ref[...])
    a_gate = _sigmoid(a0 + _dot(wa, a2_ref[...]))
    g = _dot(_sigmoid(gl), g2_ref[...])
    yield "prep"
    r = project(0, DA)
    yield "prep"
    lw = (-math.exp(-0.5)) * _sigmoid(w_raw)
    g_cum = jnp.concatenate([_dot_01(causal01, lw[i:i + cs_rows]) for i in range(0, rows, cs_rows)],
                            axis=0)
    yield "prep"
    v = project(2 * DA, 3 * DA)
    yield "prep"
    if layer == 0:
        for b in range(bt):
            vfirst_out_ref[b] = v[b * frames:(b + 1) * frames]
    else:
        v0 = pv_ref[7:8, :]
        vg = _sigmoid(v0 + _dot(_dot(v, vw1_ref[...]), vw2_ref[...]))
        vfirst = jnp.concatenate([vfirst_ref[b] for b in range(bt)], axis=0)
        v = v + (vfirst - v) * vg

    lane = lax.broadcasted_iota(jnp.int32, (PAIR, PAIR), 1)
    sub = lax.broadcasted_iota(jnp.int32, (PAIR, PAIR), 0)
    head_ones = jnp.where((lane < A_HEAD_DIM) == (sub < A_HEAD_DIM), 1.0, 0.0).astype(BF16)

    def seg_sum(x):
        return jnp.concatenate(
            [jnp.dot(x[:, p * PAIR:(p + 1) * PAIR].astype(BF16), head_ones, preferred_element_type=F32)
             for p in range(N_PAIRS)], axis=1)

    kk = k * k_k
    kk = kk * lax.rsqrt(jnp.maximum(seg_sum(kk * kk), 1e-24))
    yield "prep"
    k_h = k * (1.0 + (a_gate - 1.0) * k_a)
    a_vec = -kk
    b_vec = kk * a_gate

    slabs = {}

    def slab(b, ch):
        if (b, ch) not in slabs:
            r0 = (b * nch + ch) * c_len
            rs = slice(r0, r0 + c_len)
            gc = g_cum[rs]
            ge = jnp.broadcast_to(gc[c_len - 1:c_len, :], (c_len, DA))
            e_neg = jnp.exp(-gc)
            e_end = jnp.exp(ge - gc)
            slabs[(b, ch)] = dict(
                rq=r[rs] * jnp.exp(gc), aq=a_vec[rs] * jnp.exp(gc - lw[rs]), kn=k_h[rs] * e_neg,
                bn=b_vec[rs] * e_neg, k_end=k_h[rs] * e_end, b_end=b_vec[rs] * e_end,
                decay_end=jnp.exp(ge[0:1, :]), v=v[rs])
        return slabs[(b, ch)]

    srow = lax.broadcasted_iota(jnp.int32, (stk, PAIR), 0)
    slane = lax.broadcasted_iota(jnp.int32, (stk, PAIR), 1)
    own_lanes = (srow < c_len) == (slane < A_HEAD_DIM)

    def stack(x):
        return jnp.where(own_lanes, jnp.concatenate([x, x], axis=0), 0.0).astype(BF16)

    mrow = lax.broadcasted_iota(jnp.int32, (stk, stk), 0)
    mcol = lax.broadcasted_iota(jnp.int32, (stk, stk), 1)
    strict = (mrow & (c_len - 1)) > (mcol & (c_len - 1))
    lower = (mrow & (c_len - 1)) >= (mcol & (c_len - 1))
    eye = jnp.where(mrow == mcol, 1.0, 0.0).astype(F32)

    def cut(name, u):
        b, ch, p = u
        return slab(b, ch)[name][:, p * PAIR:(p + 1) * PAIR]

    lhs, v_st, a_ak_v, a_rk, a_rb, t_inv, y_parts = {}, {}, {}, {}, {}, {}, {}

    def scores_and_inverse(units):
        power = {}
        for u in units:
            lhs[u] = jnp.concatenate([stack(cut("aq", u)), stack(cut("rq", u))], axis=0)
            v_st[u] = stack(cut("v", u))
            sc_b = _dot_nt(lhs[u], stack(cut("bn", u)))
            sc_k = _dot_nt(lhs[u], stack(cut("kn", u)))
            power[u] = jnp.where(strict, sc_b[:stk], 0.0)
            a_rb[u] = jnp.where(lower, sc_b[stk:], 0.0)
            a_rk[u] = jnp.where(lower, sc_k[stk:], 0.0)
            a_ak_v[u] = jnp.where(strict, sc_k[:stk], 0.0)
        yield
        for u in units:
            t_inv[u] = eye + power[u]
            a_ak_v[u] = _dot(a_ak_v[u], v_st[u])
            power[u] = _dot(power[u], power[u])
        yield
        for lvl in range(1, _log2(c_len)):
            new_power = {}
            for u in units:
                t_inv[u] = t_inv[u] + _dot(power[u], t_inv[u])
                if lvl < _log2(c_len) - 1:
                    new_power[u] = _dot(power[u], power[u])
            power = new_power
            yield

    def apply_state(units):
        from_state = {u: _dot_nt(lhs[u], s_ref[u[0], u[2]]) for u in units}
        yield
        corr = {u: _dot(t_inv[u], from_state[u][:stk] + a_ak_v[u]) for u in units}
        yield
        for u in units:
            vu = jnp.concatenate([v_st[u], corr[u].astype(BF16)], axis=0)
            if stk % LANES == 0:
                y_st = from_state[u][stk:] + _dot(jnp.concatenate([a_rk[u], a_rb[u]], axis=1), vu)
            else:
                y_st = from_state[u][stk:] + _dot(a_rk[u], v_st[u]) + _dot(a_rb[u], corr[u])
            y_parts[u] = y_st[:c_len] + y_st[c_len:]
            ends = jnp.concatenate([stack(cut("k_end", u)), stack(cut("b_end", u))], axis=0)
            s_ref[u[0], u[2]] = s_ref[u[0], u[2]] * cut("decay_end", u) + _dot_tn(vu, ends)
        yield

    chunk_units = [[(b, ch, p) for b in range(bt) for p in range(N_PAIRS)] for ch in range(nch)]
    for _ in scores_and_inverse(chunk_units[0]):
        yield "dense"
    for ch in range(nch):
        if ch + 1 < nch:
            for _ in _round_robin(scores_and_inverse(chunk_units[ch + 1]), apply_state(chunk_units[ch])):
                yield "dense"
        else:
            for _ in apply_state(chunk_units[ch]):
                yield "tail"

    y = jnp.concatenate(
        [jnp.concatenate([y_parts[(b, ch, p)] for p in range(N_PAIRS)], axis=1)
         for b in range(bt) for ch in range(nch)], axis=0)
    inv_n = 1.0 / A_HEAD_DIM
    mean = seg_sum(y) * inv_n
    bonus = seg_sum(r * k_h * r_k)
    yield "tail"
    dev = y - mean
    var = seg_sum(dev * dev) * inv_n
    yield "tail"
    y = dev * lax.rsqrt(var + GN_EPS) * ln_w + ln_b
    y = ((y + bonus * v) * g).astype(BF16)
    for b in range(bt):
        yg_ref[b] = y[b * frames:(b + 1) * frames]

    @pl.when(c == pl.num_programs(1) - 1)
    def _():
        prow = lax.broadcasted_iota(jnp.int32, (PAIR, A_HEAD_DIM), 0)
        pcol = lax.broadcasted_iota(jnp.int32, (PAIR, A_HEAD_DIM), 1)
        from_lo = jnp.where(prow == pcol, 1.0, 0.0).astype(BF16)
        from_hi = jnp.where(prow == pcol + A_HEAD_DIM, 1.0, 0.0).astype(BF16)
        for b in range(bt):
            for p in range(N_PAIRS):
                s_pair = s_ref[b, p]
                s_out_ref[b, 2 * p] = _dot_x01(s_pair[:A_HEAD_DIM], from_lo)
                s_out_ref[b, 2 * p + 1] = _dot_x01(s_pair[A_HEAD_DIM:], from_hi)


def _hgrn_part(layer, bt, tb, c, h, refs):
    (w_ref, lg_ref, nw_ref, s0_ref, on_ref, s_out_ref, st_ref) = refs
    n_blk = tb // HGRN_BLOCK

    def project(c0):
        return [jnp.dot(h[b * tb:(b + 1) * tb], w_ref[:, c0:c0 + BK], preferred_element_type=F32)
                for b in range(bt)]

    fz_all = project(BK)
    yield
    q_all = project(0)
    yield

    @pl.when(c == 0)
    def _():
        for b in range(bt):
            for hd in range(B_HEADS):
                st_ref[b, hd] = s0_ref[b, hd].T

    lg = lg_ref[...]
    lrows = [lg[i:i + 1, :] for i in range(DEPTH)]
    mx = functools.reduce(jnp.maximum, lrows)
    ex = [jnp.exp(x - mx) for x in lrows]
    den = functools.reduce(lambda a, b: a + b, ex)
    sm = [e / den for e in ex]
    lb = functools.reduce(lambda a, b: a + b, sm[:layer + 1]) - sm[0]

    cs_rows = min(tb, CUMSUM_ROWS)
    row = lax.broadcasted_iota(jnp.int32, (cs_rows, cs_rows), 0)
    col = lax.broadcasted_iota(jnp.int32, (cs_rows, cs_rows), 1)
    blk_shift = _log2(HGRN_BLOCK)
    causal = jnp.logical_and(lax.shift_right_logical(row, blk_shift) == lax.shift_right_logical(col, blk_shift),
                             row >= col)
    causal01 = jnp.where(causal, 1.0, 0.0).astype(BF16)

    f_all = [lb + (1.0 - lb) * _sigmoid(fz_all[b]) for b in range(bt)]
    g_cum_all = [jnp.concatenate([_dot_01(causal01, jnp.log(f_all[b][i:i + cs_rows]))
                                  for i in range(0, tb, cs_rows)], axis=0) for b in range(bt)]
    yield
    iv = project(2 * BK)
    yield

    state_len = min(tb, STATE_LEN)
    two_pairs = state_len == 2 * PAIR_LEN
    frow = lax.broadcasted_iota(jnp.int32, (tb, BK), 0)
    second = (lax.shift_right_logical(frow, blk_shift) & 1) == 1
    second_pair = (lax.shift_right_logical(frow, blk_shift + 1) & 1) == 1
    zeros_blk = jnp.zeros((HGRN_BLOCK, BK), F32)
    zeros_pair = jnp.zeros((PAIR_LEN, BK), F32)
    sr = min(tb, SCORE_ROWS)
    srow = lax.broadcasted_iota(jnp.int32, (sr, sr), 0)
    scol = lax.broadcasted_iota(jnp.int32, (sr, sr), 1)

    def split_mask(hl):
        grp = _log2(2 * hl)
        return jnp.logical_and(
            lax.shift_right_logical(srow, grp) == lax.shift_right_logical(scol, grp),
            jnp.logical_and((srow & (2 * hl - 1)) >= hl, (scol & (2 * hl - 1)) < hl))

    pair_mask = split_mask(HGRN_BLOCK)

    levels = [HGRN_BLOCK >> (i + 1) for i in range(blk_shift)]
    pos_in = {hl: frow & (2 * hl - 1) for hl in levels}
    level_mask = {hl: split_mask(hl) for hl in levels}
    toward_anchor = {hl: jnp.where(pos_in[hl] >= hl, 1.0, -1.0).astype(F32) for hl in levels}

    def anchor(g, hl):
        if 2 * hl >= 8:
            g3 = g.reshape(tb // (2 * hl), 2 * hl, BK)
            return jnp.broadcast_to(g3[:, hl - 1:hl, :], g3.shape).reshape(tb, BK)
        out = g
        for pos in range(2 * hl):
            if pos != hl - 1:
                out = jnp.where(pos_in[hl] == pos, pltpu.roll(g, (pos - (hl - 1)) % tb, 0), out)
        return out

    q_lvl = {hl: [] for hl in levels}
    k_lvl = {hl: [] for hl in levels}
    own = []
    qg, qg_pair, k_end, k_end_pair, q_state, k_state, decay_state = [], [], [], [], [], [], []
    for b in range(bt):
        q = q_all[b]
        kx = 1.0 - f_all[b]
        g_cum = g_cum_all[b]
        qs = q * _sigmoid(q)
        own.append(qs * kx)
        for hl in levels:
            partial = jnp.exp((g_cum - anchor(g_cum, hl)) * toward_anchor[hl])
            q_lvl[hl].append((qs * partial).astype(BF16))
            k_lvl[hl].append((kx * partial).astype(BF16))
        g_end = jnp.concatenate(
            [jnp.broadcast_to(g_cum[(j + 1) * HGRN_BLOCK - 1:(j + 1) * HGRN_BLOCK, :], (HGRN_BLOCK, BK))
             for j in range(n_blk)], axis=0)
        decay = jnp.exp(g_end)
        decay_prev = jnp.concatenate([zeros_blk, decay[:-HGRN_BLOCK]], axis=0)
        decay_next = jnp.concatenate([decay[HGRN_BLOCK:], zeros_blk], axis=0)
        q_dec = qs * jnp.exp(g_cum)
        k_to_end = kx * jnp.exp(g_end - g_cum)
        qg.append(q_dec.astype(BF16))
        q_pair = q_dec * jnp.where(second, decay_prev, 1.0)
        k_pair = k_to_end * jnp.where(second, 1.0, decay_next)
        qg_pair.append(q_pair.astype(BF16))
        k_end.append(k_to_end.astype(BF16))
        k_end_pair.append(k_pair.astype(BF16))
        pair_tot = decay * jnp.where(second, decay_prev, decay_next)
        if two_pairs:
            tot_prev = jnp.concatenate([zeros_pair, pair_tot[:-PAIR_LEN]], axis=0)
            tot_next = jnp.concatenate([pair_tot[PAIR_LEN:], zeros_pair], axis=0)
            q_state.append((q_pair * jnp.where(second_pair, tot_prev, 1.0)).astype(BF16))
            k_state.append((k_pair * jnp.where(second_pair, 1.0, tot_next)).astype(BF16))
            decay_state.append(pair_tot * tot_next)
        else:
            q_state.append(qg_pair[-1])
            k_state.append(k_end_pair[-1])
            decay_state.append(pair_tot)

    units = [(b, hd) for b in range(bt) for hd in range(B_HEADS)]
    n_state = tb // state_len

    def cut(x, u):
        return x[u[0]][:, u[1] * B_KEY_DIM:(u[1] + 1) * B_KEY_DIM]

    def span(j):
        return slice(j * state_len, (j + 1) * state_len)

    def tile(i):
        return slice(i * sr, (i + 1) * sr)

    def scores(mask, qx, kx_):
        return {(u, i): jnp.where(mask, _dot_nt(cut(qx, u)[tile(i)], cut(kx_, u)[tile(i)]), 0.0)
                for u in units for i in range(tb // sr)}

    att = scores(pair_mask, qg, k_end)
    yield
    if two_pairs:
        level = scores(split_mask(PAIR_LEN), qg_pair, k_end_pair)
        att = {key: att[key] + level[key] for key in att}
        yield
    for hl in levels:
        level = scores(level_mask[hl], q_lvl[hl], k_lvl[hl])
        att = {key: att[key] + level[key] for key in att}
        yield
    upd = {}
    for j in range(n_state):
        for u in units:
            upd[(u, j)] = _dot_tn(cut(iv, u)[span(j)], cut(k_state, u)[span(j)])
        yield
    starts = {}
    for u in units:
        s = st_ref[u[0], u[1]]
        dec = cut(decay_state, u)
        for j in range(n_state):
            starts[(u, j)] = s.astype(BF16)
            s = s * dec[j * state_len:j * state_len + 1, :] + upd[(u, j)]
        st_ref[u[0], u[1]] = s
    og = project(2 * BK + DB)
    yield
    o_acc = {u: jnp.concatenate([_dot(att[(u, i)], cut(iv, u)[tile(i)]) for i in range(tb // sr)], axis=0)
             for u in units}
    yield
    inter = {}
    for j in range(n_state):
        for u in units:
            inter[(u, j)] = _dot_nt(cut(q_state, u)[span(j)], starts[(u, j)])
        yield

    for b in range(bt):
        outs = []
        for hd in range(B_HEADS):
            u = (b, hd)
            o_h = o_acc[u] + jnp.concatenate([inter[(u, j)] for j in range(n_state)], axis=0)
            o_h = o_h + jnp.sum(cut(own, u), axis=-1, keepdims=True) * cut(iv, u)
            outs.append(o_h * lax.rsqrt(jnp.mean(o_h * o_h, axis=-1, keepdims=True) + RMS_EPS))
        o = jnp.concatenate(outs, axis=1)
        on_ref[b] = (o * nw_ref[...] * (og[b] * _sigmoid(og[b]))).astype(BF16)

    @pl.when(c == pl.num_programs(1) - 1)
    def _():
        for b in range(bt):
            for hd in range(B_HEADS):
                s_out_ref[b, hd] = st_ref[b, hd].T


def _drive(gen):
    for _ in gen:
        pass


def _normalised(x_ref, nm_ref, bt):
    return jnp.concatenate([_rms(x_ref[b], nm_ref[...]).astype(BF16) for b in range(bt)], axis=0)


def _rwkv_kernel(layer, bt, nch, c_len, x_ref, nm_ref, *refs):
    _drive(_rwkv_part(layer, bt, nch, c_len, pl.program_id(1), _normalised(x_ref, nm_ref, bt), refs))


def _hgrn_kernel(layer, bt, tb, x_ref, nm_ref, *refs):
    _drive(_hgrn_part(layer, bt, tb, pl.program_id(1), _normalised(x_ref, nm_ref, bt), refs))


def _rwkv(layer, x, shift0, s_all, vfirst, nm, w_rw_all, mu_p, pv, w2_p, a2_p, g2_p, vw1_p, vw2_p, c_len, nch, bt):
    bsz, t_len, _ = x.shape
    frames = nch * c_len
    const2 = lambda b, c: (0, 0)
    step = lambda b, c: (b, c, 0)
    per_seq3 = lambda b, c: (b, 0, 0)
    per_seq4 = lambda b, c: (b, 0, 0, 0)
    in_specs = [
        pl.BlockSpec((bt, frames, D_MODEL), step),
        pl.BlockSpec((1, D_MODEL), const2),
        pl.BlockSpec((bt, 1, GROUP_COLS), per_seq3),
        pl.BlockSpec((None, bt, A_HEADS, A_HEAD_DIM, A_HEAD_DIM), lambda b, c: (layer, b, 0, 0, 0)),
    ]
    args = [x, nm, shift0, s_all]
    if layer > 0:
        in_specs.append(pl.BlockSpec((bt, frames, DA), step))
        args.append(vfirst)
    in_specs += [
        _resident((None, D_MODEL, GROUP_COLS), lambda b, c: (layer, 0, 0)),
        pl.BlockSpec((1, GROUP_COLS), const2),
        pl.BlockSpec((8, DA), const2),
        pl.BlockSpec((LANES, DA), const2),
        pl.BlockSpec((LANES, DA), const2),
        pl.BlockSpec((LORA_G_PAD, DA), const2),
    ]
    args += [w_rw_all, mu_p, pv, w2_p, a2_p, g2_p]
    if layer > 0:
        in_specs += [pl.BlockSpec((DA, LANES), const2), pl.BlockSpec((LANES, DA), const2)]
        args += [vw1_p, vw2_p]
    out_specs = [
        pl.BlockSpec((bt, frames, DA), step),
        pl.BlockSpec((bt, 1, GROUP_COLS), per_seq3),
        pl.BlockSpec((bt, A_HEADS, A_HEAD_DIM, A_HEAD_DIM), per_seq4),
    ]
    out_shape = [
        jax.ShapeDtypeStruct((bsz, t_len, DA), BF16),
        jax.ShapeDtypeStruct((bsz, 1, GROUP_COLS), F32),
        jax.ShapeDtypeStruct((bsz, A_HEADS, A_HEAD_DIM, A_HEAD_DIM), F32),
    ]
    if layer == 0:
        out_specs.append(pl.BlockSpec((bt, frames, DA), step))
        out_shape.append(jax.ShapeDtypeStruct((bsz, t_len, DA), F32))
    return pl.pallas_call(
        functools.partial(_rwkv_kernel, layer, bt, nch, c_len),
        grid=(bsz // bt, t_len // frames),
        in_specs=in_specs,
        out_specs=out_specs,
        out_shape=out_shape,
        scratch_shapes=[pltpu.VMEM((bt, frames + 8, GROUP_COLS), F32),
                        pltpu.VMEM((bt, N_PAIRS, PAIR, PAIR), F32)],
        compiler_params=pltpu.CompilerParams(
            dimension_semantics=("arbitrary", "arbitrary"), vmem_limit_bytes=VMEM_LIMIT),
        name=f"rwkv{layer}",
    )(*args)


def _hgrn(layer, x, nm, w_hg_all, lb_logits, norm_w, s_all, tb, bt):
    bsz, t_len, _ = x.shape
    const2 = lambda b, t: (0, 0)
    return pl.pallas_call(
        functools.partial(_hgrn_kernel, layer, bt, tb),
        grid=(bsz // bt, t_len // tb),
        in_specs=[
            pl.BlockSpec((bt, tb, D_MODEL), lambda b, t: (b, t, 0)),
            pl.BlockSpec((1, D_MODEL), const2),
            _resident((None, D_MODEL, HGRN_COLS), lambda b, t: (layer, 0, 0)),
            pl.BlockSpec((DEPTH, BK), const2),
            pl.BlockSpec((1, DB), const2),
            pl.BlockSpec((None, bt, B_HEADS, B_KEY_DIM, B_VAL_DIM), lambda b, t: (layer, b, 0, 0, 0)),
        ],
        out_specs=[
            pl.BlockSpec((bt, tb, DB), lambda b, t: (b, t, 0)),
            pl.BlockSpec((bt, B_HEADS, B_KEY_DIM, B_VAL_DIM), lambda b, t: (b, 0, 0, 0)),
        ],
        out_shape=[
            jax.ShapeDtypeStruct((bsz, t_len, DB), BF16),
            jax.ShapeDtypeStruct((bsz, B_HEADS, B_KEY_DIM, B_VAL_DIM), F32),
        ],
        scratch_shapes=[pltpu.VMEM((bt, B_HEADS, B_VAL_DIM, B_KEY_DIM), F32)],
        compiler_params=pltpu.CompilerParams(
            dimension_semantics=("arbitrary", "arbitrary"), vmem_limit_bytes=VMEM_LIMIT),
        name=f"hgrn{layer}",
    )(x, nm, w_hg_all, lb_logits, norm_w, s_all)


FF_CHUNK = 1024


def _mix_ffn_kernel(final, *refs):
    if final:
        (x_ref, yg_ref, on_ref, nm_ref, wg_ref, wa_ref, wb_ref, wo_ref, nf_ref, up_ref, dn_ref, nfin_ref,
         o_ref) = refs
    else:
        (x_ref, yg_ref, on_ref, nm_ref, wg_ref, wa_ref, wb_ref, wo_ref, nf_ref, up_ref, dn_ref, o_ref) = refs
    x = x_ref[...]
    gt = _sigmoid(jnp.dot(_rms(x, nm_ref[...]).astype(BF16), wg_ref[...], preferred_element_type=F32))
    ya = jnp.dot(yg_ref[...], wa_ref[...], preferred_element_type=F32)
    yb = jnp.dot(on_ref[...], wb_ref[...], preferred_element_type=F32)
    merged = gt[:, 0:D_MODEL] * ya + gt[:, D_MODEL:] * yb
    x = x + jnp.dot(merged.astype(BF16), wo_ref[...], preferred_element_type=F32)
    h = _rms(x, nf_ref[...]).astype(BF16)
    acc = x
    for cf in range(D_FF // FF_CHUNK):
        u = jnp.dot(h, up_ref[:, cf * FF_CHUNK:(cf + 1) * FF_CHUNK], preferred_element_type=F32)
        u = jnp.maximum(u, 0.0)
        acc = acc + jnp.dot((u * u).astype(BF16), dn_ref[cf * FF_CHUNK:(cf + 1) * FF_CHUNK, :],
                            preferred_element_type=F32)
    if final:
        acc = _rms(acc, nfin_ref[...])
    o_ref[...] = acc


def _mix_ffn(layer, x, yg, on, nm, nf, nfin, w):
    final = layer == DEPTH - 1
    n = x.shape[0]
    tm = min(n, 512)
    small = lambda i: (0, 0)
    mine = lambda i: (layer, 0, 0)
    in_specs = [
        pl.BlockSpec((tm, D_MODEL), lambda i: (i, 0)),
        pl.BlockSpec((tm, DA), lambda i: (i, 0)),
        pl.BlockSpec((tm, DB), lambda i: (i, 0)),
        _resident((1, D_MODEL), small),
        _resident((None, D_MODEL, GATE_COLS), mine),
        _resident((None, DA, D_MODEL), mine),
        _resident((None, DB, D_MODEL), mine),
        _resident((None, D_MODEL, D_MODEL), mine),
        _resident((1, D_MODEL), small),
        _resident((None, D_MODEL, D_FF), mine),
        _resident((None, D_FF, D_MODEL), mine),
    ]
    args = [x, yg, on, nm, w["wg"], w["wa"], w["wb"], w["wo"], nf, w["up"], w["dn"]]
    if final:
        in_specs.append(_resident((1, D_MODEL), small))
        args.append(nfin)
    return pl.pallas_call(
        functools.partial(_mix_ffn_kernel, final),
        grid=(n // tm,),
        in_specs=in_specs,
        out_specs=pl.BlockSpec((tm, D_MODEL), lambda i: (i, 0)),
        out_shape=jax.ShapeDtypeStruct((n, D_MODEL), F32),
        compiler_params=pltpu.CompilerParams(dimension_semantics=("arbitrary",), vmem_limit_bytes=VMEM_LIMIT),
        name="mix_ffn_final" if final else "mix_ffn",
    )(*args)


def _pad_rows(w, rows, at=0):
    out = jnp.zeros((rows, w.shape[1]), w.dtype)
    return out.at[at:at + w.shape[0]].set(w)


def _prep_weights(p):
    w_in = p["w_in"]
    pad = ((0, 0), (0, 0), (0, GROUP_COLS - RWKV_COLS))
    return dict(
        w_rw=jnp.pad(w_in[:, :, :RWKV_COLS], pad).astype(BF16),
        w_hg=w_in[:, :, RWKV_COLS:RWKV_COLS + HGRN_COLS].astype(BF16),
        wg=w_in[:, :, RWKV_COLS + HGRN_COLS:].astype(BF16),
        wa=p["w_out_a"].astype(BF16), wb=p["w_out_b"].astype(BF16), wo=p["w_out"].astype(BF16),
        up=p["w_ffn_up"].astype(BF16), dn=p["w_ffn_down"].astype(BF16),
    )


def _prep_layer(l, p):
    mu_p = jnp.zeros((1, GROUP_COLS), F32).at[0, :RWKV_COLS].set(p["rwkv_mu"][l])
    v0 = p["rwkv_v0"][l - 1] if l > 0 else jnp.zeros((DA,), F32)
    pv = jnp.stack([p["rwkv_w0"][l], p["rwkv_a0"][l], p["rwkv_k_k"][l], p["rwkv_k_a"][l],
                    p["rwkv_r_k"][l].reshape(DA), p["rwkv_ln_w"][l], p["rwkv_ln_b"][l], v0])
    out = dict(
        mu_p=mu_p, pv=pv,
        w2_p=_pad_rows(p["rwkv_w2"][l], LANES, 0).astype(BF16),
        a2_p=_pad_rows(p["rwkv_a2"][l], LANES, W_LORA).astype(BF16),
        g2_p=_pad_rows(p["rwkv_g2"][l], LORA_G_PAD, 0).astype(BF16),
        vw1_p=None, vw2_p=None,
        norm_mix=p["norm_mix"][l].reshape(1, D_MODEL),
        hgrn_norm_w=p["hgrn_norm_w"][l].reshape(1, DB),
        nf=p["norm_ffn"][l].reshape(1, D_MODEL),
    )
    if l > 0:
        w1 = p["rwkv_vres_w1"][l - 1]
        out["vw1_p"] = jnp.zeros((DA, LANES), F32).at[:, :V_LORA].set(w1).astype(BF16)
        out["vw2_p"] = _pad_rows(p["rwkv_vres_w2"][l - 1], LANES, 0).astype(BF16)
    return out


def _seq_tile(bsz, frames, step_rows):
    bt = min(bsz, max(1, step_rows // frames))
    assert bsz % bt == 0, (bsz, bt)
    return bt


def _trunk(x, state_shift, state_rwkv, state_hgrn, layers, weights, p):
    bsz, t_len, _ = x.shape
    c_len = min(t_len, RWKV_CHUNK)
    nch = min(t_len // c_len, RWKV_CHUNKS)
    tb = min(t_len, HGRN_TILE)
    assert t_len % (nch * c_len) == 0 and t_len % tb == 0 and tb % PAIR_LEN == 0
    nfin = p["norm_final"].reshape(1, D_MODEL)
    vfirst = None
    shifts, rwkv_states, hgrn_states = [], [], []
    for l in range(DEPTH):
        lp = layers[l]
        shift0 = jnp.zeros((bsz, 1, GROUP_COLS), F32).at[:, 0, :RWKV_COLS].set(state_shift[l])
        res = _rwkv(l, x, shift0, state_rwkv, vfirst, lp["norm_mix"], weights["w_rw"], lp["mu_p"], lp["pv"],
                    lp["w2_p"], lp["a2_p"], lp["g2_p"], lp["vw1_p"], lp["vw2_p"], c_len, nch,
                    min(_seq_tile(bsz, nch * c_len, RWKV_STEP_ROWS), RWKV_STEP_UNITS // (nch * N_PAIRS)))
        if l == 0:
            yg, shift_out, s_rwkv, vfirst = res
        else:
            yg, shift_out, s_rwkv = res
        on, s_hgrn = _hgrn(l, x, lp["norm_mix"], weights["w_hg"], p["hgrn_lb_logits"], lp["hgrn_norm_w"],
                           state_hgrn, tb, min(_seq_tile(bsz, tb, HGRN_STEP_ROWS), HGRN_STEP_UNITS // B_HEADS))
        n = bsz * t_len
        x = _mix_ffn(l, x.reshape(n, D_MODEL), yg.reshape(n, DA), on.reshape(n, DB), lp["norm_mix"], lp["nf"],
                     nfin, weights).reshape(bsz, t_len, D_MODEL)
        shifts.append(shift_out[:, 0, :RWKV_COLS])
        rwkv_states.append(s_rwkv)
        hgrn_states.append(s_hgrn)
    return x, jnp.stack(shifts), jnp.stack(rwkv_states), jnp.stack(hgrn_states)


def kernel(x_prompt, x_sample, state_shift, state_rwkv, state_hgrn, norm_mix, w_in, rwkv_mu, rwkv_w0, rwkv_w2, rwkv_a0, rwkv_a2, rwkv_g2, rwkv_v0, rwkv_vres_w1, rwkv_vres_w2, rwkv_k_k, rwkv_k_a, rwkv_r_k, rwkv_ln_w, rwkv_ln_b, hgrn_lb_logits, hgrn_norm_w, w_out_a, w_out_b, w_out, norm_ffn, w_ffn_up, w_ffn_down, norm_final):
    p = dict(norm_mix=norm_mix, w_in=w_in, rwkv_mu=rwkv_mu, rwkv_w0=rwkv_w0, rwkv_w2=rwkv_w2, rwkv_a0=rwkv_a0,
             rwkv_a2=rwkv_a2, rwkv_g2=rwkv_g2, rwkv_v0=rwkv_v0, rwkv_vres_w1=rwkv_vres_w1,
             rwkv_vres_w2=rwkv_vres_w2, rwkv_k_k=rwkv_k_k, rwkv_k_a=rwkv_k_a, rwkv_r_k=rwkv_r_k,
             rwkv_ln_w=rwkv_ln_w, rwkv_ln_b=rwkv_ln_b, hgrn_lb_logits=hgrn_lb_logits, hgrn_norm_w=hgrn_norm_w,
             w_out_a=w_out_a, w_out_b=w_out_b, w_out=w_out, norm_ffn=norm_ffn, w_ffn_up=w_ffn_up,
             w_ffn_down=w_ffn_down, norm_final=norm_final)
    layers = [_prep_layer(l, p) for l in range(DEPTH)]
    weights = _prep_weights(p)
    bp = x_prompt.shape[0]
    dt = x_prompt.dtype
    zero_shift = jnp.zeros((DEPTH, bp, RWKV_COLS), dt)
    zero_rwkv = jnp.zeros((DEPTH, bp, A_HEADS, A_HEAD_DIM, A_HEAD_DIM), dt)
    zero_hgrn = jnp.zeros((DEPTH, bp, B_HEADS, B_KEY_DIM, B_VAL_DIM), dt)
    y_prompt, shift_p, rwkv_p, hgrn_p = _trunk(x_prompt, zero_shift, zero_rwkv, zero_hgrn, layers, weights, p)
    y_sample, shift_s, rwkv_s, hgrn_s = _trunk(x_sample, state_shift, state_rwkv, state_hgrn, layers, weights, p)
    return (y_prompt, y_sample, shift_p, rwkv_p, hgrn_p, shift_s, rwkv_s, hgrn_s)
```

```python
import functools
import math

import jax
import jax.numpy as jnp
from jax import lax
from jax.experimental import pallas as pl
from jax.experimental.pallas import tpu as pltpu

F32 = jnp.float32
BF16 = jnp.bfloat16

D_MODEL = 1024
DEPTH = 2
A_HEADS = 8
A_HEAD_DIM = 64
DA = A_HEADS * A_HEAD_DIM
W_LORA = 64
A_LORA = 64
V_LORA = 32
G_LORA = 160
RWKV_COLS = 3 * DA + W_LORA + A_LORA + G_LORA
B_HEADS = 4
B_KEY_DIM = 128
B_VAL_DIM = 128
BK = B_HEADS * B_KEY_DIM
DB = B_HEADS * B_VAL_DIM
HGRN_COLS = 2 * BK + 2 * DB
GATE_COLS = 2 * D_MODEL
D_FF = 4 * D_MODEL
HGRN_BLOCK = 16
RMS_EPS = 1e-6
GN_EPS = 64e-5

LANES = 128
GROUP_COLS = 2048
LORA_WA = 3 * DA
LORA_G = LORA_WA + W_LORA + A_LORA
LORA_G_PAD = 256
PAIR = 2 * A_HEAD_DIM
N_PAIRS = A_HEADS // 2
PAIR_LEN = 2 * HGRN_BLOCK
STATE_LEN = 2 * PAIR_LEN
VMEM_LIMIT = 56 * 1024 * 1024

RWKV_CHUNK = 64
RWKV_CHUNKS = 4
RWKV_STEP_ROWS = 512
RWKV_STEP_UNITS = 32
CUMSUM_ROWS = 256
HGRN_TILE = 512
HGRN_STEP_ROWS = 1024
HGRN_STEP_UNITS = 64
SCORE_ROWS = 128


def _dot(a, b):
    return jnp.dot(a.astype(BF16), b.astype(BF16), preferred_element_type=F32)


def _dot_nt(a, b):
    return lax.dot_general(a.astype(BF16), b.astype(BF16), (((1,), (1,)), ((), ())), preferred_element_type=F32)


def _dot_tn(a, b):
    return lax.dot_general(a.astype(BF16), b.astype(BF16), (((0,), (0,)), ((), ())), preferred_element_type=F32)


def _split3(x):
    hi = x.astype(BF16)
    r1 = x - hi.astype(F32)
    mid = r1.astype(BF16)
    return hi, mid, (r1 - mid.astype(F32)).astype(BF16)


def _dot_01(m01, x):
    return functools.reduce(lambda a, b: a + b, [jnp.dot(m01, t, preferred_element_type=F32) for t in _split3(x)])


def _dot_x01(x, m01):
    return functools.reduce(lambda a, b: a + b, [jnp.dot(t, m01, preferred_element_type=F32) for t in _split3(x)])


def _rms(x, w):
    return x * lax.rsqrt(jnp.mean(x * x, axis=-1, keepdims=True) + RMS_EPS) * w


def _sigmoid(x):
    return 1.0 / (1.0 + jnp.exp(-x))


def _log2(n):
    assert n & (n - 1) == 0, n
    return n.bit_length() - 1


def _resident(shape, index_map):
    return pl.BlockSpec(shape, index_map, pipeline_mode=pl.Buffered(1))


_DONE = object()


def _round_robin(*gens):
    live = list(gens)
    while live:
        for gen in list(live):
            if next(gen, _DONE) is _DONE:
                live.remove(gen)
            else:
                yield


def _rwkv_part(layer, bt, nch, c_len, c, h, refs):
    if layer == 0:
        (shift_ref, s0_ref, w_ref, mu_ref, pv_ref, w2_ref, a2_ref, g2_ref,
         yg_ref, shift_out_ref, s_out_ref, vfirst_out_ref, xs_ref, s_ref) = refs
    else:
        (shift_ref, s0_ref, vfirst_ref, w_ref, mu_ref, pv_ref, w2_ref, a2_ref, g2_ref, vw1_ref, vw2_ref,
         yg_ref, shift_out_ref, s_out_ref, xs_ref, s_ref) = refs
    frames = nch * c_len
    rows = bt * frames
    stk = 2 * c_len

    hrow = lax.broadcasted_iota(jnp.int32, (A_HEAD_DIM, PAIR), 0)
    hlane = lax.broadcasted_iota(jnp.int32, (A_HEAD_DIM, PAIR), 1)
    to_lo = jnp.where(hlane == hrow, 1.0, 0.0).astype(BF16)
    to_hi = jnp.where(hlane == hrow + A_HEAD_DIM, 1.0, 0.0).astype(BF16)

    @pl.when(c == 0)
    def _():
        for b in range(bt):
            xs_ref[b, 7:8, :] = shift_ref[b]
            for p in range(N_PAIRS):
                s_ref[b, p] = jnp.concatenate(
                    [_dot_x01(s0_ref[b, 2 * p], to_lo), _dot_x01(s0_ref[b, 2 * p + 1], to_hi)], axis=0)

    def project(c0, c1):
        rw = jnp.dot(h, w_ref[:, c0:c1], preferred_element_type=F32)
        prevs = []
        for b in range(bt):
            rw_b = rw[b * frames:(b + 1) * frames]
            xs_ref[b, 8:8 + frames, c0:c1] = rw_b
            prevs.append(xs_ref[b, 7:7 + frames, c0:c1])
            last = rw_b[frames - 1:frames, :]
            xs_ref[b, 7:8, c0:c1] = last
            shift_out_ref[b, :, c0:c1] = last
        return rw + (jnp.concatenate(prevs, axis=0) - rw) * mu_ref[:, c0:c1]

    w0 = pv_ref[0:1, :]
    a0 = pv_ref[1:2, :]
    k_k = pv_ref[2:3, :]
    k_a = pv_ref[3:4, :]
    r_k = pv_ref[4:5, :]
    ln_w = pv_ref[5:6, :]
    ln_b = pv_ref[6:7, :]

    cs_rows = min(rows, CUMSUM_ROWS)
    row = lax.broadcasted_iota(jnp.int32, (cs_rows, cs_rows), 0)
    col = lax.broadcasted_iota(jnp.int32, (cs_rows, cs_rows), 1)
    chunk_shift = _log2(c_len)
    causal01 = jnp.where(jnp.logical_and(
        lax.shift_right_logical(row, chunk_shift) == lax.shift_right_logical(col, chunk_shift), row >= col),
        1.0, 0.0).astype(BF16)

    lora = project(LORA_WA, GROUP_COLS)
    yield "prep"
    k = project(DA, 2 * DA)
    yield "prep"
    wa = lora[:, 0:LORA_G - LORA_WA]
    gl = lora[:, LORA_G - LORA_WA:LORA_G - LORA_WA + LORA_G_PAD]
    w_raw = w0 + _dot(jnp.tanh(wa), w2_ref[...])
    a_gate = _sigmoid(a0 + _dot(wa, a2_ref[...]))
    g = _dot(_sigmoid(gl), g2_ref[...])
    yield "prep"
    r = project(0, DA)
    yield "prep"
    lw = (-math.exp(-0.5)) * _sigmoid(w_raw)
    g_cum = jnp.concatenate([_dot_01(causal01, lw[i:i + cs_rows]) for i in range(0, rows, cs_rows)],
                            axis=0)
    yield "prep"
    v = project(2 * DA, 3 * DA)
    yield "prep"
    if layer == 0:
        for b in range(bt):
            vfirst_out_ref[b] = v[b * frames:(b + 1) * frames]
    else:
        v0 = pv_ref[7:8, :]
        vg = _sigmoid(v0 + _dot(_dot(v, vw1_ref[...]), vw2_ref[...]))
        vfirst = jnp.concatenate([vfirst_ref[b] for b in range(bt)], axis=0)
        v = v + (vfirst - v) * vg

    lane = lax.broadcasted_iota(jnp.int32, (PAIR, PAIR), 1)
    sub = lax.broadcasted_iota(jnp.int32, (PAIR, PAIR), 0)
    head_ones = jnp.where((lane < A_HEAD_DIM) == (sub < A_HEAD_DIM), 1.0, 0.0).astype(BF16)

    def seg_sum(x):
        return jnp.concatenate(
            [jnp.dot(x[:, p * PAIR:(p + 1) * PAIR].astype(BF16), head_ones, preferred_element_type=F32)
             for p in range(N_PAIRS)], axis=1)

    kk = k * k_k
    kk = kk * lax.rsqrt(jnp.maximum(seg_sum(kk * kk), 1e-24))
    yield "prep"
    k_h = k * (1.0 + (a_gate - 1.0) * k_a)
    a_vec = -kk
    b_vec = kk * a_gate

    slabs = {}

    def slab(b, ch):
        if (b, ch) not in slabs:
            r0 = (b * nch + ch) * c_len
            rs = slice(r0, r0 + c_len)
            gc = g_cum[rs]
            ge = jnp.broadcast_to(gc[c_len - 1:c_len, :], (c_len, DA))
            e_neg = jnp.exp(-gc)
            e_end = jnp.exp(ge - gc)
            slabs[(b, ch)] = dict(
                rq=r[rs] * jnp.exp(gc), aq=a_vec[rs] * jnp.exp(gc - lw[rs]), kn=k_h[rs] * e_neg,
                bn=b_vec[rs] * e_neg, k_end=k_h[rs] * e_end, b_end=b_vec[rs] * e_end,
                decay_end=jnp.exp(ge[0:1, :]), v=v[rs])
        return slabs[(b, ch)]

    srow = lax.broadcasted_iota(jnp.int32, (stk, PAIR), 0)
    slane = lax.broadcasted_iota(jnp.int32, (stk, PAIR), 1)
    own_lanes = (srow < c_len) == (slane < A_HEAD_DIM)

    def stack(x):
        return jnp.where(own_lanes, jnp.concatenate([x, x], axis=0), 0.0).astype(BF16)

    mrow = lax.broadcasted_iota(jnp.int32, (stk, stk), 0)
    mcol = lax.broadcasted_iota(jnp.int32, (stk, stk), 1)
    strict = (mrow & (c_len - 1)) > (mcol & (c_len - 1))
    lower = (mrow & (c_len - 1)) >= (mcol & (c_len - 1))
    eye = jnp.where(mrow == mcol, 1.0, 0.0).astype(F32)

    def cut(name, u):
        b, ch, p = u
        return slab(b, ch)[name][:, p * PAIR:(p + 1) * PAIR]

    lhs, v_st, a_ak_v, a_rk, a_rb, t_inv, y_parts = {}, {}, {}, {}, {}, {}, {}

    def scores_and_inverse(units):
        power = {}
        for u in units:
            lhs[u] = jnp.concatenate([stack(cut("aq", u)), stack(cut("rq", u))], axis=0)
            v_st[u] = stack(cut("v", u))
            sc_b = _dot_nt(lhs[u], stack(cut("bn", u)))
            sc_k = _dot_nt(lhs[u], stack(cut("kn", u)))
            power[u] = jnp.where(strict, sc_b[:stk], 0.0)
            a_rb[u] = jnp.where(lower, sc_b[stk:], 0.0)
            a_rk[u] = jnp.where(lower, sc_k[stk:], 0.0)
            a_ak_v[u] = jnp.where(strict, sc_k[:stk], 0.0)
        yield
        for u in units:
            t_inv[u] = eye + power[u]
            a_ak_v[u] = _dot(a_ak_v[u], v_st[u])
            power[u] = _dot(power[u], power[u])
        yield
        for lvl in range(1, _log2(c_len)):
            new_power = {}
            for u in units:
                t_inv[u] = t_inv[u] + _dot(power[u], t_inv[u])
                if lvl < _log2(c_len) - 1:
                    new_power[u] = _dot(power[u], power[u])
            power = new_power
            yield

    def apply_state(units):
        from_state = {u: _dot_nt(lhs[u], s_ref[u[0], u[2]]) for u in units}
        yield
        corr = {u: _dot(t_inv[u], from_state[u][:stk] + a_ak_v[u]) for u in units}
        yield
        for u in units:
            vu = jnp.concatenate([v_st[u], corr[u].astype(BF16)], axis=0)
            if stk % LANES == 0:
                y_st = from_state[u][stk:] + _dot(jnp.concatenate([a_rk[u], a_rb[u]], axis=1), vu)
            else:
                y_st = from_state[u][stk:] + _dot(a_rk[u], v_st[u]) + _dot(a_rb[u], corr[u])
            y_parts[u] = y_st[:c_len] + y_st[c_len:]
            ends = jnp.concatenate([stack(cut("k_end", u)), stack(cut("b_end", u))], axis=0)
            s_ref[u[0], u[2]] = s_ref[u[0], u[2]] * cut("decay_end", u) + _dot_tn(vu, ends)
        yield

    chunk_units = [[(b, ch, p) for b in range(bt) for p in range(N_PAIRS)] for ch in range(nch)]
    for _ in scores_and_inverse(chunk_units[0]):
        yield "dense"
    for ch in range(nch):
        if ch + 1 < nch:
            for _ in _round_robin(scores_and_inverse(chunk_units[ch + 1]), apply_state(chunk_units[ch])):
                yield "dense"
        else:
            for _ in apply_state(chunk_units[ch]):
                yield "tail"

    y = jnp.concatenate(
        [jnp.concatenate([y_parts[(b, ch, p)] for p in range(N_PAIRS)], axis=1)
         for b in range(bt) for ch in range(nch)], axis=0)
    inv_n = 1.0 / A_HEAD_DIM
    mean = seg_sum(y) * inv_n
    bonus = seg_sum(r * k_h * r_k)
    yield "tail"
    dev = y - mean
    var = seg_sum(dev * dev) * inv_n
    yield "tail"
    y = dev * lax.rsqrt(var + GN_EPS) * ln_w + ln_b
    y = ((y + bonus * v) * g).astype(BF16)
    for b in range(bt):
        yg_ref[b] = y[b * frames:(b + 1) * frames]

    @pl.when(c == pl.num_programs(1) - 1)
    def _():
        prow = lax.broadcasted_iota(jnp.int32, (PAIR, A_HEAD_DIM), 0)
        pcol = lax.broadcasted_iota(jnp.int32, (PAIR, A_HEAD_DIM), 1)
        from_lo = jnp.where(prow == pcol, 1.0, 0.0).astype(BF16)
        from_hi = jnp.where(prow == pcol + A_HEAD_DIM, 1.0, 0.0).astype(BF16)
        for b in range(bt):
            for p in range(N_PAIRS):
                s_pair = s_ref[b, p]
                s_out_ref[b, 2 * p] = _dot_x01(s_pair[:A_HEAD_DIM], from_lo)
                s_out_ref[b, 2 * p + 1] = _dot_x01(s_pair[A_HEAD_DIM:], from_hi)


def _hgrn_part(layer, bt, tb, c, h, refs):
    (w_ref, lg_ref, nw_ref, s0_ref, on_ref, s_out_ref, st_ref) = refs
    n_blk = tb // HGRN_BLOCK

    def project(c0):
        return [jnp.dot(h[b * tb:(b + 1) * tb], w_ref[:, c0:c0 + BK], preferred_element_type=F32)
                for b in range(bt)]

    fz_all = project(BK)
    yield
    q_all = project(0)
    yield

    @pl.when(c == 0)
    def _():
        for b in range(bt):
            for hd in range(B_HEADS):
                st_ref[b, hd] = s0_ref[b, hd].T

    lg = lg_ref[...]
    lrows = [lg[i:i + 1, :] for i in range(DEPTH)]
    mx = functools.reduce(jnp.maximum, lrows)
    ex = [jnp.exp(x - mx) for x in lrows]
    den = functools.reduce(lambda a, b: a + b, ex)
    sm = [e / den for e in ex]
    lb = functools.reduce(lambda a, b: a + b, sm[:layer + 1]) - sm[0]

    cs_rows = min(tb, CUMSUM_ROWS)
    row = lax.broadcasted_iota(jnp.int32, (cs_rows, cs_rows), 0)
    col = lax.broadcasted_iota(jnp.int32, (cs_rows, cs_rows), 1)
    blk_shift = _log2(HGRN_BLOCK)
    causal = jnp.logical_and(lax.shift_right_logical(row, blk_shift) == lax.shift_right_logical(col, blk_shift),
                             row >= col)
    causal01 = jnp.where(causal, 1.0, 0.0).astype(BF16)

    f_all = [lb + (1.0 - lb) * _sigmoid(fz_all[b]) for b in range(bt)]
    g_cum_all = [jnp.concatenate([_dot_01(causal01, jnp.log(f_all[b][i:i + cs_rows]))
                                  for i in range(0, tb, cs_rows)], axis=0) for b in range(bt)]
    yield
    iv = project(2 * BK)
    yield

    state_len = min(tb, STATE_LEN)
    two_pairs = state_len == 2 * PAIR_LEN
    frow = lax.broadcasted_iota(jnp.int32, (tb, BK), 0)
    second = (lax.shift_right_logical(frow, blk_shift) & 1) == 1
    second_pair = (lax.shift_right_logical(frow, blk_shift + 1) & 1) == 1
    zeros_blk = jnp.zeros((HGRN_BLOCK, BK), F32)
    zeros_pair = jnp.zeros((PAIR_LEN, BK), F32)
    sr = min(tb, SCORE_ROWS)
    srow = lax.broadcasted_iota(jnp.int32, (sr, sr), 0)
    scol = lax.broadcasted_iota(jnp.int32, (sr, sr), 1)

    def split_mask(hl):
        grp = _log2(2 * hl)
        return jnp.logical_and(
            lax.shift_right_logical(srow, grp) == lax.shift_right_logical(scol, grp),
            jnp.logical_and((srow & (2 * hl - 1)) >= hl, (scol & (2 * hl - 1)) < hl))

    pair_mask = split_mask(HGRN_BLOCK)

    levels = [HGRN_BLOCK >> (i + 1) for i in range(blk_shift)]
    pos_in = {hl: frow & (2 * hl - 1) for hl in levels}
    level_mask = {hl: split_mask(hl) for hl in levels}
    toward_anchor = {hl: jnp.where(pos_in[hl] >= hl, 1.0, -1.0).astype(F32) for hl in levels}

    def anchor(g, hl):
        if 2 * hl >= 8:
            g3 = g.reshape(tb // (2 * hl), 2 * hl, BK)
            return jnp.broadcast_to(g3[:, hl - 1:hl, :], g3.shape).reshape(tb, BK)
        out = g
        for pos in range(2 * hl):
            if pos != hl - 1:
                out = jnp.where(pos_in[hl] == pos, pltpu.roll(g, (pos - (hl - 1)) % tb, 0), out)
        return out

    q_lvl = {hl: [] for hl in levels}
    k_lvl = {hl: [] for hl in levels}
    own = []
    qg, qg_pair, k_end, k_end_pair, q_state, k_state, decay_state = [], [], [], [], [], [], []
    for b in range(bt):
        q = q_all[b]
        kx = 1.0 - f_all[b]
        g_cum = g_cum_all[b]
        qs = q * _sigmoid(q)
        own.append(qs * kx)
        for hl in levels:
            partial = jnp.exp((g_cum - anchor(g_cum, hl)) * toward_anchor[hl])
            q_lvl[hl].append((qs * partial).astype(BF16))
            k_lvl[hl].append((kx * partial).astype(BF16))
        g_end = jnp.concatenate(
            [jnp.broadcast_to(g_cum[(j + 1) * HGRN_BLOCK - 1:(j + 1) * HGRN_BLOCK, :], (HGRN_BLOCK, BK))
             for j in range(n_blk)], axis=0)
        decay = jnp.exp(g_end)
        decay_prev = jnp.concatenate([zeros_blk, decay[:-HGRN_BLOCK]], axis=0)
        decay_next = jnp.concatenate([decay[HGRN_BLOCK:], zeros_blk], axis=0)
        q_dec = qs * jnp.exp(g_cum)
        k_to_end = kx * jnp.exp(g_end - g_cum)
        qg.append(q_dec.astype(BF16))
        q_pair = q_dec * jnp.where(second, decay_prev, 1.0)
        k_pair = k_to_end * jnp.where(second, 1.0, decay_next)
        qg_pair.append(q_pair.astype(BF16))
        k_end.append(k_to_end.astype(BF16))
        k_end_pair.append(k_pair.astype(BF16))
        pair_tot = decay * jnp.where(second, decay_prev, decay_next)
        if two_pairs:
            tot_prev = jnp.concatenate([zeros_pair, pair_tot[:-PAIR_LEN]], axis=0)
            tot_next = jnp.concatenate([pair_tot[PAIR_LEN:], zeros_pair], axis=0)
            q_state.append((q_pair * jnp.where(second_pair, tot_prev, 1.0)).astype(BF16))
            k_state.append((k_pair * jnp.where(second_pair, 1.0, tot_next)).astype(BF16))
            decay_state.append(pair_tot * tot_next)
        else:
            q_state.append(qg_pair[-1])
            k_state.append(k_end_pair[-1])
            decay_state.append(pair_tot)

    units = [(b, hd) for b in range(bt) for hd in range(B_HEADS)]
    n_state = tb // state_len

    def cut(x, u):
        return x[u[0]][:, u[1] * B_KEY_DIM:(u[1] + 1) * B_KEY_DIM]

    def span(j):
        return slice(j * state_len, (j + 1) * state_len)

    def tile(i):
        return slice(i * sr, (i + 1) * sr)

    def scores(mask, qx, kx_, below):
        return {(u, i): jnp.where(mask, _dot_nt(cut(qx, u)[tile(i)], cut(kx_, u)[tile(i)]),
                                  0.0 if below is None else below[(u, i)])
                for u in units for i in range(tb // sr)}

    att = scores(pair_mask, qg, k_end, None)
    yield
    if two_pairs:
        att = scores(split_mask(PAIR_LEN), qg_pair, k_end_pair, att)
        yield
    for hl in levels:
        att = scores(level_mask[hl], q_lvl[hl], k_lvl[hl], att)
        yield
    upd = {}
    for j in range(n_state):
        for u in units:
            upd[(u, j)] = _dot_tn(cut(iv, u)[span(j)], cut(k_state, u)[span(j)])
        yield
    starts = {}
    for u in units:
        s = st_ref[u[0], u[1]]
        dec = cut(decay_state, u)
        for j in range(n_state):
            starts[(u, j)] = s.astype(BF16)
            s = s * dec[j * state_len:j * state_len + 1, :] + upd[(u, j)]
        st_ref[u[0], u[1]] = s
    og = project(2 * BK + DB)
    yield
    o_acc = {u: jnp.concatenate([_dot(att[(u, i)], cut(iv, u)[tile(i)]) for i in range(tb // sr)], axis=0)
             for u in units}
    yield
    inter = {}
    for j in range(n_state):
        for u in units:
            inter[(u, j)] = _dot_nt(cut(q_state, u)[span(j)], starts[(u, j)])
        yield

    for b in range(bt):
        outs = []
        for hd in range(B_HEADS):
            u = (b, hd)
            o_h = o_acc[u] + jnp.concatenate([inter[(u, j)] for j in range(n_state)], axis=0)
            o_h = o_h + jnp.sum(cut(own, u), axis=-1, keepdims=True) * cut(iv, u)
            outs.append(o_h * lax.rsqrt(jnp.mean(o_h * o_h, axis=-1, keepdims=True) + RMS_EPS))
        o = jnp.concatenate(outs, axis=1)
        on_ref[b] = (o * nw_ref[...] * (og[b] * _sigmoid(og[b]))).astype(BF16)

    @pl.when(c == pl.num_programs(1) - 1)
    def _():
        for b in range(bt):
            for hd in range(B_HEADS):
                s_out_ref[b, hd] = st_ref[b, hd].T


def _drive(gen):
    for _ in gen:
        pass


def _normalised(x_ref, nm_ref, bt):
    return jnp.concatenate([_rms(x_ref[b], nm_ref[...]).astype(BF16) for b in range(bt)], axis=0)


def _rwkv_kernel(layer, bt, nch, c_len, x_ref, nm_ref, *refs):
    _drive(_rwkv_part(layer, bt, nch, c_len, pl.program_id(1), _normalised(x_ref, nm_ref, bt), refs))


def _hgrn_kernel(layer, bt, tb, x_ref, nm_ref, *refs):
    _drive(_hgrn_part(layer, bt, tb, pl.program_id(1), _normalised(x_ref, nm_ref, bt), refs))


def _rwkv(layer, x, shift0, s_all, vfirst, nm, w_rw_all, mu_p, pv, w2_p, a2_p, g2_p, vw1_p, vw2_p, c_len, nch, bt):
    bsz, t_len, _ = x.shape
    frames = nch * c_len
    const2 = lambda b, c: (0, 0)
    step = lambda b, c: (b, c, 0)
    per_seq3 = lambda b, c: (b, 0, 0)
    per_seq4 = lambda b, c: (b, 0, 0, 0)
    in_specs = [
        pl.BlockSpec((bt, frames, D_MODEL), step),
        pl.BlockSpec((1, D_MODEL), const2),
        pl.BlockSpec((bt, 1, GROUP_COLS), per_seq3),
        pl.BlockSpec((None, bt, A_HEADS, A_HEAD_DIM, A_HEAD_DIM), lambda b, c: (layer, b, 0, 0, 0)),
    ]
    args = [x, nm, shift0, s_all]
    if layer > 0:
        in_specs.append(pl.BlockSpec((bt, frames, DA), step))
        args.append(vfirst)
    in_specs += [
        _resident((None, D_MODEL, GROUP_COLS), lambda b, c: (layer, 0, 0)),
        pl.BlockSpec((1, GROUP_COLS), const2),
        pl.BlockSpec((8, DA), const2),
        pl.BlockSpec((LANES, DA), const2),
        pl.BlockSpec((LANES, DA), const2),
        pl.BlockSpec((LORA_G_PAD, DA), const2),
    ]
    args += [w_rw_all, mu_p, pv, w2_p, a2_p, g2_p]
    if layer > 0:
        in_specs += [pl.BlockSpec((DA, LANES), const2), pl.BlockSpec((LANES, DA), const2)]
        args += [vw1_p, vw2_p]
    out_specs = [
        pl.BlockSpec((bt, frames, DA), step),
        pl.BlockSpec((bt, 1, GROUP_COLS), per_seq3),
        pl.BlockSpec((bt, A_HEADS, A_HEAD_DIM, A_HEAD_DIM), per_seq4),
    ]
    out_shape = [
        jax.ShapeDtypeStruct((bsz, t_len, DA), BF16),
        jax.ShapeDtypeStruct((bsz, 1, GROUP_COLS), F32),
        jax.ShapeDtypeStruct((bsz, A_HEADS, A_HEAD_DIM, A_HEAD_DIM), F32),
    ]
    if layer == 0:
        out_specs.append(pl.BlockSpec((bt, frames, DA), step))
        out_shape.append(jax.ShapeDtypeStruct((bsz, t_len, DA), F32))
    return pl.pallas_call(
        functools.partial(_rwkv_kernel, layer, bt, nch, c_len),
        grid=(bsz // bt, t_len // frames),
        in_specs=in_specs,
        out_specs=out_specs,
        out_shape=out_shape,
        scratch_shapes=[pltpu.VMEM((bt, frames + 8, GROUP_COLS), F32),
                        pltpu.VMEM((bt, N_PAIRS, PAIR, PAIR), F32)],
        compiler_params=pltpu.CompilerParams(
            dimension_semantics=("arbitrary", "arbitrary"), vmem_limit_bytes=VMEM_LIMIT),
        name=f"rwkv{layer}",
    )(*args)


def _hgrn(layer, x, nm, w_hg_all, lb_logits, norm_w, s_all, tb, bt):
    bsz, t_len, _ = x.shape
    const2 = lambda b, t: (0, 0)
    return pl.pallas_call(
        functools.partial(_hgrn_kernel, layer, bt, tb),
        grid=(bsz // bt, t_len // tb),
        in_specs=[
            pl.BlockSpec((bt, tb, D_MODEL), lambda b, t: (b, t, 0)),
            pl.BlockSpec((1, D_MODEL), const2),
            _resident((None, D_MODEL, HGRN_COLS), lambda b, t: (layer, 0, 0)),
            pl.BlockSpec((DEPTH, BK), const2),
            pl.BlockSpec((1, DB), const2),
            pl.BlockSpec((None, bt, B_HEADS, B_KEY_DIM, B_VAL_DIM), lambda b, t: (layer, b, 0, 0, 0)),
        ],
        out_specs=[
            pl.BlockSpec((bt, tb, DB), lambda b, t: (b, t, 0)),
            pl.BlockSpec((bt, B_HEADS, B_KEY_DIM, B_VAL_DIM), lambda b, t: (b, 0, 0, 0)),
        ],
        out_shape=[
            jax.ShapeDtypeStruct((bsz, t_len, DB), BF16),
            jax.ShapeDtypeStruct((bsz, B_HEADS, B_KEY_DIM, B_VAL_DIM), F32),
        ],
        scratch_shapes=[pltpu.VMEM((bt, B_HEADS, B_VAL_DIM, B_KEY_DIM), F32)],
        compiler_params=pltpu.CompilerParams(
            dimension_semantics=("arbitrary", "arbitrary"), vmem_limit_bytes=VMEM_LIMIT),
        name=f"hgrn{layer}",
    )(x, nm, w_hg_all, lb_logits, norm_w, s_all)


FF_CHUNK = 1024


def _mix_ffn_kernel(final, *refs):
    if final:
        (x_ref, yg_ref, on_ref, nm_ref, wg_ref, wa_ref, wb_ref, wo_ref, nf_ref, up_ref, dn_ref, nfin_ref,
         o_ref) = refs
    else:
        (x_ref, yg_ref, on_ref, nm_ref, wg_ref, wa_ref, wb_ref, wo_ref, nf_ref, up_ref, dn_ref, o_ref) = refs
    x = x_ref[...]
    gt = _sigmoid(jnp.dot(_rms(x, nm_ref[...]).astype(BF16), wg_ref[...], preferred_element_type=F32))
    ya = jnp.dot(yg_ref[...], wa_ref[...], preferred_element_type=F32)
    yb = jnp.dot(on_ref[...], wb_ref[...], preferred_element_type=F32)
    merged = gt[:, 0:D_MODEL] * ya + gt[:, D_MODEL:] * yb
    x = x + jnp.dot(merged.astype(BF16), wo_ref[...], preferred_element_type=F32)
    h = _rms(x, nf_ref[...]).astype(BF16)
    acc = x
    for cf in range(D_FF // FF_CHUNK):
        u = jnp.dot(h, up_ref[:, cf * FF_CHUNK:(cf + 1) * FF_CHUNK], preferred_element_type=F32)
        u = jnp.maximum(u, 0.0)
        acc = acc + jnp.dot((u * u).astype(BF16), dn_ref[cf * FF_CHUNK:(cf + 1) * FF_CHUNK, :],
                            preferred_element_type=F32)
    if final:
        acc = _rms(acc, nfin_ref[...])
    o_ref[...] = acc


def _mix_ffn(layer, x, yg, on, nm, nf, nfin, w):
    final = layer == DEPTH - 1
    n = x.shape[0]
    tm = min(n, 512)
    small = lambda i: (0, 0)
    mine = lambda i: (layer, 0, 0)
    in_specs = [
        pl.BlockSpec((tm, D_MODEL), lambda i: (i, 0)),
        pl.BlockSpec((tm, DA), lambda i: (i, 0)),
        pl.BlockSpec((tm, DB), lambda i: (i, 0)),
        _resident((1, D_MODEL), small),
        _resident((None, D_MODEL, GATE_COLS), mine),
        _resident((None, DA, D_MODEL), mine),
        _resident((None, DB, D_MODEL), mine),
        _resident((None, D_MODEL, D_MODEL), mine),
        _resident((1, D_MODEL), small),
        _resident((None, D_MODEL, D_FF), mine),
        _resident((None, D_FF, D_MODEL), mine),
    ]
    args = [x, yg, on, nm, w["wg"], w["wa"], w["wb"], w["wo"], nf, w["up"], w["dn"]]
    if final:
        in_specs.append(_resident((1, D_MODEL), small))
        args.append(nfin)
    return pl.pallas_call(
        functools.partial(_mix_ffn_kernel, final),
        grid=(n // tm,),
        in_specs=in_specs,
        out_specs=pl.BlockSpec((tm, D_MODEL), lambda i: (i, 0)),
        out_shape=jax.ShapeDtypeStruct((n, D_MODEL), F32),
        compiler_params=pltpu.CompilerParams(dimension_semantics=("arbitrary",), vmem_limit_bytes=VMEM_LIMIT),
        name="mix_ffn_final" if final else "mix_ffn",
    )(*args)


def _pad_rows(w, rows, at=0):
    out = jnp.zeros((rows, w.shape[1]), w.dtype)
    return out.at[at:at + w.shape[0]].set(w)


def _prep_weights(p):
    w_in = p["w_in"]
    pad = ((0, 0), (0, 0), (0, GROUP_COLS - RWKV_COLS))
    return dict(
        w_rw=jnp.pad(w_in[:, :, :RWKV_COLS], pad).astype(BF16),
        w_hg=w_in[:, :, RWKV_COLS:RWKV_COLS + HGRN_COLS].astype(BF16),
        wg=w_in[:, :, RWKV_COLS + HGRN_COLS:].astype(BF16),
        wa=p["w_out_a"].astype(BF16), wb=p["w_out_b"].astype(BF16), wo=p["w_out"].astype(BF16),
        up=p["w_ffn_up"].astype(BF16), dn=p["w_ffn_down"].astype(BF16),
    )


def _prep_layer(l, p):
    mu_p = jnp.zeros((1, GROUP_COLS), F32).at[0, :RWKV_COLS].set(p["rwkv_mu"][l])
    v0 = p["rwkv_v0"][l - 1] if l > 0 else jnp.zeros((DA,), F32)
    pv = jnp.stack([p["rwkv_w0"][l], p["rwkv_a0"][l], p["rwkv_k_k"][l], p["rwkv_k_a"][l],
                    p["rwkv_r_k"][l].reshape(DA), p["rwkv_ln_w"][l], p["rwkv_ln_b"][l], v0])
    out = dict(
        mu_p=mu_p, pv=pv,
        w2_p=_pad_rows(p["rwkv_w2"][l], LANES, 0).astype(BF16),
        a2_p=_pad_rows(p["rwkv_a2"][l], LANES, W_LORA).astype(BF16),
        g2_p=_pad_rows(p["rwkv_g2"][l], LORA_G_PAD, 0).astype(BF16),
        vw1_p=None, vw2_p=None,
        norm_mix=p["norm_mix"][l].reshape(1, D_MODEL),
        hgrn_norm_w=p["hgrn_norm_w"][l].reshape(1, DB),
        nf=p["norm_ffn"][l].reshape(1, D_MODEL),
    )
    if l > 0:
        w1 = p["rwkv_vres_w1"][l - 1]
        out["vw1_p"] = jnp.zeros((DA, LANES), F32).at[:, :V_LORA].set(w1).astype(BF16)
        out["vw2_p"] = _pad_rows(p["rwkv_vres_w2"][l - 1], LANES, 0).astype(BF16)
    return out


def _seq_tile(bsz, frames, step_rows):
    bt = min(bsz, max(1, step_rows // frames))
    assert bsz % bt == 0, (bsz, bt)
    return bt


def _trunk(x, state_shift, state_rwkv, state_hgrn, layers, weights, p):
    bsz, t_len, _ = x.shape
    c_len = min(t_len, RWKV_CHUNK)
    nch = min(t_len // c_len, RWKV_CHUNKS)
    tb = min(t_len, HGRN_TILE)
    assert t_len % (nch * c_len) == 0 and t_len % tb == 0 and tb % PAIR_LEN == 0
    nfin = p["norm_final"].reshape(1, D_MODEL)
    vfirst = None
    shifts, rwkv_states, hgrn_states = [], [], []
    for l in range(DEPTH):
        lp = layers[l]
        shift0 = jnp.zeros((bsz, 1, GROUP_COLS), F32).at[:, 0, :RWKV_COLS].set(state_shift[l])
        res = _rwkv(l, x, shift0, state_rwkv, vfirst, lp["norm_mix"], weights["w_rw"], lp["mu_p"], lp["pv"],
                    lp["w2_p"], lp["a2_p"], lp["g2_p"], lp["vw1_p"], lp["vw2_p"], c_len, nch,
                    min(_seq_tile(bsz, nch * c_len, RWKV_STEP_ROWS), RWKV_STEP_UNITS // (nch * N_PAIRS)))
        if l == 0:
            yg, shift_out, s_rwkv, vfirst = res
        else:
            yg, shift_out, s_rwkv = res
        on, s_hgrn = _hgrn(l, x, lp["norm_mix"], weights["w_hg"], p["hgrn_lb_logits"], lp["hgrn_norm_w"],
                           state_hgrn, tb, min(_seq_tile(bsz, tb, HGRN_STEP_ROWS), HGRN_STEP_UNITS // B_HEADS))
        n = bsz * t_len
        x = _mix_ffn(l, x.reshape(n, D_MODEL), yg.reshape(n, DA), on.reshape(n, DB), lp["norm_mix"], lp["nf"],
                     nfin, weights).reshape(bsz, t_len, D_MODEL)
        shifts.append(shift_out[:, 0, :RWKV_COLS])
        rwkv_states.append(s_rwkv)
        hgrn_states.append(s_hgrn)
    return x, jnp.stack(shifts), jnp.stack(rwkv_states), jnp.stack(hgrn_states)


def kernel(x_prompt, x_sample, state_shift, state_rwkv, state_hgrn, norm_mix, w_in, rwkv_mu, rwkv_w0, rwkv_w2, rwkv_a0, rwkv_a2, rwkv_g2, rwkv_v0, rwkv_vres_w1, rwkv_vres_w2, rwkv_k_k, rwkv_k_a, rwkv_r_k, rwkv_ln_w, rwkv_ln_b, hgrn_lb_logits, hgrn_norm_w, w_out_a, w_out_b, w_out, norm_ffn, w_ffn_up, w_ffn_down, norm_final):
    p = dict(norm_mix=norm_mix, w_in=w_in, rwkv_mu=rwkv_mu, rwkv_w0=rwkv_w0, rwkv_w2=rwkv_w2, rwkv_a0=rwkv_a0,
             rwkv_a2=rwkv_a2, rwkv_g2=rwkv_g2, rwkv_v0=rwkv_v0, rwkv_vres_w1=rwkv_vres_w1,
             rwkv_vres_w2=rwkv_vres_w2, rwkv_k_k=rwkv_k_k, rwkv_k_a=rwkv_k_a, rwkv_r_k=rwkv_r_k,
             rwkv_ln_w=rwkv_ln_w, rwkv_ln_b=rwkv_ln_b, hgrn_lb_logits=hgrn_lb_logits, hgrn_norm_w=hgrn_norm_w,
             w_out_a=w_out_a, w_out_b=w_out_b, w_out=w_out, norm_ffn=norm_ffn, w_ffn_up=w_ffn_up,
             w_ffn_down=w_ffn_down, norm_final=norm_final)
    layers = [_prep_layer(l, p) for l in range(DEPTH)]
    weights = _prep_weights(p)
    bp = x_prompt.shape[0]
    dt = x_prompt.dtype
    zero_shift = jnp.zeros((DEPTH, bp, RWKV_COLS), dt)
    zero_rwkv = jnp.zeros((DEPTH, bp, A_HEADS, A_HEAD_DIM, A_HEAD_DIM), dt)
    zero_hgrn = jnp.zeros((DEPTH, bp, B_HEADS, B_KEY_DIM, B_VAL_DIM), dt)
    y_prompt, shift_p, rwkv_p, hgrn_p = _trunk(x_prompt, zero_shift, zero_rwkv, zero_hgrn, layers, weights, p)
    y_sample, shift_s, rwkv_s, hgrn_s = _trunk(x_sample, state_shift, state_rwkv, state_hgrn, layers, weights, p)
    return (y_prompt, y_sample, shift_p, rwkv_p, hgrn_p, shift_s, rwkv_s, hgrn_s)
```

```python
import functools
import math

import jax
import jax.numpy as jnp
from jax import lax
from jax.experimental import pallas as pl
from jax.experimental.pallas import tpu as pltpu

F32 = jnp.float32
BF16 = jnp.bfloat16

D_MODEL = 1024
DEPTH = 2
A_HEADS = 8
A_HEAD_DIM = 64
DA = A_HEADS * A_HEAD_DIM
W_LORA = 64
A_LORA = 64
V_LORA = 32
G_LORA = 160
RWKV_COLS = 3 * DA + W_LORA + A_LORA + G_LORA
B_HEADS = 4
B_KEY_DIM = 128
B_VAL_DIM = 128
BK = B_HEADS * B_KEY_DIM
DB = B_HEADS * B_VAL_DIM
HGRN_COLS = 2 * BK + 2 * DB
GATE_COLS = 2 * D_MODEL
D_FF = 4 * D_MODEL
HGRN_BLOCK = 16
RMS_EPS = 1e-6
GN_EPS = 64e-5

LANES = 128
GROUP_COLS = 2048
LORA_WA = 3 * DA
LORA_G = LORA_WA + W_LORA + A_LORA
LORA_G_PAD = 256
PAIR = 2 * A_HEAD_DIM
N_PAIRS = A_HEADS // 2
PAIR_LEN = 2 * HGRN_BLOCK
STATE_LEN = 2 * PAIR_LEN
VMEM_LIMIT = 56 * 1024 * 1024

RWKV_CHUNK = 64
RWKV_CHUNKS = 4
RWKV_STEP_ROWS = 512
RWKV_STEP_UNITS = 32
CUMSUM_ROWS = 256
HGRN_TILE = 512
HGRN_STEP_ROWS = 1024
HGRN_STEP_UNITS = 64
SCORE_ROWS = 128


def _dot(a, b):
    return jnp.dot(a.astype(BF16), b.astype(BF16), preferred_element_type=F32)


def _dot_nt(a, b):
    return lax.dot_general(a.astype(BF16), b.astype(BF16), (((1,), (1,)), ((), ())), preferred_element_type=F32)


def _dot_tn(a, b):
    return lax.dot_general(a.astype(BF16), b.astype(BF16), (((0,), (0,)), ((), ())), preferred_element_type=F32)


def _split3(x):
    hi = x.astype(BF16)
    r1 = x - hi.astype(F32)
    mid = r1.astype(BF16)
    return hi, mid, (r1 - mid.astype(F32)).astype(BF16)


def _dot_01(m01, x):
    return functools.reduce(lambda a, b: a + b, [jnp.dot(m01, t, preferred_element_type=F32) for t in _split3(x)])


def _dot_x01(x, m01):
    return functools.reduce(lambda a, b: a + b, [jnp.dot(t, m01, preferred_element_type=F32) for t in _split3(x)])


def _rms(x, w):
    return x * lax.rsqrt(jnp.mean(x * x, axis=-1, keepdims=True) + RMS_EPS) * w


def _sigmoid(x):
    return 1.0 / (1.0 + jnp.exp(-x))


def _log2(n):
    assert n & (n - 1) == 0, n
    return n.bit_length() - 1


def _resident(shape, index_map):
    return pl.BlockSpec(shape, index_map, pipeline_mode=pl.Buffered(1))


_DONE = object()


def _round_robin(*gens):
    live = list(gens)
    while live:
        for gen in list(live):
            if next(gen, _DONE) is _DONE:
                live.remove(gen)
            else:
                yield


def _rwkv_part(layer, bt, nch, c_len, c, h, refs):
    if layer == 0:
        (shift_ref, s0_ref, w_ref, mu_ref, pv_ref, w2_ref, a2_ref, g2_ref,
         yg_ref, shift_out_ref, s_out_ref, vfirst_out_ref, xs_ref, s_ref) = refs
    else:
        (shift_ref, s0_ref, vfirst_ref, w_ref, mu_ref, pv_ref, w2_ref, a2_ref, g2_ref, vw1_ref, vw2_ref,
         yg_ref, shift_out_ref, s_out_ref, xs_ref, s_ref) = refs
    frames = nch * c_len
    rows = bt * frames
    stk = 2 * c_len

    hrow = lax.broadcasted_iota(jnp.int32, (A_HEAD_DIM, PAIR), 0)
    hlane = lax.broadcasted_iota(jnp.int32, (A_HEAD_DIM, PAIR), 1)
    to_lo = jnp.where(hlane == hrow, 1.0, 0.0).astype(BF16)
    to_hi = jnp.where(hlane == hrow + A_HEAD_DIM, 1.0, 0.0).astype(BF16)

    @pl.when(c == 0)
    def _():
        for b in range(bt):
            xs_ref[b, 7:8, :] = shift_ref[b]
            for p in range(N_PAIRS):
                s_ref[b, p] = jnp.concatenate(
                    [_dot_x01(s0_ref[b, 2 * p], to_lo), _dot_x01(s0_ref[b, 2 * p + 1], to_hi)], axis=0)

    def project(c0, c1):
        rw = jnp.dot(h, w_ref[:, c0:c1], preferred_element_type=F32)
        prevs = []
        for b in range(bt):
            rw_b = rw[b * frames:(b + 1) * frames]
            xs_ref[b, 8:8 + frames, c0:c1] = rw_b
            prevs.append(xs_ref[b, 7:7 + frames, c0:c1])
            last = rw_b[frames - 1:frames, :]
            xs_ref[b, 7:8, c0:c1] = last
            shift_out_ref[b, :, c0:c1] = last
        return rw + (jnp.concatenate(prevs, axis=0) - rw) * mu_ref[:, c0:c1]

    w0 = pv_ref[0:1, :]
    a0 = pv_ref[1:2, :]
    k_k = pv_ref[2:3, :]
    k_a = pv_ref[3:4, :]
    r_k = pv_ref[4:5, :]
    ln_w = pv_ref[5:6, :]
    ln_b = pv_ref[6:7, :]

    cs_rows = min(rows, CUMSUM_ROWS)
    row = lax.broadcasted_iota(jnp.int32, (cs_rows, cs_rows), 0)
    col = lax.broadcasted_iota(jnp.int32, (cs_rows, cs_rows), 1)
    chunk_shift = _log2(c_len)
    causal01 = jnp.where(jnp.logical_and(
        lax.shift_right_logical(row, chunk_shift) == lax.shift_right_logical(col, chunk_shift), row >= col),
        1.0, 0.0).astype(BF16)

    lora = project(LORA_WA, GROUP_COLS)
    yield "prep"
    k = project(DA, 2 * DA)
    yield "prep"
    wa = lora[:, 0:LORA_G - LORA_WA]
    gl = lora[:, LORA_G - LORA_WA:LORA_G - LORA_WA + LORA_G_PAD]
    w_raw = w0 + _dot(jnp.tanh(wa), w2_ref[...])
    a_gate = _sigmoid(a0 + _dot(wa, a2_ref[...]))
    g = _dot(_sigmoid(gl), g2_ref[...])
    yield "prep"
    r = project(0, DA)
    yield "prep"
    lw = (-math.exp(-0.5)) * _sigmoid(w_raw)
    g_cum = jnp.concatenate([_dot_01(causal01, lw[i:i + cs_rows]) for i in range(0, rows, cs_rows)],
                            axis=0)
    yield "prep"
    v = project(2 * DA, 3 * DA)
    yield "prep"
    if layer == 0:
        for b in range(bt):
            vfirst_out_ref[b] = v[b * frames:(b + 1) * frames]
    else:
        v0 = pv_ref[7:8, :]
        vg = _sigmoid(v0 + _dot(_dot(v, vw1_ref[...]), vw2_ref[...]))
        vfirst = jnp.concatenate([vfirst_ref[b] for b in range(bt)], axis=0)
        v = v + (vfirst - v) * vg

    lane = lax.broadcasted_iota(jnp.int32, (PAIR, PAIR), 1)
    sub = lax.broadcasted_iota(jnp.int32, (PAIR, PAIR), 0)
    head_ones = jnp.where((lane < A_HEAD_DIM) == (sub < A_HEAD_DIM), 1.0, 0.0).astype(BF16)

    def seg_sum(x):
        return jnp.concatenate(
            [jnp.dot(x[:, p * PAIR:(p + 1) * PAIR].astype(BF16), head_ones, preferred_element_type=F32)
             for p in range(N_PAIRS)], axis=1)

    kk = k * k_k
    kk = kk * lax.rsqrt(jnp.maximum(seg_sum(kk * kk), 1e-24))
    yield "prep"
    k_h = k * (1.0 + (a_gate - 1.0) * k_a)
    a_vec = -kk
    b_vec = kk * a_gate

    slabs = {}

    def slab(b, ch):
        if (b, ch) not in slabs:
            r0 = (b * nch + ch) * c_len
            rs = slice(r0, r0 + c_len)
            gc = g_cum[rs]
            ge = jnp.broadcast_to(gc[c_len - 1:c_len, :], (c_len, DA))
            e_neg = jnp.exp(-gc)
            e_end = jnp.exp(ge - gc)
            slabs[(b, ch)] = dict(
                rq=r[rs] * jnp.exp(gc), aq=a_vec[rs] * jnp.exp(gc - lw[rs]), kn=k_h[rs] * e_neg,
                bn=b_vec[rs] * e_neg, k_end=k_h[rs] * e_end, b_end=b_vec[rs] * e_end,
                decay_end=jnp.exp(ge[0:1, :]), v=v[rs])
        return slabs[(b, ch)]

    srow = lax.broadcasted_iota(jnp.int32, (stk, PAIR), 0)
    slane = lax.broadcasted_iota(jnp.int32, (stk, PAIR), 1)
    own_lanes = (srow < c_len) == (slane < A_HEAD_DIM)

    def stack(x):
        return jnp.where(own_lanes, jnp.concatenate([x, x], axis=0), 0.0).astype(BF16)

    mrow = lax.broadcasted_iota(jnp.int32, (stk, stk), 0)
    mcol = lax.broadcasted_iota(jnp.int32, (stk, stk), 1)
    strict = (mrow & (c_len - 1)) > (mcol & (c_len - 1))
    lower = (mrow & (c_len - 1)) >= (mcol & (c_len - 1))
    eye = jnp.where(mrow == mcol, 1.0, 0.0).astype(F32)

    def cut(name, u):
        b, ch, p = u
        return slab(b, ch)[name][:, p * PAIR:(p + 1) * PAIR]

    lhs, v_st, a_ak_v, a_rk, a_rb, t_inv, y_parts = {}, {}, {}, {}, {}, {}, {}

    def scores_and_inverse(units):
        power = {}
        for u in units:
            lhs[u] = jnp.concatenate([stack(cut("aq", u)), stack(cut("rq", u))], axis=0)
            v_st[u] = stack(cut("v", u))
            sc_b = _dot_nt(lhs[u], stack(cut("bn", u)))
            sc_k = _dot_nt(lhs[u], stack(cut("kn", u)))
            power[u] = jnp.where(strict, sc_b[:stk], 0.0)
            a_rb[u] = jnp.where(lower, sc_b[stk:], 0.0)
            a_rk[u] = jnp.where(lower, sc_k[stk:], 0.0)
            a_ak_v[u] = jnp.where(strict, sc_k[:stk], 0.0)
        yield
        for u in units:
            t_inv[u] = eye + power[u]
            a_ak_v[u] = _dot(a_ak_v[u], v_st[u])
            power[u] = _dot(power[u], power[u])
        yield
        for lvl in range(1, _log2(c_len)):
            new_power = {}
            for u in units:
                t_inv[u] = t_inv[u] + _dot(power[u], t_inv[u])
                if lvl < _log2(c_len) - 1:
                    new_power[u] = _dot(power[u], power[u])
            power = new_power
            yield

    def apply_state(units):
        from_state = {u: _dot_nt(lhs[u], s_ref[u[0], u[2]]) for u in units}
        yield
        corr = {u: _dot(t_inv[u], from_state[u][:stk] + a_ak_v[u]) for u in units}
        yield
        for u in units:
            vu = jnp.concatenate([v_st[u], corr[u].astype(BF16)], axis=0)
            if stk % LANES == 0:
                y_st = from_state[u][stk:] + _dot(jnp.concatenate([a_rk[u], a_rb[u]], axis=1), vu)
            else:
                y_st = from_state[u][stk:] + _dot(a_rk[u], v_st[u]) + _dot(a_rb[u], corr[u])
            y_parts[u] = y_st[:c_len] + y_st[c_len:]
            ends = jnp.concatenate([stack(cut("k_end", u)), stack(cut("b_end", u))], axis=0)
            s_ref[u[0], u[2]] = s_ref[u[0], u[2]] * cut("decay_end", u) + _dot_tn(vu, ends)
        yield

    chunk_units = [[(b, ch, p) for b in range(bt) for p in range(N_PAIRS)] for ch in range(nch)]
    for _ in scores_and_inverse(chunk_units[0]):
        yield "dense"
    for ch in range(nch):
        if ch + 1 < nch:
            for _ in _round_robin(scores_and_inverse(chunk_units[ch + 1]), apply_state(chunk_units[ch])):
                yield "dense"
        else:
            for _ in apply_state(chunk_units[ch]):
                yield "tail"

    y = jnp.concatenate(
        [jnp.concatenate([y_parts[(b, ch, p)] for p in range(N_PAIRS)], axis=1)
         for b in range(bt) for ch in range(nch)], axis=0)
    inv_n = 1.0 / A_HEAD_DIM
    mean = seg_sum(y) * inv_n
    bonus = seg_sum(r * k_h * r_k)
    yield "tail"
    dev = y - mean
    var = seg_sum(dev * dev) * inv_n
    yield "tail"
    y = dev * lax.rsqrt(var + GN_EPS) * ln_w + ln_b
    y = ((y + bonus * v) * g).astype(BF16)
    for b in range(bt):
        yg_ref[b] = y[b * frames:(b + 1) * frames]

    @pl.when(c == pl.num_programs(1) - 1)
    def _():
        prow = lax.broadcasted_iota(jnp.int32, (PAIR, A_HEAD_DIM), 0)
        pcol = lax.broadcasted_iota(jnp.int32, (PAIR, A_HEAD_DIM), 1)
        from_lo = jnp.where(prow == pcol, 1.0, 0.0).astype(BF16)
        from_hi = jnp.where(prow == pcol + A_HEAD_DIM, 1.0, 0.0).astype(BF16)
        for b in range(bt):
            for p in range(N_PAIRS):
                s_pair = s_ref[b, p]
                s_out_ref[b, 2 * p] = _dot_x01(s_pair[:A_HEAD_DIM], from_lo)
                s_out_ref[b, 2 * p + 1] = _dot_x01(s_pair[A_HEAD_DIM:], from_hi)


def _hgrn_part(layer, bt, tb, c, h, refs):
    (w_ref, lg_ref, nw_ref, s0_ref, on_ref, s_out_ref, st_ref) = refs
    n_blk = tb // HGRN_BLOCK

    def project(c0):
        return [jnp.dot(h[b * tb:(b + 1) * tb], w_ref[:, c0:c0 + BK], preferred_element_type=F32)
                for b in range(bt)]

    fz_all = project(BK)
    yield
    q_all = project(0)
    yield

    @pl.when(c == 0)
    def _():
        for b in range(bt):
            for hd in range(B_HEADS):
                st_ref[b, hd] = s0_ref[b, hd].T

    lg = lg_ref[...]
    lrows = [lg[i:i + 1, :] for i in range(DEPTH)]
    mx = functools.reduce(jnp.maximum, lrows)
    ex = [jnp.exp(x - mx) for x in lrows]
    den = functools.reduce(lambda a, b: a + b, ex)
    sm = [e / den for e in ex]
    lb = functools.reduce(lambda a, b: a + b, sm[:layer + 1]) - sm[0]

    cs_rows = min(tb, CUMSUM_ROWS)
    row = lax.broadcasted_iota(jnp.int32, (cs_rows, cs_rows), 0)
    col = lax.broadcasted_iota(jnp.int32, (cs_rows, cs_rows), 1)
    blk_shift = _log2(HGRN_BLOCK)
    causal = jnp.logical_and(lax.shift_right_logical(row, blk_shift) == lax.shift_right_logical(col, blk_shift),
                             row >= col)
    causal01 = jnp.where(causal, 1.0, 0.0).astype(BF16)

    f_all = [lb + (1.0 - lb) * _sigmoid(fz_all[b]) for b in range(bt)]
    g_cum_all = [jnp.concatenate([_dot_01(causal01, jnp.log(f_all[b][i:i + cs_rows]))
                                  for i in range(0, tb, cs_rows)], axis=0) for b in range(bt)]
    yield
    iv = project(2 * BK)
    yield

    state_len = min(tb, STATE_LEN)
    two_pairs = state_len == 2 * PAIR_LEN
    frow = lax.broadcasted_iota(jnp.int32, (tb, BK), 0)
    second = (lax.shift_right_logical(frow, blk_shift) & 1) == 1
    second_pair = (lax.shift_right_logical(frow, blk_shift + 1) & 1) == 1
    zeros_blk = jnp.zeros((HGRN_BLOCK, BK), F32)
    zeros_pair = jnp.zeros((PAIR_LEN, BK), F32)
    sr = min(tb, SCORE_ROWS)
    srow = lax.broadcasted_iota(jnp.int32, (sr, sr), 0)
    scol = lax.broadcasted_iota(jnp.int32, (sr, sr), 1)

    def split_mask(hl):
        grp = _log2(2 * hl)
        return jnp.logical_and(
            lax.shift_right_logical(srow, grp) == lax.shift_right_logical(scol, grp),
            jnp.logical_and((srow & (2 * hl - 1)) >= hl, (scol & (2 * hl - 1)) < hl))

    pair_mask = split_mask(HGRN_BLOCK)

    levels = [HGRN_BLOCK >> (i + 1) for i in range(blk_shift)]
    pos_in = {hl: frow & (2 * hl - 1) for hl in levels}
    level_mask = {hl: split_mask(hl) for hl in levels}
    toward_anchor = {hl: jnp.where(pos_in[hl] >= hl, 1.0, -1.0).astype(F32) for hl in levels}

    def anchor(g, hl):
        if 2 * hl >= 8:
            g3 = g.reshape(tb // (2 * hl), 2 * hl, BK)
            return jnp.broadcast_to(g3[:, hl - 1:hl, :], g3.shape).reshape(tb, BK)
        out = g
        for pos in range(2 * hl):
            if pos != hl - 1:
                out = jnp.where(pos_in[hl] == pos, pltpu.roll(g, (pos - (hl - 1)) % tb, 0), out)
        return out

    q_lvl = {hl: [] for hl in levels}
    k_lvl = {hl: [] for hl in levels}
    own = []
    qg, qg_pair, k_end, k_end_pair, q_state, k_state, decay_state = [], [], [], [], [], [], []
    for b in range(bt):
        q = q_all[b]
        kx = 1.0 - f_all[b]
        g_cum = g_cum_all[b]
        qs = q * _sigmoid(q)
        own.append(qs * kx)
        for hl in levels:
            partial = jnp.exp((g_cum - anchor(g_cum, hl)) * toward_anchor[hl])
            q_lvl[hl].append((qs * partial).astype(BF16))
            k_lvl[hl].append((kx * partial).astype(BF16))
        g_end = jnp.concatenate(
            [jnp.broadcast_to(g_cum[(j + 1) * HGRN_BLOCK - 1:(j + 1) * HGRN_BLOCK, :], (HGRN_BLOCK, BK))
             for j in range(n_blk)], axis=0)
        decay = jnp.exp(g_end)
        decay_prev = jnp.concatenate([zeros_blk, decay[:-HGRN_BLOCK]], axis=0)
        decay_next = jnp.concatenate([decay[HGRN_BLOCK:], zeros_blk], axis=0)
        q_dec = qs * jnp.exp(g_cum)
        k_to_end = kx * jnp.exp(g_end - g_cum)
        qg.append(q_dec.astype(BF16))
        q_pair = q_dec * jnp.where(second, decay_prev, 1.0)
        k_pair = k_to_end * jnp.where(second, 1.0, decay_next)
        qg_pair.append(q_pair.astype(BF16))
        k_end.append(k_to_end.astype(BF16))
        k_end_pair.append(k_pair.astype(BF16))
        pair_tot = decay * jnp.where(second, decay_prev, decay_next)
        if two_pairs:
            tot_prev = jnp.concatenate([zeros_pair, pair_tot[:-PAIR_LEN]], axis=0)
            tot_next = jnp.concatenate([pair_tot[PAIR_LEN:], zeros_pair], axis=0)
            q_state.append((q_pair * jnp.where(second_pair, tot_prev, 1.0)).astype(BF16))
            k_state.append((k_pair * jnp.where(second_pair, 1.0, tot_next)).astype(BF16))
            decay_state.append(pair_tot * tot_next)
        else:
            q_state.append(qg_pair[-1])
            k_state.append(k_end_pair[-1])
            decay_state.append(pair_tot)

    units = [(b, hd) for b in range(bt) for hd in range(B_HEADS)]
    n_state = tb // state_len

    def cut(x, u):
        return x[u[0]][:, u[1] * B_KEY_DIM:(u[1] + 1) * B_KEY_DIM]

    def span(j):
        return slice(j * state_len, (j + 1) * state_len)

    def tile(i):
        return slice(i * sr, (i + 1) * sr)

    def scores(mask, qx, kx_, below):
        return {(u, i): jnp.where(mask, _dot_nt(cut(qx, u)[tile(i)], cut(kx_, u)[tile(i)]),
                                  0.0 if below is None else below[(u, i)])
                for u in units for i in range(tb // sr)}

    att = scores(pair_mask, qg, k_end, None)
    yield
    if two_pairs:
        att = scores(split_mask(PAIR_LEN), qg_pair, k_end_pair, att)
        yield
    for hl in levels:
        att = scores(level_mask[hl], q_lvl[hl], k_lvl[hl], att)
        yield
    upd = {}
    for j in range(n_state):
        for u in units:
            upd[(u, j)] = _dot_tn(cut(iv, u)[span(j)], cut(k_state, u)[span(j)])
        yield
    starts = {}
    for u in units:
        s = st_ref[u[0], u[1]]
        dec = cut(decay_state, u)
        for j in range(n_state):
            starts[(u, j)] = s.astype(BF16)
            s = s * dec[j * state_len:j * state_len + 1, :] + upd[(u, j)]
        st_ref[u[0], u[1]] = s
    og = project(2 * BK + DB)
    yield
    o_acc = {u: jnp.concatenate([_dot(att[(u, i)], cut(iv, u)[tile(i)]) for i in range(tb // sr)], axis=0)
             for u in units}
    yield
    inter = {}
    for j in range(n_state):
        for u in units:
            inter[(u, j)] = _dot_nt(cut(q_state, u)[span(j)], starts[(u, j)])
        yield

    for b in range(bt):
        outs = []
        for hd in range(B_HEADS):
            u = (b, hd)
            o_h = o_acc[u] + jnp.concatenate([inter[(u, j)] for j in range(n_state)], axis=0)
            o_h = o_h + jnp.sum(cut(own, u), axis=-1, keepdims=True) * cut(iv, u)
            outs.append(o_h * lax.rsqrt(jnp.mean(o_h * o_h, axis=-1, keepdims=True) + RMS_EPS))
        o = jnp.concatenate(outs, axis=1)
        on_ref[b] = (o * nw_ref[...] * (og[b] * _sigmoid(og[b]))).astype(BF16)

    @pl.when(c == pl.num_programs(1) - 1)
    def _():
        for b in range(bt):
            for hd in range(B_HEADS):
                s_out_ref[b, hd] = st_ref[b, hd].T


def _drive(gen):
    for _ in gen:
        pass


def _normalised(x_ref, nm_ref, bt):
    return jnp.concatenate([_rms(x_ref[b], nm_ref[...]).astype(BF16) for b in range(bt)], axis=0)


def _rwkv_kernel(layer, bt, nch, c_len, x_ref, nm_ref, *refs):
    _drive(_rwkv_part(layer, bt, nch, c_len, pl.program_id(1), _normalised(x_ref, nm_ref, bt), refs))


def _hgrn_kernel(layer, bt, tb, x_ref, nm_ref, *refs):
    _drive(_hgrn_part(layer, bt, tb, pl.program_id(1), _normalised(x_ref, nm_ref, bt), refs))


def _rwkv(layer, x, shift0, s_all, vfirst, nm, w_rw_all, mu_p, pv, w2_p, a2_p, g2_p, vw1_p, vw2_p, c_len, nch, bt):
    bsz, t_len, _ = x.shape
    frames = nch * c_len
    const2 = lambda b, c: (0, 0)
    step = lambda b, c: (b, c, 0)
    per_seq3 = lambda b, c: (b, 0, 0)
    per_seq4 = lambda b, c: (b, 0, 0, 0)
    in_specs = [
        pl.BlockSpec((bt, frames, D_MODEL), step),
        pl.BlockSpec((1, D_MODEL), const2),
        pl.BlockSpec((bt, 1, GROUP_COLS), per_seq3),
        pl.BlockSpec((None, bt, A_HEADS, A_HEAD_DIM, A_HEAD_DIM), lambda b, c: (layer, b, 0, 0, 0)),
    ]
    args = [x, nm, shift0, s_all]
    if layer > 0:
        in_specs.append(pl.BlockSpec((bt, frames, DA), step))
        args.append(vfirst)
    in_specs += [
        _resident((None, D_MODEL, GROUP_COLS), lambda b, c: (layer, 0, 0)),
        pl.BlockSpec((1, GROUP_COLS), const2),
        pl.BlockSpec((8, DA), const2),
        pl.BlockSpec((LANES, DA), const2),
        pl.BlockSpec((LANES, DA), const2),
        pl.BlockSpec((LORA_G_PAD, DA), const2),
    ]
    args += [w_rw_all, mu_p, pv, w2_p, a2_p, g2_p]
    if layer > 0:
        in_specs += [pl.BlockSpec((DA, LANES), const2), pl.BlockSpec((LANES, DA), const2)]
        args += [vw1_p, vw2_p]
    out_specs = [
        pl.BlockSpec((bt, frames, DA), step),
        pl.BlockSpec((bt, 1, GROUP_COLS), per_seq3),
        pl.BlockSpec((bt, A_HEADS, A_HEAD_DIM, A_HEAD_DIM), per_seq4),
    ]
    out_shape = [
        jax.ShapeDtypeStruct((bsz, t_len, DA), BF16),
        jax.ShapeDtypeStruct((bsz, 1, GROUP_COLS), F32),
        jax.ShapeDtypeStruct((bsz, A_HEADS, A_HEAD_DIM, A_HEAD_DIM), F32),
    ]
    if layer == 0:
        out_specs.append(pl.BlockSpec((bt, frames, DA), step))
        out_shape.append(jax.ShapeDtypeStruct((bsz, t_len, DA), F32))
    return pl.pallas_call(
        functools.partial(_rwkv_kernel, layer, bt, nch, c_len),
        grid=(bsz // bt, t_len // frames),
        in_specs=in_specs,
        out_specs=out_specs,
        out_shape=out_shape,
        scratch_shapes=[pltpu.VMEM((bt, frames + 8, GROUP_COLS), F32),
                        pltpu.VMEM((bt, N_PAIRS, PAIR, PAIR), F32)],
        compiler_params=pltpu.CompilerParams(
            dimension_semantics=("arbitrary", "arbitrary"), vmem_limit_bytes=VMEM_LIMIT),
        name=f"rwkv{layer}",
    )(*args)


def _hgrn(layer, x, nm, w_hg_all, lb_logits, norm_w, s_all, tb, bt):
    bsz, t_len, _ = x.shape
    const2 = lambda b, t: (0, 0)
    return pl.pallas_call(
        functools.partial(_hgrn_kernel, layer, bt, tb),
        grid=(bsz // bt, t_len // tb),
        in_specs=[
            pl.BlockSpec((bt, tb, D_MODEL), lambda b, t: (b, t, 0)),
            pl.BlockSpec((1, D_MODEL), const2),
            _resident((None, D_MODEL, HGRN_COLS), lambda b, t: (layer, 0, 0)),
            pl.BlockSpec((DEPTH, BK), const2),
            pl.BlockSpec((1, DB), const2),
            pl.BlockSpec((None, bt, B_HEADS, B_KEY_DIM, B_VAL_DIM), lambda b, t: (layer, b, 0, 0, 0)),
        ],
        out_specs=[
            pl.BlockSpec((bt, tb, DB), lambda b, t: (b, t, 0)),
            pl.BlockSpec((bt, B_HEADS, B_KEY_DIM, B_VAL_DIM), lambda b, t: (b, 0, 0, 0)),
        ],
        out_shape=[
            jax.ShapeDtypeStruct((bsz, t_len, DB), BF16),
            jax.ShapeDtypeStruct((bsz, B_HEADS, B_KEY_DIM, B_VAL_DIM), F32),
        ],
        scratch_shapes=[pltpu.VMEM((bt, B_HEADS, B_VAL_DIM, B_KEY_DIM), F32)],
        compiler_params=pltpu.CompilerParams(
            dimension_semantics=("arbitrary", "arbitrary"), vmem_limit_bytes=VMEM_LIMIT),
        name=f"hgrn{layer}",
    )(x, nm, w_hg_all, lb_logits, norm_w, s_all)


FF_CHUNK = 512


def _mix_ffn_kernel(final, *refs):
    if final:
        (x_ref, yg_ref, on_ref, nm_ref, wg_ref, wa_ref, wb_ref, wo_ref, nf_ref, up_ref, dn_ref, nfin_ref,
         o_ref) = refs
    else:
        (x_ref, yg_ref, on_ref, nm_ref, wg_ref, wa_ref, wb_ref, wo_ref, nf_ref, up_ref, dn_ref, o_ref) = refs
    x = x_ref[...]
    gt = _sigmoid(jnp.dot(_rms(x, nm_ref[...]).astype(BF16), wg_ref[...], preferred_element_type=F32))
    ya = jnp.dot(yg_ref[...], wa_ref[...], preferred_element_type=F32)
    yb = jnp.dot(on_ref[...], wb_ref[...], preferred_element_type=F32)
    merged = gt[:, 0:D_MODEL] * ya + gt[:, D_MODEL:] * yb
    x = x + jnp.dot(merged.astype(BF16), wo_ref[...], preferred_element_type=F32)
    h = _rms(x, nf_ref[...]).astype(BF16)
    acc = x
    for cf in range(D_FF // FF_CHUNK):
        u = jnp.dot(h, up_ref[:, cf * FF_CHUNK:(cf + 1) * FF_CHUNK], preferred_element_type=F32)
        u = jnp.maximum(u, 0.0)
        acc = acc + jnp.dot((u * u).astype(BF16), dn_ref[cf * FF_CHUNK:(cf + 1) * FF_CHUNK, :],
                            preferred_element_type=F32)
    if final:
        acc = _rms(acc, nfin_ref[...])
    o_ref[...] = acc


def _mix_ffn(layer, x, yg, on, nm, nf, nfin, w):
    final = layer == DEPTH - 1
    n = x.shape[0]
    tm = min(n, 1024)
    small = lambda i: (0, 0)
    mine = lambda i: (layer, 0, 0)
    in_specs = [
        pl.BlockSpec((tm, D_MODEL), lambda i: (i, 0)),
        pl.BlockSpec((tm, DA), lambda i: (i, 0)),
        pl.BlockSpec((tm, DB), lambda i: (i, 0)),
        _resident((1, D_MODEL), small),
        _resident((None, D_MODEL, GATE_COLS), mine),
        _resident((None, DA, D_MODEL), mine),
        _resident((None, DB, D_MODEL), mine),
        _resident((None, D_MODEL, D_MODEL), mine),
        _resident((1, D_MODEL), small),
        _resident((None, D_MODEL, D_FF), mine),
        _resident((None, D_FF, D_MODEL), mine),
    ]
    args = [x, yg, on, nm, w["wg"], w["wa"], w["wb"], w["wo"], nf, w["up"], w["dn"]]
    if final:
        in_specs.append(_resident((1, D_MODEL), small))
        args.append(nfin)
    return pl.pallas_call(
        functools.partial(_mix_ffn_kernel, final),
        grid=(n // tm,),
        in_specs=in_specs,
        out_specs=pl.BlockSpec((tm, D_MODEL), lambda i: (i, 0)),
        out_shape=jax.ShapeDtypeStruct((n, D_MODEL), F32),
        compiler_params=pltpu.CompilerParams(dimension_semantics=("arbitrary",), vmem_limit_bytes=VMEM_LIMIT),
        name="mix_ffn_final" if final else "mix_ffn",
    )(*args)


def _pad_rows(w, rows, at=0):
    out = jnp.zeros((rows, w.shape[1]), w.dtype)
    return out.at[at:at + w.shape[0]].set(w)


def _prep_weights(p):
    w_in = p["w_in"]
    pad = ((0, 0), (0, 0), (0, GROUP_COLS - RWKV_COLS))
    return dict(
        w_rw=jnp.pad(w_in[:, :, :RWKV_COLS], pad).astype(BF16),
        w_hg=w_in[:, :, RWKV_COLS:RWKV_COLS + HGRN_COLS].astype(BF16),
        wg=w_in[:, :, RWKV_COLS + HGRN_COLS:].astype(BF16),
        wa=p["w_out_a"].astype(BF16), wb=p["w_out_b"].astype(BF16), wo=p["w_out"].astype(BF16),
        up=p["w_ffn_up"].astype(BF16), dn=p["w_ffn_down"].astype(BF16),
    )


def _prep_layer(l, p):
    mu_p = jnp.zeros((1, GROUP_COLS), F32).at[0, :RWKV_COLS].set(p["rwkv_mu"][l])
    v0 = p["rwkv_v0"][l - 1] if l > 0 else jnp.zeros((DA,), F32)
    pv = jnp.stack([p["rwkv_w0"][l], p["rwkv_a0"][l], p["rwkv_k_k"][l], p["rwkv_k_a"][l],
                    p["rwkv_r_k"][l].reshape(DA), p["rwkv_ln_w"][l], p["rwkv_ln_b"][l], v0])
    out = dict(
        mu_p=mu_p, pv=pv,
        w2_p=_pad_rows(p["rwkv_w2"][l], LANES, 0).astype(BF16),
        a2_p=_pad_rows(p["rwkv_a2"][l], LANES, W_LORA).astype(BF16),
        g2_p=_pad_rows(p["rwkv_g2"][l], LORA_G_PAD, 0).astype(BF16),
        vw1_p=None, vw2_p=None,
        norm_mix=p["norm_mix"][l].reshape(1, D_MODEL),
        hgrn_norm_w=p["hgrn_norm_w"][l].reshape(1, DB),
        nf=p["norm_ffn"][l].reshape(1, D_MODEL),
    )
    if l > 0:
        w1 = p["rwkv_vres_w1"][l - 1]
        out["vw1_p"] = jnp.zeros((DA, LANES), F32).at[:, :V_LORA].set(w1).astype(BF16)
        out["vw2_p"] = _pad_rows(p["rwkv_vres_w2"][l - 1], LANES, 0).astype(BF16)
    return out


def _seq_tile(bsz, frames, step_rows):
    bt = min(bsz, max(1, step_rows // frames))
    assert bsz % bt == 0, (bsz, bt)
    return bt


def _trunk(x, state_shift, state_rwkv, state_hgrn, layers, weights, p):
    bsz, t_len, _ = x.shape
    c_len = min(t_len, RWKV_CHUNK)
    nch = min(t_len // c_len, RWKV_CHUNKS)
    tb = min(t_len, HGRN_TILE)
    assert t_len % (nch * c_len) == 0 and t_len % tb == 0 and tb % PAIR_LEN == 0
    nfin = p["norm_final"].reshape(1, D_MODEL)
    vfirst = None
    shifts, rwkv_states, hgrn_states = [], [], []
    for l in range(DEPTH):
        lp = layers[l]
        shift0 = jnp.zeros((bsz, 1, GROUP_COLS), F32).at[:, 0, :RWKV_COLS].set(state_shift[l])
        res = _rwkv(l, x, shift0, state_rwkv, vfirst, lp["norm_mix"], weights["w_rw"], lp["mu_p"], lp["pv"],
                    lp["w2_p"], lp["a2_p"], lp["g2_p"], lp["vw1_p"], lp["vw2_p"], c_len, nch,
                    min(_seq_tile(bsz, nch * c_len, RWKV_STEP_ROWS), RWKV_STEP_UNITS // (nch * N_PAIRS)))
        if l == 0:
            yg, shift_out, s_rwkv, vfirst = res
        else:
            yg, shift_out, s_rwkv = res
        on, s_hgrn = _hgrn(l, x, lp["norm_mix"], weights["w_hg"], p["hgrn_lb_logits"], lp["hgrn_norm_w"],
                           state_hgrn, tb, min(_seq_tile(bsz, tb, HGRN_STEP_ROWS), HGRN_STEP_UNITS // B_HEADS))
        n = bsz * t_len
        x = _mix_ffn(l, x.reshape(n, D_MODEL), yg.reshape(n, DA), on.reshape(n, DB), lp["norm_mix"], lp["nf"],
                     nfin, weights).reshape(bsz, t_len, D_MODEL)
        shifts.append(shift_out[:, 0, :RWKV_COLS])
        rwkv_states.append(s_rwkv)
        hgrn_states.append(s_hgrn)
    return x, jnp.stack(shifts), jnp.stack(rwkv_states), jnp.stack(hgrn_states)


def kernel(x_prompt, x_sample, state_shift, state_rwkv, state_hgrn, norm_mix, w_in, rwkv_mu, rwkv_w0, rwkv_w2, rwkv_a0, rwkv_a2, rwkv_g2, rwkv_v0, rwkv_vres_w1, rwkv_vres_w2, rwkv_k_k, rwkv_k_a, rwkv_r_k, rwkv_ln_w, rwkv_ln_b, hgrn_lb_logits, hgrn_norm_w, w_out_a, w_out_b, w_out, norm_ffn, w_ffn_up, w_ffn_down, norm_final):
    p = dict(norm_mix=norm_mix, w_in=w_in, rwkv_mu=rwkv_mu, rwkv_w0=rwkv_w0, rwkv_w2=rwkv_w2, rwkv_a0=rwkv_a0,
             rwkv_a2=rwkv_a2, rwkv_g2=rwkv_g2, rwkv_v0=rwkv_v0, rwkv_vres_w1=rwkv_vres_w1,
             rwkv_vres_w2=rwkv_vres_w2, rwkv_k_k=rwkv_k_k, rwkv_k_a=rwkv_k_a, rwkv_r_k=rwkv_r_k,
             rwkv_ln_w=rwkv_ln_w, rwkv_ln_b=rwkv_ln_b, hgrn_lb_logits=hgrn_lb_logits, hgrn_norm_w=hgrn_norm_w,
             w_out_a=w_out_a, w_out_b=w_out_b, w_out=w_out, norm_ffn=norm_ffn, w_ffn_up=w_ffn_up,
             w_ffn_down=w_ffn_down, norm_final=norm_final)
    layers = [_prep_layer(l, p) for l in range(DEPTH)]
    weights = _prep_weights(p)
    bp = x_prompt.shape[0]
    dt = x_prompt.dtype
    zero_shift = jnp.zeros((DEPTH, bp, RWKV_COLS), dt)
    zero_rwkv = jnp.zeros((DEPTH, bp, A_HEADS, A_HEAD_DIM, A_HEAD_DIM), dt)
    zero_hgrn = jnp.zeros((DEPTH, bp, B_HEADS, B_KEY_DIM, B_VAL_DIM), dt)
    y_prompt, shift_p, rwkv_p, hgrn_p = _trunk(x_prompt, zero_shift, zero_rwkv, zero_hgrn, layers, weights, p)
    y_sample, shift_s, rwkv_s, hgrn_s = _trunk(x_sample, state_shift, state_rwkv, state_hgrn, layers, weights, p)
    return (y_prompt, y_sample, shift_p, rwkv_p, hgrn_p, shift_s, rwkv_s, hgrn_s)
```
